```python
import jax, jax.numpy as jnp
from jax import lax
import numpy as np

D_MODEL = 1024
BATCH = 1
SEQ = 16384
DEPTH = 1
DEC_BATCH = 32
DEC_SEQ = 1
PAST_LEN = 16384
PAGE_SIZE = 128

HEAD_DIM = 64
N_ATT_HEADS = (D_MODEL // 2) // HEAD_DIM
N_KV_HEADS = 2
GQA_REP = N_ATT_HEADS // N_KV_HEADS
D_ATT = N_ATT_HEADS * HEAD_DIM
CMP_BLOCK = 32
CMP_STRIDE = 16
CMP_HIDDEN = 2 * HEAD_DIM
SLC_BLOCK = 64
N_SELECT = 16
WINDOW = 512
Q_BLOCK = 128
FORCE_BONUS = 1000.0
N_MLSTM_HEADS = 4
MLSTM_HEAD_DIM = (D_MODEL // 2) // N_MLSTM_HEADS
D_MLSTM = N_MLSTM_HEADS * MLSTM_HEAD_DIM
MLSTM_CHUNK = 64
D_MIX = D_ATT + D_MLSTM
KV_COLS = 2 * N_KV_HEADS * HEAD_DIM
SPLIT_SIZES = (D_ATT, KV_COLS, KV_COLS, KV_COLS, 3 * N_ATT_HEADS, D_MLSTM, D_MLSTM, D_MLSTM, D_MLSTM, N_MLSTM_HEADS, N_MLSTM_HEADS)
D_IN = sum(SPLIT_SIZES)
PEER_N_KEYS = 128
PEER_N_EXPERTS = PEER_N_KEYS * PEER_N_KEYS
PEER_HEADS = 8
PEER_TOPK = 16
PEER_D_KEY = 128
PEER_BLOCK = 128
NORM_EPS = 1e-6
NEG_INF = -1e30

kernel_name = 'hymba_nsa_mlstm_peer_step'


def rmsnorm(x, w):
    xf = x.astype(jnp.float32)
    y = xf * lax.rsqrt(jnp.mean(xf * xf, axis=-1, keepdims=True) + NORM_EPS)
    return (y * w.astype(jnp.float32)).astype(x.dtype)


def alibi_slopes(n_heads):
    return jnp.exp2(-8.0 * jnp.arange(1, n_heads + 1, dtype=jnp.float32) / n_heads)


def masked_softmax(s, mask):
    p = jax.nn.softmax(jnp.where(mask, s.astype(jnp.float32), NEG_INF), axis=-1)
    return p * jnp.any(mask, axis=-1, keepdims=True)


def stack_layers(per_layer, j):
    return jnp.stack([entry[j] for entry in per_layer], axis=0)


def mix_projections(xn, w_in, b_i, b_f):
    B, T, _ = xn.shape
    points = np.cumsum(SPLIT_SIZES)[:-1].tolist()
    q_att, kv_c, kv_s, kv_w, g_att, mq, mk, mv, mo, ig, fg = jnp.split(xn @ w_in, points, axis=-1)
    kv_shape = (B, T, 2, N_KV_HEADS, HEAD_DIM)
    mh_shape = (B, T, N_MLSTM_HEADS, MLSTM_HEAD_DIM)
    return (q_att.reshape(B, T, N_ATT_HEADS, HEAD_DIM), kv_c.reshape(kv_shape), kv_s.reshape(kv_shape),
            kv_w.reshape(kv_shape), jax.nn.sigmoid(g_att.reshape(B, T, N_ATT_HEADS, 3)),
            mq.reshape(mh_shape), mk.reshape(mh_shape), mv.reshape(mh_shape),
            jax.nn.sigmoid(mo.reshape(mh_shape)), ig + b_i, fg + b_f)


def compress_blocks(kv, pe, w1, w2):
    B, L = kv.shape[:2]
    chunks = kv.reshape(B, L // CMP_STRIDE, CMP_STRIDE, 2, N_KV_HEADS, HEAD_DIM)
    first = jnp.einsum('bnlcgd,cldh->bncgh', chunks, w1[:, :CMP_STRIDE])[:, :-1]
    second = jnp.einsum('bnlcgd,cldh->bncgh', chunks, w1[:, CMP_STRIDE:])[:, 1:]
    pe_term = jnp.einsum('lcd,cldh->ch', pe, w1)
    h = jax.nn.gelu(first + second + pe_term[:, None, :])
    return jnp.einsum('bncgh,chd->bncgd', h, w2)


def nsa_attend(q, q_pos, k_cmp, v_cmp, cmp_end, k_slc, v_slc, k_win, v_win, win_pos, gates, slopes):
    B, Tq = q.shape[:2]
    f32 = jnp.float32
    scale = HEAD_DIM ** -0.5
    qg = q.reshape(B, Tq, N_KV_HEADS, GQA_REP, HEAD_DIM)
    slope = slopes.reshape(N_KV_HEADS, GQA_REP)
    d_c = q_pos[:, None] - cmp_end[None, :]
    s_c = (jnp.einsum('btgrd,bngd->btgrn', qg, k_cmp).astype(f32) * scale
           - slope[:, :, None] * d_c[None, :, None, None, :].astype(f32))
    p_c = masked_softmax(s_c, (d_c >= 0)[None, :, None, None, :])
    o_c = jnp.einsum('btgrn,bngd->btgrd', p_c, v_cmp.astype(f32))
    n_slc = k_slc.shape[2]
    ratio = SLC_BLOCK // CMP_STRIDE
    p_pad = jnp.pad(p_c.sum(axis=3), ((0, 0), (0, 0), (0, 0), (1, 1)))
    p_slc = (p_pad[..., :ratio * n_slc].reshape(B, Tq, N_KV_HEADS, n_slc, ratio).sum(-1)
             + p_pad[..., ratio::ratio])
    blk = jnp.arange(n_slc)
    cur = (q_pos // SLC_BLOCK)[:, None]
    valid = blk[None, :] <= cur
    forced = (blk[None, :] == 0) | (blk[None, :] == cur) | (blk[None, :] == cur - 1)
    score = jnp.where(valid[:, None, :], p_slc + FORCE_BONUS * forced[:, None, :], -1.0)
    _, idx = lax.top_k(score, min(N_SELECT, n_slc))
    bi = jnp.arange(B)[:, None, None, None]
    gi = jnp.arange(N_KV_HEADS)[None, None, :, None]
    k_sel = k_slc[bi, gi, idx]
    v_sel = v_slc[bi, gi, idx]
    kpos = idx[..., None] * SLC_BLOCK + jnp.arange(SLC_BLOCK)
    d_s = q_pos[None, :, None, None, None] - kpos
    s_s = (jnp.einsum('btgrd,btgnkd->btgrnk', qg, k_sel).astype(f32) * scale
           - slope[:, :, None, None] * d_s[:, :, :, None].astype(f32))
    p_s = masked_softmax(s_s.reshape(B, Tq, N_KV_HEADS, GQA_REP, -1),
                         (d_s >= 0).reshape(B, Tq, N_KV_HEADS, 1, -1))
    o_s = jnp.einsum('btgrm,btgmd->btgrd', p_s, v_sel.reshape(B, Tq, N_KV_HEADS, -1, HEAD_DIM).astype(f32))
    d_w = q_pos[:, None] - win_pos[None, :]
    m_w = (d_w >= 0) & (d_w < WINDOW) & (win_pos >= 0)[None, :]
    s_w = (jnp.einsum('btgrd,bkgd->btgrk', qg, k_win).astype(f32) * scale
           - slope[:, :, None] * d_w[None, :, None, None, :].astype(f32))
    p_w = masked_softmax(s_w, m_w[None, :, None, None, :])
    o_w = jnp.einsum('btgrk,bkgd->btgrd', p_w, v_win.astype(f32))
    g = gates.reshape(B, Tq, N_KV_HEADS, GQA_REP, 3).astype(f32)
    o = g[..., 0:1] * o_c + g[..., 1:2] * o_s + g[..., 2:3] * o_w
    return o.reshape(B, Tq, D_ATT).astype(q.dtype)


def nsa_prompt(q, kv_cmp, kv_slc, kv_win, gates, pe, w1, w2, slopes):
    B, T = q.shape[:2]
    kv_c = compress_blocks(kv_cmp, pe, w1, w2)
    k_cmp, v_cmp = kv_c[:, :, 0], kv_c[:, :, 1]
    cmp_end = jnp.arange(kv_c.shape[1]) * CMP_STRIDE + (CMP_BLOCK - 1)
    n_slc = T // SLC_BLOCK
    slc = kv_slc.reshape(B, n_slc, SLC_BLOCK, 2, N_KV_HEADS, HEAD_DIM).transpose(3, 0, 4, 1, 2, 5)
    k_slc, v_slc = slc[0], slc[1]
    win_pad = jnp.pad(kv_win, ((0, 0), (WINDOW, 0), (0, 0), (0, 0), (0, 0)))
    k_wp, v_wp = win_pad[:, :, 0], win_pad[:, :, 1]

    def one_block(start):
        qb = lax.dynamic_slice_in_dim(q, start, Q_BLOCK, axis=1)
        gb = lax.dynamic_slice_in_dim(gates, start, Q_BLOCK, axis=1)
        kw = lax.dynamic_slice_in_dim(k_wp, start, WINDOW + Q_BLOCK, axis=1)
        vw = lax.dynamic_slice_in_dim(v_wp, start, WINDOW + Q_BLOCK, axis=1)
        q_pos = start + jnp.arange(Q_BLOCK)
        win_pos = start - WINDOW + jnp.arange(WINDOW + Q_BLOCK)
        return nsa_attend(qb, q_pos, k_cmp, v_cmp, cmp_end, k_slc, v_slc, kw, vw, win_pos, gb, slopes)

    out = lax.map(one_block, jnp.arange(T // Q_BLOCK) * Q_BLOCK)
    att = out.transpose(1, 0, 2, 3).reshape(B, T, D_ATT)
    return att, kv_win[:, -min(WINDOW, T):]


def nsa_sample(q, kv_cmp, kv_slc, kv_win, gates, pool_cmp, pool_slc, win_buf, page_table, pe, w1, w2, slopes):
    B, T = q.shape[:2]
    past = page_table.shape[1] * pool_cmp.shape[1]
    total = past + T
    padded = -(-total // SLC_BLOCK) * SLC_BLOCK

    def full_rows(pool, new):
        old = pool[page_table].reshape((B, past) + pool.shape[2:])
        rows = jnp.concatenate([old, new.astype(old.dtype)], axis=1)
        return jnp.pad(rows, ((0, 0), (0, padded - total), (0, 0), (0, 0), (0, 0)))

    kv_c = compress_blocks(full_rows(pool_cmp, kv_cmp), pe, w1, w2)
    cmp_end = jnp.arange(kv_c.shape[1]) * CMP_STRIDE + (CMP_BLOCK - 1)
    n_slc = padded // SLC_BLOCK
    slc = full_rows(pool_slc, kv_slc).reshape(B, n_slc, SLC_BLOCK, 2, N_KV_HEADS, HEAD_DIM).transpose(3, 0, 4, 1, 2, 5)
    w_buf = win_buf.shape[1]
    win_all = jnp.concatenate([win_buf, kv_win.astype(win_buf.dtype)], axis=1)
    win_pos = past - w_buf + jnp.arange(w_buf + T)
    q_pos = past + jnp.arange(T)
    att = nsa_attend(q, q_pos, kv_c[:, :, 0], kv_c[:, :, 1], cmp_end, slc[0], slc[1],
                     win_all[:, :, 0], win_all[:, :, 1], win_pos, gates, slopes)
    return att, win_all[:, -min(WINDOW, w_buf + T):]


def mlstm(q, k, v, i_pre, f_pre, C0, n0, m0):
    B, T, NH, DH = q.shape
    f32 = jnp.float32
    chunk = MLSTM_CHUNK if T % MLSTM_CHUNK == 0 else T
    nc = T // chunk

    def to_chunks(a):
        a = a.astype(f32)
        return a.reshape((B, nc, chunk) + a.shape[2:]).swapaxes(0, 1)

    xs = (to_chunks(q), to_chunks(k) * (DH ** -0.5), to_chunks(v), to_chunks(i_pre),
          to_chunks(jax.nn.log_sigmoid(f_pre.astype(f32))))
    causal = jnp.tril(jnp.ones((chunk, chunk), dtype=bool))

    def step(carry, inp):
        C, n, m = carry
        qc, kc, vc, ic, lfc = inp
        F = jnp.cumsum(lfc, axis=1)
        log_d = F[:, :, None, :] - F[:, None, :, :] + ic[:, None, :, :]
        log_d = jnp.where(causal[None, :, :, None], log_d, -jnp.inf)
        m_t = jnp.maximum(F + m[:, None, :], jnp.max(log_d, axis=2))
        w = jnp.exp(log_d - m_t[:, :, None, :]) * jnp.einsum('bthd,bshd->btsh', qc, kc)
        inter = jnp.exp(F + m[:, None, :] - m_t)
        num = jnp.einsum('btsh,bshe->bthe', w, vc) + inter[..., None] * jnp.einsum('bthd,bhde->bthe', qc, C)
        den = jnp.sum(w, axis=2) + inter * jnp.einsum('bthd,bhd->bth', qc, n)
        h = num / jnp.maximum(jnp.abs(den), jnp.exp(-m_t))[..., None]
        f_tot = F[:, -1]
        m_new = m_t[:, -1]
        ws = jnp.exp(f_tot[:, None, :] - F + ic - m_new[:, None, :])
        decay = jnp.exp(f_tot + m - m_new)
        C_new = decay[:, :, None, None] * C + jnp.einsum('bsh,bshd,bshe->bhde', ws, kc, vc)
        n_new = decay[:, :, None] * n + jnp.einsum('bsh,bshd->bhd', ws, kc)
        return (C_new, n_new, m_new), h

    (C, n, m), h = lax.scan(step, (C0.astype(f32), n0.astype(f32), m0.astype(f32)), xs)
    h = h.swapaxes(0, 1).reshape(B, T, NH, DH).astype(q.dtype)
    return h, C, n, m


def mix_output(att, mh, mo, g_att, g_ml, w_out):
    B, T = att.shape[:2]
    a = rmsnorm(att.reshape(B, T, N_ATT_HEADS, HEAD_DIM), g_att.reshape(N_ATT_HEADS, HEAD_DIM))
    m = mo * rmsnorm(mh, g_ml.reshape(N_MLSTM_HEADS, MLSTM_HEAD_DIM))
    cat = jnp.concatenate([a.reshape(B, T, D_ATT), m.reshape(B, T, D_MLSTM).astype(a.dtype)], axis=-1)
    return cat @ w_out


def peer(x, w_q, sub_keys, u, v):
    N, D = x.shape
    n_pad = -(-N // PEER_BLOCK) * PEER_BLOCK
    xb_all = jnp.pad(x, ((0, n_pad - N), (0, 0))).reshape(n_pad // PEER_BLOCK, PEER_BLOCK, D)
    K = PEER_TOPK

    def block(xb):
        qh = (xb @ w_q).reshape(PEER_BLOCK, PEER_HEADS, 2, PEER_D_KEY // 2)
        s = jnp.einsum('nhcd,hckd->nhck', qh, sub_keys).astype(jnp.float32)
        sv, si = lax.top_k(s, K)
        cand = (sv[:, :, 0, :, None] + sv[:, :, 1, None, :]).reshape(PEER_BLOCK, PEER_HEADS, K * K)
        cidx = (si[:, :, 0, :, None] * PEER_N_KEYS + si[:, :, 1, None, :]).reshape(PEER_BLOCK, PEER_HEADS, K * K)
        best, pos = lax.top_k(cand, K)
        eidx = jnp.take_along_axis(cidx, pos, axis=-1)
        g = jax.nn.softmax(best, axis=-1)
        ue = u[eidx]
        ve = v[eidx]
        act = jax.nn.gelu(jnp.einsum('nd,nhkd->nhk', xb, ue).astype(jnp.float32))
        return jnp.einsum('nhk,nhkd->nd', (g * act).astype(ve.dtype), ve)

    return lax.map(block, xb_all).reshape(n_pad, D)[:N]


def channel_mix(h, g, w_q, sub_keys, u, v):
    B, T, D = h.shape
    xn = rmsnorm(h, g).reshape(B * T, D)
    return peer(xn, w_q, sub_keys, u, v).reshape(B, T, D).astype(h.dtype)


def setup_inputs(seed: int = 0) -> dict:
    key = jax.random.key(seed)
    ks = jax.random.split(key, 32)
    f32 = jnp.float32
    n_pages = PAST_LEN // PAGE_SIZE
    n_pool = (5 * DEC_BATCH * n_pages + 3) // 4
    w_buf = min(WINDOW, PAST_LEN)
    row = (2, N_KV_HEADS, HEAD_DIM)
    NH, DH = N_MLSTM_HEADS, MLSTM_HEAD_DIM

    def nrm(k, shape, scale):
        return jax.random.normal(k, shape, f32) * scale

    page_table = jax.random.permutation(ks[0], n_pool)[: DEC_BATCH * n_pages].reshape(DEC_BATCH, n_pages).astype(jnp.int32)
    return {
        'x_prompt': nrm(ks[1], (BATCH, SEQ, D_MODEL), 1.0),
        'x_sample': nrm(ks[2], (DEC_BATCH, DEC_SEQ, D_MODEL), 1.0),
        'cache_cmp_kv': nrm(ks[3], (DEPTH, n_pool, PAGE_SIZE) + row, 1.0),
        'cache_slc_kv': nrm(ks[4], (DEPTH, n_pool, PAGE_SIZE) + row, 1.0),
        'cache_win_kv': nrm(ks[5], (DEPTH, DEC_BATCH, w_buf) + row, 1.0),
        'state_mlstm_C': nrm(ks[6], (DEPTH, DEC_BATCH, NH, DH, DH), 0.1),
        'state_mlstm_n': nrm(ks[7], (DEPTH, DEC_BATCH, NH, DH), 1.0),
        'state_mlstm_m': nrm(ks[8], (DEPTH, DEC_BATCH, NH), 0.5),
        'page_table': page_table,
        'norm_mix': 1.0 + nrm(ks[9], (DEPTH, D_MODEL), 0.02),
        'w_in': nrm(ks[10], (DEPTH, D_MODEL, D_IN), D_MODEL ** -0.5),
        'b_igate': nrm(ks[11], (DEPTH, NH), 0.1),
        'b_fgate': jnp.linspace(3.0, 6.0, NH, dtype=f32)[None, :] + nrm(ks[12], (DEPTH, NH), 0.1),
        'pe_cmp': nrm(ks[13], (DEPTH, CMP_BLOCK, 2, HEAD_DIM), 0.1),
        'w_cmp1': nrm(ks[14], (DEPTH, 2, CMP_BLOCK, HEAD_DIM, CMP_HIDDEN), (CMP_BLOCK * HEAD_DIM) ** -0.5),
        'w_cmp2': nrm(ks[15], (DEPTH, 2, CMP_HIDDEN, HEAD_DIM), CMP_HIDDEN ** -0.5),
        'norm_att_out': 1.0 + nrm(ks[16], (DEPTH, D_ATT), 0.02),
        'norm_mlstm_out': 1.0 + nrm(ks[17], (DEPTH, D_MLSTM), 0.02),
        'w_out': nrm(ks[18], (DEPTH, D_MIX, D_MODEL), D_MIX ** -0.5),
        'norm_ffn': 1.0 + nrm(ks[19], (DEPTH, D_MODEL), 0.02),
        'peer_wq': nrm(ks[20], (DEPTH, D_MODEL, PEER_HEADS * PEER_D_KEY), D_MODEL ** -0.5),
        'peer_subkeys': nrm(ks[21], (DEPTH, PEER_HEADS, 2, PEER_N_KEYS, PEER_D_KEY // 2), (PEER_D_KEY // 2) ** -0.5),
        'peer_u': nrm(ks[22], (DEPTH, PEER_N_EXPERTS, D_MODEL), D_MODEL ** -0.5),
        'peer_v': nrm(ks[23], (DEPTH, PEER_N_EXPERTS, D_MODEL), 0.5),
        'norm_final': 1.0 + nrm(ks[24], (D_MODEL,), 0.02),
    }


def reference(x_prompt, x_sample, cache_cmp_kv, cache_slc_kv, cache_win_kv, state_mlstm_C, state_mlstm_n,
              state_mlstm_m, page_table, norm_mix, w_in, b_igate, b_fgate, pe_cmp, w_cmp1, w_cmp2,
              norm_att_out, norm_mlstm_out, w_out, norm_ffn, peer_wq, peer_subkeys, peer_u, peer_v, norm_final):
    slopes = alibi_slopes(N_ATT_HEADS)
    h_p, h_s = x_prompt, x_sample
    bp = x_prompt.shape[0]
    st_p, st_s = [], []
    for l in range(DEPTH):
        xn = rmsnorm(h_p, norm_mix[l])
        qa, kvc, kvs, kvw, ga, mq, mk, mv, mo, ig, fg = mix_projections(xn, w_in[l], b_igate[l], b_fgate[l])
        att, win_new = nsa_prompt(qa, kvc, kvs, kvw, ga, pe_cmp[l], w_cmp1[l], w_cmp2[l], slopes)
        C0 = jnp.zeros((bp, N_MLSTM_HEADS, MLSTM_HEAD_DIM, MLSTM_HEAD_DIM), jnp.float32)
        n0 = jnp.zeros((bp, N_MLSTM_HEADS, MLSTM_HEAD_DIM), jnp.float32)
        m0 = jnp.zeros((bp, N_MLSTM_HEADS), jnp.float32)
        mh, C1, n1, m1 = mlstm(mq, mk, mv, ig, fg, C0, n0, m0)
        h_p = h_p + mix_output(att, mh, mo, norm_att_out[l], norm_mlstm_out[l], w_out[l])
        h_p = h_p + channel_mix(h_p, norm_ffn[l], peer_wq[l], peer_subkeys[l], peer_u[l], peer_v[l])
        dt = h_p.dtype
        st_p.append((kvc, kvs, win_new, C1.astype(dt), n1.astype(dt), m1.astype(dt)))
        xn = rmsnorm(h_s, norm_mix[l])
        qa, kvc, kvs, kvw, ga, mq, mk, mv, mo, ig, fg = mix_projections(xn, w_in[l], b_igate[l], b_fgate[l])
        att, win_new = nsa_sample(qa, kvc, kvs, kvw, ga, cache_cmp_kv[l], cache_slc_kv[l], cache_win_kv[l],
                                  page_table, pe_cmp[l], w_cmp1[l], w_cmp2[l], slopes)
        mh, C1, n1, m1 = mlstm(mq, mk, mv, ig, fg, state_mlstm_C[l], state_mlstm_n[l], state_mlstm_m[l])
        h_s = h_s + mix_output(att, mh, mo, norm_att_out[l], norm_mlstm_out[l], w_out[l])
        h_s = h_s + channel_mix(h_s, norm_ffn[l], peer_wq[l], peer_subkeys[l], peer_u[l], peer_v[l])
        st_s.append((kvc, kvs, win_new, C1.astype(state_mlstm_C.dtype), n1.astype(state_mlstm_n.dtype),
                     m1.astype(state_mlstm_m.dtype)))
    y_prompt = rmsnorm(h_p, norm_final)
    y_sample = rmsnorm(h_s, norm_final)
    return (y_prompt, y_sample,
            stack_layers(st_p, 0), stack_layers(st_s, 0),
            stack_layers(st_p, 1), stack_layers(st_s, 1),
            stack_layers(st_p, 2), stack_layers(st_s, 2),
            stack_layers(st_p, 3), stack_layers(st_s, 3),
            stack_layers(st_p, 4), stack_layers(st_s, 4),
            stack_layers(st_p, 5), stack_layers(st_s, 5))
```

```python
import functools

import jax
import jax.numpy as jnp
import numpy as np
from jax import lax
from jax.experimental import pallas as pl
from jax.experimental.pallas import tpu as pltpu

f32 = jnp.float32
bf16 = jnp.bfloat16
HIGHEST = lax.Precision.HIGHEST

D_MODEL = 1024
HEAD_DIM = 64
N_ATT_HEADS = 8
N_KV_HEADS = 2
GQA_REP = 4
D_ATT = 512
KV_COLS = 256
CMP_BLOCK = 32
CMP_STRIDE = 16
CMP_HIDDEN = 128
SLC_BLOCK = 64
N_SELECT = 16
WINDOW = 512
Q_BLOCK = 128
FORCE_BONUS = 1000.0
N_MLSTM_HEADS = 4
MLSTM_HEAD_DIM = 128
D_MLSTM = 512
N_GATE_COLS = 3 * N_ATT_HEADS
SPLIT_SIZES = (D_ATT, KV_COLS, KV_COLS, KV_COLS, N_GATE_COLS, D_MLSTM, D_MLSTM, D_MLSTM, D_MLSTM,
               N_MLSTM_HEADS, N_MLSTM_HEADS)
SMALL_COLS = 128
PEER_N_KEYS = 128
PEER_HEADS = 8
PEER_TOPK = 16
PEER_D_KEY = 128
NORM_EPS = 1e-6
NEG_INF = -1e30
VMEM_LIMIT = 56 * 1024 * 1024


def _cparams(*sem):
    return pltpu.CompilerParams(dimension_semantics=sem, vmem_limit_bytes=VMEM_LIMIT)


def _rms(x, w):
    return x * lax.rsqrt(jnp.mean(x * x, axis=-1, keepdims=True) + NORM_EPS) * w


_P_Q, _P_KVC, _P_KVS, _P_KVW, _P_MQ, _P_MK, _P_MV, _P_MO, _P_SM, _P_END = (
    0, 512, 768, 1024, 1280, 1792, 2304, 2816, 3328, 3456)


def _relayout_w_in(w_in):
    pts = np.cumsum(SPLIT_SIZES)[:-1].tolist()
    q, kc, ks, kw, ga, mq, mk, mv, mo, ig, fg = jnp.split(w_in, pts, axis=-1)
    pad = jnp.zeros((w_in.shape[0], SMALL_COLS - N_GATE_COLS - 2 * N_MLSTM_HEADS), w_in.dtype)
    return jnp.concatenate([q, kc, ks, kw, mq, mk, mv, mo, ga, ig, fg, pad], axis=-1).astype(bf16)


def _bias_row(b_i, b_f):
    pad = jnp.zeros((SMALL_COLS - N_GATE_COLS - 2 * N_MLSTM_HEADS,), f32)
    return jnp.concatenate([jnp.zeros((N_GATE_COLS,), f32), b_i, b_f, pad])[None, :]


def _proj_kernel(x_ref, nw_ref, w_ref, b_ref, q_ref, kvc_ref, kvs_ref, kvw_ref, mq_ref, mk_ref, mv_ref,
                 mo_ref, sm_ref):
    xb = _rms(x_ref[...], nw_ref[...]).astype(bf16)

    def mm(lo, hi):
        return jnp.dot(xb, w_ref[:, lo:hi], preferred_element_type=f32)

    q_ref[...] = mm(_P_Q, _P_KVC)
    kvc_ref[...] = mm(_P_KVC, _P_KVS)
    kvs_ref[...] = mm(_P_KVS, _P_KVW)
    kvw_ref[...] = mm(_P_KVW, _P_MQ)
    mq_ref[...] = mm(_P_MQ, _P_MK)
    mk_ref[...] = mm(_P_MK, _P_MV)
    mv_ref[...] = mm(_P_MV, _P_MO)
    mo_ref[...] = jax.nn.sigmoid(mm(_P_MO, _P_SM))
    s = mm(_P_SM, _P_END) + b_ref[...]
    col = lax.broadcasted_iota(jnp.int32, s.shape, 1)
    sm_ref[...] = jnp.where(col < N_GATE_COLS, jax.nn.sigmoid(s), s)


def _project(x, norm_w, w_re, bias_row, tm):
    n = x.shape[0]
    widths = (D_ATT, KV_COLS, KV_COLS, KV_COLS, D_MLSTM, D_MLSTM, D_MLSTM, D_MLSTM, SMALL_COLS)
    return pl.pallas_call(
        _proj_kernel,
        grid=(n // tm,),
        in_specs=[pl.BlockSpec((tm, D_MODEL), lambda i: (i, 0)),
                  pl.BlockSpec((1, D_MODEL), lambda i: (0, 0)),
                  pl.BlockSpec((D_MODEL, _P_END), lambda i: (0, 0)),
                  pl.BlockSpec((1, SMALL_COLS), lambda i: (0, 0))],
        out_specs=[pl.BlockSpec((tm, w), lambda i: (i, 0)) for w in widths],
        out_shape=[jax.ShapeDtypeStruct((n, w), f32) for w in widths],
        compiler_params=_cparams("parallel"),
        name="proj",
    )(x, norm_w, w_re, bias_row)


_CHUNK_COLS = CMP_STRIDE * KV_COLS
_HID_COLS = 2 * N_KV_HEADS * CMP_HIDDEN


def _compress_weights(pe, w1, w2):
    eye = jnp.eye(2, dtype=f32)

    def big1(w):
        return jnp.einsum('cldh,ce,gf->lefdcgh', w, eye, eye).reshape(_CHUNK_COLS, _HID_COLS).astype(bf16)

    def pe_row(p):
        return jnp.broadcast_to(p[:, :, None, :], (CMP_STRIDE, 2, N_KV_HEADS, HEAD_DIM)).reshape(1, _CHUNK_COLS)

    w2_big = jnp.einsum('chd,ce,gf->efhcgd', w2, eye, eye).reshape(_HID_COLS, KV_COLS).astype(bf16)
    pe_rows = jnp.concatenate([pe_row(pe[:CMP_STRIDE]), jnp.zeros((7, _CHUNK_COLS), f32),
                               pe_row(pe[CMP_STRIDE:]), jnp.zeros((7, _CHUNK_COLS), f32)], axis=0).astype(bf16)
    return big1(w1[:, :CMP_STRIDE]), big1(w1[:, CMP_STRIDE:]), pe_rows, w2_big


_CMP_ROWS = 512


def _compress_math(x_ref, w1a_ref, w1b_ref, pe_ref, w2_ref):
    n_chunks = x_ref.shape[0]
    step = min(_CMP_ROWS, n_chunks)
    first, second = [], []
    for r0 in range(0, n_chunks, step):
        x = x_ref[r0:r0 + step, :].astype(bf16)
        first.append(jnp.dot(x, w1a_ref[...], preferred_element_type=f32))
        second.append(jnp.dot(x, w1b_ref[...], preferred_element_type=f32))
    first = jnp.concatenate(first, axis=0)
    second = jnp.concatenate(second, axis=0)
    pe_term = (jnp.dot(pe_ref[0:8, :], w1a_ref[...], preferred_element_type=f32)
               + jnp.dot(pe_ref[8:16, :], w1b_ref[...], preferred_element_type=f32))[0:1]
    h = jax.nn.gelu(first + pltpu.roll(second, n_chunks - 1, 0) + pe_term)
    return jnp.dot(h.astype(bf16), w2_ref[...], preferred_element_type=f32)


def _compress_kernel(x_ref, w1a_ref, w1b_ref, pe_ref, w2_ref, o_ref):
    o_ref[0] = _compress_math(x_ref.at[0], w1a_ref, w1b_ref, pe_ref, w2_ref)


def _compress_paged_kernel(pt_ref, pool_ref, w1a_ref, w1b_ref, pe_ref, w2_ref, o_ref, x_s, sem):
    b = pl.program_id(0)
    n_pages = x_s.shape[0] // _PAGE_CHUNKS

    def page_copy(p):
        return pltpu.make_async_copy(pool_ref.at[pt_ref[b * n_pages + p]],
                                     x_s.at[pl.ds(pl.multiple_of(p * _PAGE_CHUNKS, _PAGE_CHUNKS), _PAGE_CHUNKS)],
                                     sem)

    def start(p, c):
        page_copy(p).start()
        return c

    def wait(p, c):
        page_copy(p).wait()
        return c

    lax.fori_loop(0, n_pages, start, 0)
    lax.fori_loop(0, n_pages, wait, 0)
    o_ref[0] = _compress_math(x_s, w1a_ref, w1b_ref, pe_ref, w2_ref)


_PAGE_CHUNKS = 8


def _compress_paged(page_table, pool_chunks, cw):
    nb, n_pages = page_table.shape
    n_chunks = n_pages * _PAGE_CHUNKS
    full = lambda a: pl.BlockSpec(a.shape, lambda b, pt: (0,) * a.ndim)
    return pl.pallas_call(
        _compress_paged_kernel,
        grid_spec=pltpu.PrefetchScalarGridSpec(
            num_scalar_prefetch=1,
            grid=(nb,),
            in_specs=[pl.BlockSpec(memory_space=pl.ANY)] + [full(a) for a in cw],
            out_specs=pl.BlockSpec((1, n_chunks, KV_COLS), lambda b, pt: (b, 0, 0)),
            scratch_shapes=[pltpu.VMEM((n_chunks, _CHUNK_COLS), f32), pltpu.SemaphoreType.DMA(())]),
        out_shape=jax.ShapeDtypeStruct((nb, n_chunks, KV_COLS), f32),
        compiler_params=_cparams("arbitrary"),
        name="compress_paged",
    )(page_table.reshape(-1), pool_chunks, *cw)


def _compress_prompt(kv_chunks, cw):
    nb, n_chunks, _ = kv_chunks.shape
    full = lambda a: pl.BlockSpec(a.shape, lambda b: (0,) * a.ndim)
    return pl.pallas_call(
        _compress_kernel,
        grid=(nb,),
        in_specs=[pl.BlockSpec((1, n_chunks, _CHUNK_COLS), lambda b: (b, 0, 0))] + [full(a) for a in cw],
        out_specs=pl.BlockSpec((1, n_chunks, KV_COLS), lambda b: (b, 0, 0)),
        out_shape=jax.ShapeDtypeStruct((nb, n_chunks, KV_COLS), f32),
        compiler_params=_cparams("parallel"),
        name="compress_prompt",
    )(kv_chunks, *cw)


def _dot_t0(a, b, **kw):
    return lax.dot_general(a, b, (((0,), (0,)), ((), ())), preferred_element_type=f32, **kw)


def _dot_nt(a, b, **kw):
    return lax.dot_general(a, b, (((1,), (1,)), ((), ())), preferred_element_type=f32, **kw)


def _mlstm_chunk_kernel(q_ref, k_ref, v_ref, gc_ref, gr_ref, h_ref, c_out, n_out, m_out, c_s, n_s, m_s):
    L = q_ref.shape[0]
    NH, DH = N_MLSTM_HEADS, MLSTM_HEAD_DIM

    @pl.when(pl.program_id(0) == 0)
    def _():
        c_s[...] = jnp.zeros_like(c_s)
        n_s[...] = jnp.zeros_like(n_s)
        m_s[...] = jnp.zeros_like(m_s)

    row = lax.broadcasted_iota(jnp.int32, (L, L), 0)
    col = lax.broadcasted_iota(jnp.int32, (L, L), 1)
    causal = col <= row
    gc = gc_ref[...]
    gr = gr_ref[...]
    lf_c = jax.nn.log_sigmoid(gc[:, NH:2 * NH])
    lf_r = jax.nn.log_sigmoid(gr[NH:2 * NH, :])
    f_c = jnp.dot(causal.astype(f32), lf_c, preferred_element_type=f32, precision=HIGHEST)
    f_r = jnp.dot(lf_r, (row <= col).astype(f32), preferred_element_type=f32, precision=HIGHEST)
    for h in range(NH):
        sl = slice(h * DH, (h + 1) * DH)
        fc, fr = f_c[:, h:h + 1], f_r[h:h + 1, :]
        ic, ir = gc[:, h:h + 1], gr[h:h + 1, :]
        m_prev = m_s[h:h + 1, 0:1]
        qh = q_ref[:, sl].astype(bf16)
        kh = k_ref[:, sl] * (DH ** -0.5)
        vh = v_ref[:, sl].astype(bf16)
        log_d = fc - fr + ir
        m_t = jnp.maximum(fc + m_prev, jnp.max(jnp.where(causal, log_d, NEG_INF), axis=1, keepdims=True))
        w = jnp.where(causal, jnp.exp(log_d - m_t), 0.0) * _dot_nt(qh, kh.astype(bf16))
        inter = jnp.exp(fc + m_prev - m_t)
        c_old = c_s[h]
        n_old = n_s[h:h + 1, :]
        num = (jnp.dot(w.astype(bf16), vh, preferred_element_type=f32)
               + inter * jnp.dot(qh, c_old.astype(bf16), preferred_element_type=f32))
        den = (jnp.sum(w, axis=1, keepdims=True)
               + inter * jnp.sum(q_ref[:, sl] * n_old, axis=1, keepdims=True))
        h_ref[:, sl] = num / jnp.maximum(jnp.abs(den), jnp.exp(-m_t))
        f_tot = fc[L - 1:L, :]
        m_new = m_t[L - 1:L, :]
        kw = kh * jnp.exp(f_tot - fc + ic - m_new)
        decay = jnp.exp(f_tot + m_prev - m_new)
        c_new = decay * c_old + _dot_t0(kw.astype(bf16), vh)
        n_new = decay * n_old + jnp.sum(kw, axis=0, keepdims=True)
        c_s[h] = c_new
        n_s[h:h + 1, :] = n_new
        m_s[h:h + 1, :] = jnp.broadcast_to(m_new, (1, DH))
        c_out[h] = c_new
    n_out[...] = n_s[...]
    m_out[...] = m_s[...]


def _mlstm_prompt(mq, mk, mv, gates_c, gates_r, chunk):
    t = mq.shape[0]
    NH, DH = N_MLSTM_HEADS, MLSTM_HEAD_DIM
    tok = lambda w: pl.BlockSpec((chunk, w), lambda c: (c, 0))
    return pl.pallas_call(
        _mlstm_chunk_kernel,
        grid=(t // chunk,),
        in_specs=[tok(D_MLSTM), tok(D_MLSTM), tok(D_MLSTM), tok(2 * NH),
                  pl.BlockSpec((2 * NH, chunk), lambda c: (0, c))],
        out_specs=[tok(D_MLSTM),
                   pl.BlockSpec((NH, DH, DH), lambda c: (0, 0, 0)),
                   pl.BlockSpec((8, DH), lambda c: (0, 0)),
                   pl.BlockSpec((8, DH), lambda c: (0, 0))],
        out_shape=[jax.ShapeDtypeStruct((t, D_MLSTM), f32),
                   jax.ShapeDtypeStruct((NH, DH, DH), f32),
                   jax.ShapeDtypeStruct((8, DH), f32),
                   jax.ShapeDtypeStruct((8, DH), f32)],
        scratch_shapes=[pltpu.VMEM((NH, DH, DH), f32), pltpu.VMEM((8, DH), f32), pltpu.VMEM((8, DH), f32)],
        compiler_params=_cparams("arbitrary"),
        name="mlstm_prompt",
    )(mq, mk, mv, gates_c, gates_r)


def _mlstm_step_kernel(q_ref, k_ref, v_ref, g_ref, c_ref, n_ref, m_ref, h_ref, c_out, n_out, m_out):
    NH, DH = N_MLSTM_HEADS, MLSTM_HEAD_DIM
    b = pl.program_id(0)
    row8 = lax.broadcasted_iota(jnp.int32, (8, DH), 0)
    g = g_ref[pl.ds(b, 1), :]
    m_row = m_ref[pl.ds(b, 1), :]
    m_new_row = jnp.zeros((1, NH), f32)
    lane4 = lax.broadcasted_iota(jnp.int32, (1, NH), 1)
    q_row, k_row, v_row = q_ref[pl.ds(b, 1), :], k_ref[pl.ds(b, 1), :], v_ref[pl.ds(b, 1), :]
    h_parts = []
    for h in range(NH):
        sl = slice(h * DH, (h + 1) * DH)
        q = q_row[:, sl]
        k = k_row[:, sl] * (DH ** -0.5)
        v = v_row[:, sl]
        ig = g[:, h:h + 1]
        lf = jax.nn.log_sigmoid(g[:, NH + h:NH + h + 1])
        m_prev = m_row[:, h:h + 1]
        c_old = c_ref[0, h]
        n_old = n_ref[0, h:h + 1, :]
        m_t = jnp.maximum(lf + m_prev, ig)
        w = jnp.exp(ig - m_t) * jnp.sum(q * k, axis=1, keepdims=True)
        inter = jnp.exp(lf + m_prev - m_t)
        q8 = jnp.where(row8 == 0, q, 0.0)
        qc = jnp.dot(q8, c_old, preferred_element_type=f32, precision=HIGHEST)[0:1]
        num = w * v + inter * qc
        den = w + inter * jnp.sum(q * n_old, axis=1, keepdims=True)
        h_parts.append(num / jnp.maximum(jnp.abs(den), jnp.exp(-m_t)))
        kw = k * jnp.exp(ig - m_t)
        decay = jnp.exp(lf + m_prev - m_t)
        kw8 = jnp.where(row8 == 0, kw, 0.0)
        v8 = jnp.where(row8 == 0, v, 0.0)
        c_out[0, h] = decay * c_old + _dot_t0(kw8, v8, precision=HIGHEST)
        n_out[0, h:h + 1, :] = decay * n_old + kw
        m_new_row = jnp.where(lane4 == h, m_t, m_new_row)
    h_ref[pl.ds(b, 1), :] = jnp.concatenate(h_parts, axis=1)
    m_out[pl.ds(b, 1), :] = m_new_row


def _mlstm_sample(mq, mk, mv, gates, c0, n0, m0):
    nb = mq.shape[0]
    NH, DH = N_MLSTM_HEADS, MLSTM_HEAD_DIM
    full = lambda a: pl.BlockSpec(a.shape, lambda b: (0,) * a.ndim)
    return pl.pallas_call(
        _mlstm_step_kernel,
        grid=(nb,),
        in_specs=[full(mq), full(mk), full(mv), full(gates),
                  pl.BlockSpec((1, NH, DH, DH), lambda b: (b, 0, 0, 0)),
                  pl.BlockSpec((1, NH, DH), lambda b: (b, 0, 0)),
                  full(m0)],
        out_specs=[pl.BlockSpec((nb, D_MLSTM), lambda b: (0, 0)),
                   pl.BlockSpec((1, NH, DH, DH), lambda b: (b, 0, 0, 0)),
                   pl.BlockSpec((1, NH, DH), lambda b: (b, 0, 0)),
                   pl.BlockSpec((nb, NH), lambda b: (0, 0))],
        out_shape=[jax.ShapeDtypeStruct((nb, D_MLSTM), f32),
                   jax.ShapeDtypeStruct((nb, NH, DH, DH), f32),
                   jax.ShapeDtypeStruct((nb, NH, DH), f32),
                   jax.ShapeDtypeStruct((nb, NH), f32)],
        compiler_params=_cparams("arbitrary"),
        name="mlstm_sample",
    )(mq, mk, mv, gates, c0, n0, m0)


SLC_TILE = 512


def _masked_softmax(s, mask):
    sm = jnp.where(mask, s, NEG_INF)
    mx = jnp.max(sm, axis=-1, keepdims=True)
    e = jnp.exp(sm - mx)
    p = e / jnp.sum(e, axis=-1, keepdims=True)
    return jnp.where(mx > 0.5 * NEG_INF, p, 0.0)


def _pool_matrix(n_cmp_rows, n_slc):
    i = np.arange(n_cmp_rows)[:, None]
    j = np.arange(n_slc)[None, :]
    ratio = SLC_BLOCK // CMP_STRIDE
    return jnp.asarray(((i >= ratio * j - 1) & (i <= ratio * j + ratio - 1)).astype(np.float32))


def _select_blocks(p_slc, cur, n_valid_lanes):
    nb = p_slc.shape[1]
    blk = lax.broadcasted_iota(jnp.int32, p_slc.shape, 1)
    valid = blk <= cur
    forced = (blk == 0) | (blk == cur) | (blk == cur - 1)
    score = jnp.where(valid, p_slc + jnp.where(forced, FORCE_BONUS, 0.0), -1.0)
    score = jnp.where(blk < n_valid_lanes, score, -2.0)
    sel = jnp.zeros(p_slc.shape, f32)
    picks = []
    for _ in range(N_SELECT):
        mx = jnp.max(score, axis=1, keepdims=True)
        idx = jnp.min(jnp.where(score == mx, blk, nb), axis=1, keepdims=True)
        hit = blk == idx
        sel = jnp.where(hit, 1.0, sel)
        score = jnp.where(hit, -3.0, score)
        picks.append(idx)
    return sel, picks


def _group_queries(q, g, scale):
    return jnp.concatenate(
        [q[:, (GQA_REP * g + r) * HEAD_DIM:(GQA_REP * g + r + 1) * HEAD_DIM] for r in range(GQA_REP)],
        axis=0) * scale


def _group_slopes(g, tq):
    r = lax.broadcasted_iota(jnp.int32, (GQA_REP * tq, 1), 0) // tq
    head1 = (GQA_REP * g + 1 + r).astype(f32)
    return jnp.exp2(-8.0 * head1 / N_ATT_HEADS)


def _nsa_prompt_kernel(q_ref, gate_ref, cmp_ref, pool_ref, kvs_ref, kvw_ref, o_ref):
    tq = q_ref.shape[0]
    n_cmp = cmp_ref.shape[0]
    n_slc = pool_ref.shape[1]
    i = pl.program_id(0)
    start = pl.multiple_of(i * tq, tq)
    R4 = GQA_REP * tq
    qpos = start + lax.broadcasted_iota(jnp.int32, (tq, 1), 0)
    qpos4 = jnp.concatenate([qpos] * GQA_REP, axis=0)
    q = q_ref[...]
    gates = gate_ref[...]
    n_tiles = (start + tq + SLC_TILE - 1) // SLC_TILE
    for g in range(N_KV_HEADS):
        kcol = slice(g * HEAD_DIM, (g + 1) * HEAD_DIM)
        vcol = slice((N_KV_HEADS + g) * HEAD_DIM, (N_KV_HEADS + g + 1) * HEAD_DIM)
        q4 = _group_queries(q, g, HEAD_DIM ** -0.5).astype(bf16)
        slope = _group_slopes(g, tq)

        cmp_end = lax.broadcasted_iota(jnp.int32, (1, n_cmp), 1) * CMP_STRIDE + (CMP_BLOCK - 1)
        d_c = qpos4 - cmp_end
        s_c = _dot_nt(q4, cmp_ref[:, kcol]) - slope * d_c.astype(f32)
        p_c = _masked_softmax(s_c, d_c >= 0)
        o_c = jnp.dot(p_c.astype(bf16), cmp_ref[:, vcol], preferred_element_type=f32)
        p_sum = p_c[0:tq] + p_c[tq:2 * tq] + p_c[2 * tq:3 * tq] + p_c[3 * tq:4 * tq]
        p_slc = jnp.dot(p_sum, pool_ref[...], preferred_element_type=f32, precision=HIGHEST)
        sel = _select_blocks(p_slc, qpos // SLC_BLOCK, n_slc)[0].astype(bf16)

        def tile_step(j, carry):
            m, l, acc = carry
            k0 = pl.multiple_of(j * SLC_TILE, SLC_TILE)
            kpos = k0 + lax.broadcasted_iota(jnp.int32, (1, SLC_TILE), 1)
            blk_of_key = k0 // SLC_BLOCK + lax.broadcasted_iota(jnp.int32, (n_slc, SLC_TILE), 1) // SLC_BLOCK
            expand = jnp.where(lax.broadcasted_iota(jnp.int32, (n_slc, SLC_TILE), 0) == blk_of_key, 1.0, 0.0)
            sel_k = jnp.dot(sel, expand.astype(bf16), preferred_element_type=f32)
            sel_k4 = jnp.concatenate([sel_k] * GQA_REP, axis=0)
            d_s = qpos4 - kpos
            mask = (sel_k4 > 0.5) & (d_s >= 0)
            s = _dot_nt(q4, kvs_ref[pl.ds(k0, SLC_TILE), kcol]) - slope * d_s.astype(f32)
            s = jnp.where(mask, s, NEG_INF)
            m_new = jnp.maximum(m, jnp.max(s, axis=1, keepdims=True))
            p = jnp.where(mask, jnp.exp(s - m_new), 0.0)
            alpha = jnp.exp(m - m_new)
            l_new = alpha * l + jnp.sum(p, axis=1, keepdims=True)
            acc_new = alpha * acc + jnp.dot(p.astype(bf16), kvs_ref[pl.ds(k0, SLC_TILE), vcol],
                                            preferred_element_type=f32)
            return m_new, l_new, acc_new

        m0 = jnp.full((R4, 1), NEG_INF, f32)
        l0 = jnp.zeros((R4, 1), f32)
        a0 = jnp.zeros((R4, HEAD_DIM), f32)
        _, l_s, acc_s = lax.fori_loop(0, n_tiles, tile_step, (m0, l0, a0))
        o_s = jnp.where(l_s > 0.0, acc_s / jnp.where(l_s > 0.0, l_s, 1.0), 0.0)

        n_win = WINDOW + tq
        win_pos = start - WINDOW + lax.broadcasted_iota(jnp.int32, (1, n_win), 1)
        d_w = qpos4 - win_pos
        m_w = (d_w >= 0) & (d_w < WINDOW) & (win_pos >= 0)
        s_w = _dot_nt(q4, kvw_ref[pl.ds(start, n_win), kcol]) - slope * d_w.astype(f32)
        p_w = _masked_softmax(s_w, m_w)
        o_w = jnp.dot(p_w.astype(bf16), kvw_ref[pl.ds(start, n_win), vcol], preferred_element_type=f32)

        for r in range(GQA_REP):
            head = GQA_REP * g + r
            rows = slice(r * tq, (r + 1) * tq)
            o_ref[:, head * HEAD_DIM:(head + 1) * HEAD_DIM] = (
                gates[:, 3 * head:3 * head + 1] * o_c[rows]
                + gates[:, 3 * head + 1:3 * head + 2] * o_s[rows]
                + gates[:, 3 * head + 2:3 * head + 3] * o_w[rows])


def _nsa_prompt(q, small, cmp_bf, pool, kvs_bf, kvw_pad_bf):
    t = q.shape[0]
    full = lambda a: pl.BlockSpec(a.shape, lambda i: (0,) * a.ndim)
    return pl.pallas_call(
        _nsa_prompt_kernel,
        grid=(t // Q_BLOCK,),
        in_specs=[pl.BlockSpec((Q_BLOCK, D_ATT), lambda i: (i, 0)),
                  pl.BlockSpec((Q_BLOCK, SMALL_COLS), lambda i: (i, 0)),
                  full(cmp_bf), full(pool), full(kvs_bf), full(kvw_pad_bf)],
        out_specs=pl.BlockSpec((Q_BLOCK, D_ATT), lambda i: (i, 0)),
        out_shape=jax.ShapeDtypeStruct((t, D_ATT), f32),
        compiler_params=_cparams("parallel"),
        name="nsa_prompt",
    )(q, small, cmp_bf, pool, kvs_bf, kvw_pad_bf)


_QROWS = 16
_IDS_LANES = 128
PAGE_ROWS = 128


def _group_queries_1(q, g):
    rows = [q[:, (GQA_REP * g + r) * HEAD_DIM:(GQA_REP * g + r + 1) * HEAD_DIM] for r in range(GQA_REP)]
    rows.append(jnp.zeros((_QROWS - GQA_REP, HEAD_DIM), f32))
    return jnp.concatenate(rows, axis=0) * (HEAD_DIM ** -0.5)


def _group_slopes_1(g):
    r = jnp.minimum(lax.broadcasted_iota(jnp.int32, (_QROWS, 1), 0), GQA_REP - 1)
    return jnp.exp2(-8.0 * (GQA_REP * g + 1 + r).astype(f32) / N_ATT_HEADS)


def _nsa_sample_cmp_kernel(q_ref, cmp_ref, pool_ref, oc_ref, ids_ref, *, past):
    b = pl.program_id(0)
    n_cmp = cmp_ref.shape[1]
    n_old = pool_ref.shape[1]
    q = q_ref[pl.ds(b, 1), :]
    cmpb = cmp_ref[0].astype(bf16)
    d_c = past - (lax.broadcasted_iota(jnp.int32, (1, n_cmp), 1) * CMP_STRIDE + (CMP_BLOCK - 1))
    lane = lax.broadcasted_iota(jnp.int32, (1, _IDS_LANES), 1)
    cur = jnp.full((1, 1), past // SLC_BLOCK, jnp.int32)
    oc_parts = []
    for g in range(N_KV_HEADS):
        q16 = _group_queries_1(q, g).astype(bf16)
        s = _dot_nt(q16, cmpb[:, g * HEAD_DIM:(g + 1) * HEAD_DIM]) - _group_slopes_1(g) * d_c.astype(f32)
        p = _masked_softmax(s, jnp.broadcast_to(d_c >= 0, s.shape))
        o_c = jnp.dot(p.astype(bf16), cmpb[:, (N_KV_HEADS + g) * HEAD_DIM:(N_KV_HEADS + g + 1) * HEAD_DIM],
                      preferred_element_type=f32)
        oc_parts.extend(o_c[r:r + 1] for r in range(GQA_REP))
        p_sum = jnp.sum(p[0:GQA_REP], axis=0, keepdims=True)
        p_slc = jnp.dot(jnp.broadcast_to(p_sum, (8, n_cmp)), pool_ref[...], preferred_element_type=f32,
                        precision=HIGHEST)[0:1]
        p_ext = jnp.concatenate([p_slc, jnp.zeros((1, 128), f32)], axis=1)
        _, picks = _select_blocks(p_ext, cur, past // SLC_BLOCK + 1)
        row = jnp.zeros((1, _IDS_LANES), jnp.int32)
        for k, pick in enumerate(picks):
            row = jnp.where(lane == k, pick, row)
        ids_ref[0, g:g + 1, :] = row
    oc_ref[pl.ds(b, 1), :] = jnp.concatenate(oc_parts, axis=1)


def _nsa_sample_cmp(q, cmp_s, pool, past):
    nb = q.shape[0]
    return pl.pallas_call(
        functools.partial(_nsa_sample_cmp_kernel, past=past),
        grid=(nb,),
        in_specs=[pl.BlockSpec(q.shape, lambda b: (0, 0)),
                  pl.BlockSpec((1,) + cmp_s.shape[1:], lambda b: (b, 0, 0)),
                  pl.BlockSpec(pool.shape, lambda b: (0, 0))],
        out_specs=[pl.BlockSpec((nb, D_ATT), lambda b: (0, 0)),
                   pl.BlockSpec((1, N_KV_HEADS, _IDS_LANES), lambda b: (b, 0, 0))],
        out_shape=[jax.ShapeDtypeStruct((nb, D_ATT), f32),
                   jax.ShapeDtypeStruct((nb, N_KV_HEADS, _IDS_LANES), jnp.int32)],
        compiler_params=_cparams("arbitrary"),
        name="nsa_sample_cmp",
    )(q, cmp_s, pool)


def _softmax_with_new(s, mask, s_new, new_ok):
    s = jnp.where(mask, s, NEG_INF)
    s_new = jnp.where(new_ok, s_new, NEG_INF)
    m = jnp.maximum(jnp.max(s, axis=1, keepdims=True), s_new)
    p = jnp.where(mask, jnp.exp(s - m), 0.0)
    p_new = jnp.where(new_ok, jnp.exp(s_new - m), 0.0)
    l = jnp.sum(p, axis=1, keepdims=True) + p_new
    inv = jnp.where(l > 0.0, 1.0 / jnp.where(l > 0.0, l, 1.0), 0.0)
    return p * inv, p_new * inv


def _nsa_sample_sel_kernel(ids_ref, pt_ref, pool_ref, q_ref, sm_ref, kvs_ref, kvw_ref, win_ref, oc_ref, o_ref,
                           buf, sem, *, past):
    b = pl.program_id(0)
    n_pages = past // PAGE_ROWS
    last_blk = past // SLC_BLOCK
    blocks_per_page = PAGE_ROWS // SLC_BLOCK

    def block_copy(g, k):
        blk = jnp.minimum(ids_ref[(b * N_KV_HEADS + g) * N_SELECT + k], last_blk - 1)
        page = pt_ref[b * n_pages + blk // blocks_per_page]
        off = pl.multiple_of((blk % blocks_per_page) * SLC_BLOCK, SLC_BLOCK)
        return pltpu.make_async_copy(pool_ref.at[page, pl.ds(off, SLC_BLOCK), :],
                                     buf.at[g, pl.ds(k * SLC_BLOCK, SLC_BLOCK), :], sem)

    for g in range(N_KV_HEADS):
        for k in range(N_SELECT):
            block_copy(g, k).start()
    for g in range(N_KV_HEADS):
        for k in range(N_SELECT):
            block_copy(g, k).wait()

    q = q_ref[pl.ds(b, 1), :]
    gates = sm_ref[pl.ds(b, 1), :]
    new_s = kvs_ref[pl.ds(b, 1), :]
    new_w = kvw_ref[pl.ds(b, 1), :]
    n_sel = N_SELECT * SLC_BLOCK
    lane = lax.broadcasted_iota(jnp.int32, (1, n_sel), 1)
    w_buf = win_ref.shape[1]
    d_w = w_buf - lax.broadcasted_iota(jnp.int32, (1, w_buf), 1)
    mask_w = jnp.broadcast_to((d_w < WINDOW) & (past - d_w >= 0), (_QROWS, w_buf))
    always = jnp.full((_QROWS, 1), True)
    o_c = oc_ref[pl.ds(b, 1), :]
    out_parts = []
    for g in range(N_KV_HEADS):
        kcol = slice(g * HEAD_DIM, (g + 1) * HEAD_DIM)
        vcol = slice((N_KV_HEADS + g) * HEAD_DIM, (N_KV_HEADS + g + 1) * HEAD_DIM)
        q16f = _group_queries_1(q, g)
        q16 = q16f.astype(bf16)
        slope = _group_slopes_1(g)

        blk_vec = jnp.zeros((1, n_sel), jnp.int32)
        for k in range(N_SELECT):
            blk_vec = jnp.where(lane // SLC_BLOCK == k, ids_ref[(b * N_KV_HEADS + g) * N_SELECT + k], blk_vec)
        d_s = past - (blk_vec * SLC_BLOCK + lane % SLC_BLOCK)
        mask_s = jnp.broadcast_to((d_s >= 0) & (blk_vec < last_blk), (_QROWS, n_sel))
        has_new = jnp.max(jnp.where(blk_vec == last_blk, 1, 0), axis=1, keepdims=True) > 0
        kb = buf[g].astype(bf16)
        s = _dot_nt(q16, kb[:, kcol]) - slope * d_s.astype(f32)
        s_new = jnp.sum(q16f * new_s[:, kcol], axis=1, keepdims=True)
        p, p_new = _softmax_with_new(s, mask_s, s_new, jnp.broadcast_to(has_new, (_QROWS, 1)))
        o_s = jnp.dot(p.astype(bf16), kb[:, vcol], preferred_element_type=f32) + p_new * new_s[:, vcol]

        wb = win_ref[0].astype(bf16)
        s = _dot_nt(q16, wb[:, kcol]) - slope * d_w.astype(f32)
        s_new = jnp.sum(q16f * new_w[:, kcol], axis=1, keepdims=True)
        p, p_new = _softmax_with_new(s, mask_w, s_new, always)
        o_w = jnp.dot(p.astype(bf16), wb[:, vcol], preferred_element_type=f32) + p_new * new_w[:, vcol]

        for r in range(GQA_REP):
            head = GQA_REP * g + r
            hs = slice(head * HEAD_DIM, (head + 1) * HEAD_DIM)
            out_parts.append(gates[:, 3 * head:3 * head + 1] * o_c[:, hs]
                             + gates[:, 3 * head + 1:3 * head + 2] * o_s[r:r + 1]
                             + gates[:, 3 * head + 2:3 * head + 3] * o_w[r:r + 1])
    o_ref[pl.ds(b, 1), :] = jnp.concatenate(out_parts, axis=1)


def _nsa_sample_sel(ids, page_table, pool_slc, q, small, kvs_new, kvw_new, win_buf, o_c, past):
    nb = q.shape[0]
    full = lambda a: pl.BlockSpec(a.shape, lambda b, ids, pt: (0,) * a.ndim)
    return pl.pallas_call(
        functools.partial(_nsa_sample_sel_kernel, past=past),
        grid_spec=pltpu.PrefetchScalarGridSpec(
            num_scalar_prefetch=2,
            grid=(nb,),
            in_specs=[pl.BlockSpec(memory_space=pl.ANY), full(q), full(small), full(kvs_new), full(kvw_new),
                      pl.BlockSpec((1,) + win_buf.shape[1:], lambda b, ids, pt: (b, 0, 0)), full(o_c)],
            out_specs=pl.BlockSpec((nb, D_ATT), lambda b, ids, pt: (0, 0)),
            scratch_shapes=[pltpu.VMEM((N_KV_HEADS, N_SELECT * SLC_BLOCK, KV_COLS), f32),
                            pltpu.SemaphoreType.DMA(())]),
        out_shape=jax.ShapeDtypeStruct((nb, D_ATT), f32),
        compiler_params=_cparams("arbitrary"),
        name="nsa_sample_sel",
    )(ids, page_table.reshape(-1), pool_slc, q, small, kvs_new, kvw_new, win_buf, o_c)


def _mixout_kernel(att_ref, mh_ref, mo_ref, x_ref, ga_ref, gm_ref, wo_ref, nf_ref, wq_ref, h_ref, xn_ref,
                   qp_ref):
    parts = []
    for h in range(N_ATT_HEADS):
        sl = slice(h * HEAD_DIM, (h + 1) * HEAD_DIM)
        parts.append(_rms(att_ref[:, sl], ga_ref[:, sl]))
    for h in range(N_MLSTM_HEADS):
        sl = slice(h * MLSTM_HEAD_DIM, (h + 1) * MLSTM_HEAD_DIM)
        parts.append(mo_ref[:, sl] * _rms(mh_ref[:, sl], gm_ref[:, sl]))
    cat = jnp.concatenate(parts, axis=1).astype(bf16)
    h1 = x_ref[...] + jnp.dot(cat, wo_ref[...], preferred_element_type=f32)
    h_ref[...] = h1
    xn = _rms(h1, nf_ref[...]).astype(bf16)
    xn_ref[...] = xn
    qp_ref[...] = jnp.dot(xn, wq_ref[...], preferred_element_type=f32)


def _mix_output(att, mh, mo, x, g_att, g_ml, w_out_bf, norm_ffn, w_q_bf, tm):
    n = x.shape[0]
    tok = lambda w: pl.BlockSpec((tm, w), lambda i: (i, 0))
    full = lambda a: pl.BlockSpec(a.shape, lambda i: (0,) * a.ndim)
    return pl.pallas_call(
        _mixout_kernel,
        grid=(n // tm,),
        in_specs=[tok(D_ATT), tok(D_MLSTM), tok(D_MLSTM), tok(D_MODEL), full(g_att), full(g_ml),
                  full(w_out_bf), full(norm_ffn), full(w_q_bf)],
        out_specs=[tok(D_MODEL), tok(D_MODEL), tok(D_MODEL)],
        out_shape=[jax.ShapeDtypeStruct((n, D_MODEL), f32), jax.ShapeDtypeStruct((n, D_MODEL), bf16),
                   jax.ShapeDtypeStruct((n, D_MODEL), f32)],
        compiler_params=_cparams("parallel"),
        name="mix_output",
    )(att, mh, mo, x, g_att, g_ml, w_out_bf, norm_ffn, w_q_bf)


def _topk_rows(s, k):
    n = s.shape[0]
    rows = lax.broadcasted_iota(jnp.int32, s.shape, 0)
    vals, idxs = [], []
    for _ in range(k):
        mx = jnp.max(s, axis=0, keepdims=True)
        idx = jnp.min(jnp.where(s == mx, rows, n), axis=0, keepdims=True)
        vals.append(mx)
        idxs.append(idx)
        s = jnp.where(rows == idx, NEG_INF, s)
    return jnp.concatenate(vals, axis=0), jnp.concatenate(idxs, axis=0)


def _peer_topk_kernel(qp_ref, sub_ref, ei_ref, ej_ref, g_ref):
    K = PEER_TOPK
    half = PEER_D_KEY // 2
    ei, ej, gg = [], [], []
    for h in range(PEER_HEADS):
        sv, si = [], []
        for c in range(2):
            qhc = qp_ref[:, (2 * h + c) * half:(2 * h + c + 1) * half]
            s = _dot_nt(sub_ref[h, c], qhc, precision=HIGHEST)
            v, i = _topk_rows(s, K)
            sv.append(v)
            si.append(i)
        cand = jnp.concatenate([sv[0][a:a + 1] + sv[1] for a in range(K)], axis=0)
        pos_i = jnp.concatenate([jnp.broadcast_to(si[0][a:a + 1], si[1].shape) for a in range(K)], axis=0)
        pos_j = jnp.concatenate([si[1]] * K, axis=0)
        rows = lax.broadcasted_iota(jnp.int32, cand.shape, 0)
        best, bi, bj = [], [], []
        for _ in range(K):
            mx = jnp.max(cand, axis=0, keepdims=True)
            pos = jnp.min(jnp.where(cand == mx, rows, K * K), axis=0, keepdims=True)
            hit = rows == pos
            best.append(mx)
            bi.append(jnp.max(jnp.where(hit, pos_i, -1), axis=0, keepdims=True))
            bj.append(jnp.max(jnp.where(hit, pos_j, -1), axis=0, keepdims=True))
            cand = jnp.where(hit, NEG_INF, cand)
        best = jnp.concatenate(best, axis=0)
        e = jnp.exp(best - best[0:1])
        gg.append(e / jnp.sum(e, axis=0, keepdims=True))
        ei.append(jnp.concatenate(bi, axis=0))
        ej.append(jnp.concatenate(bj, axis=0))
    ei_ref[...] = jnp.concatenate(ei, axis=0).astype(f32).T
    ej_ref[...] = jnp.concatenate(ej, axis=0).astype(f32).T
    g_ref[...] = jnp.concatenate(gg, axis=0).T


def _peer_topk(qp, sub_keys, tm):
    n = qp.shape[0]
    hk = PEER_HEADS * PEER_TOPK
    return pl.pallas_call(
        _peer_topk_kernel,
        grid=(n // tm,),
        in_specs=[pl.BlockSpec((tm, D_MODEL), lambda i: (i, 0)),
                  pl.BlockSpec(sub_keys.shape, lambda i: (0, 0, 0, 0))],
        out_specs=[pl.BlockSpec((tm, hk), lambda i: (i, 0))] * 3,
        out_shape=[jax.ShapeDtypeStruct((n, hk), f32)] * 3,
        compiler_params=_cparams("parallel"),
        name="peer_topk",
    )(qp, sub_keys)


def _peer_gate_kernel(ei_ref, ej_ref, g_ref, o_ref):
    tb, hk = ei_ref.shape
    sub = lax.broadcasted_iota(jnp.int32, (PEER_N_KEYS, hk), 0).astype(f32)

    def body(t, carry):
        a = jnp.where(sub == ei_ref[pl.ds(t, 1), :], 1.0, 0.0).astype(bf16)
        b = jnp.where(sub == ej_ref[pl.ds(t, 1), :], g_ref[pl.ds(t, 1), :], 0.0).astype(bf16)
        o_ref[t] = _dot_nt(a, b).astype(bf16)
        return carry

    lax.fori_loop(0, tb, body, 0)


def _peer_gates(ei, ej, gg, tb):
    n, hk = ei.shape
    return pl.pallas_call(
        _peer_gate_kernel,
        grid=(n // tb,),
        in_specs=[pl.BlockSpec((tb, hk), lambda i: (i, 0))] * 3,
        out_specs=pl.BlockSpec((tb, PEER_N_KEYS, PEER_N_KEYS), lambda i: (i, 0, 0)),
        out_shape=jax.ShapeDtypeStruct((n, PEER_N_KEYS, PEER_N_KEYS), bf16),
        compiler_params=_cparams("parallel"),
        name="peer_gates",
    )(ei, ej, gg)


def _peer_dense_kernel(x_ref, g_ref, u_ref, v_ref, h_ref, nf_ref, o_ref, acc_ref):
    e = pl.program_id(1)

    @pl.when(e == 0)
    def _():
        acc_ref[...] = jnp.zeros_like(acc_ref)

    act = jax.nn.gelu(_dot_nt(x_ref[...], u_ref[...]))
    w = (g_ref[...].astype(f32) * act).astype(bf16)
    acc_ref[...] += jnp.dot(w, v_ref[...], preferred_element_type=f32)

    @pl.when(e == pl.num_programs(1) - 1)
    def _():
        o_ref[...] = _rms(h_ref[...] + acc_ref[...], nf_ref[...])


def _peer_dense(xn_bf, gmat, u_bf, v_bf, h1, norm_final, tm, et):
    n = xn_bf.shape[0]
    n_exp = u_bf.shape[0]
    return pl.pallas_call(
        _peer_dense_kernel,
        grid=(n // tm, n_exp // et),
        in_specs=[pl.BlockSpec((tm, D_MODEL), lambda i, e: (i, 0)),
                  pl.BlockSpec((tm, et), lambda i, e: (i, e)),
                  pl.BlockSpec((et, D_MODEL), lambda i, e: (e, 0)),
                  pl.BlockSpec((et, D_MODEL), lambda i, e: (e, 0)),
                  pl.BlockSpec((tm, D_MODEL), lambda i, e: (i, 0)),
                  pl.BlockSpec((1, D_MODEL), lambda i, e: (0, 0))],
        out_specs=pl.BlockSpec((tm, D_MODEL), lambda i, e: (i, 0)),
        out_shape=jax.ShapeDtypeStruct((n, D_MODEL), f32),
        scratch_shapes=[pltpu.VMEM((tm, D_MODEL), f32)],
        compiler_params=_cparams("parallel", "arbitrary"),
        name="peer_dense",
    )(xn_bf, gmat, u_bf, v_bf, h1, norm_final)


def _channel_mix_and_norm(h1, xn_bf, qp, sub_keys, u_bf, v_bf, norm_final, tm_topk, tb_gate, tm, et):
    n = h1.shape[0]
    ei, ej, gg = _peer_topk(qp, sub_keys, tm_topk)
    gmat = _peer_gates(ei, ej, gg, tb_gate).reshape(n, PEER_N_KEYS * PEER_N_KEYS)
    return _peer_dense(xn_bf, gmat, u_bf, v_bf, h1, norm_final, tm, et)


PROMPT_TM = 512
MLSTM_CHUNK = 256
SAMPLE_PAD = 128


def kernel(x_prompt, x_sample, cache_cmp_kv, cache_slc_kv, cache_win_kv, state_mlstm_C, state_mlstm_n,
           state_mlstm_m, page_table, norm_mix, w_in, b_igate, b_fgate, pe_cmp, w_cmp1, w_cmp2, norm_att_out,
           norm_mlstm_out, w_out, norm_ffn, peer_wq, peer_subkeys, peer_u, peer_v, norm_final):
    assert x_prompt.shape[0] == 1 and x_sample.shape[1] == 1 and w_in.shape[0] == 1
    _, t, d = x_prompt.shape
    nb = x_sample.shape[0]
    n_pool = cache_cmp_kv.shape[1]
    past = page_table.shape[1] * PAGE_ROWS
    w_buf = cache_win_kv.shape[2]
    row = (2, N_KV_HEADS, HEAD_DIM)
    NH, DH = N_MLSTM_HEADS, MLSTM_HEAD_DIM
    g0 = N_GATE_COLS

    w_re = _relayout_w_in(w_in[0])
    bias = _bias_row(b_igate[0], b_fgate[0])
    cw = _compress_weights(pe_cmp[0], w_cmp1[0], w_cmp2[0])
    nm = norm_mix[0][None]
    ga, gm, nf, nfin = norm_att_out[0][None], norm_mlstm_out[0][None], norm_ffn[0][None], norm_final[None]
    w_out_bf, w_q_bf = w_out[0].astype(bf16), peer_wq[0].astype(bf16)
    u_bf, v_bf = peer_u[0].astype(bf16), peer_v[0].astype(bf16)
    sub_keys = peer_subkeys[0]

    xp = x_prompt.reshape(t, d)
    q, kvc, kvs, kvw, mq, mk, mv, mo, sm = _project(xp, nm, w_re, bias, PROMPT_TM)
    cmp_p = _compress_prompt(kvc.reshape(1, t // CMP_STRIDE, _CHUNK_COLS), cw)[0]
    pool_p = _pool_matrix(t // CMP_STRIDE, t // SLC_BLOCK)
    kvw_pad = jnp.pad(kvw, ((WINDOW, 0), (0, 0))).astype(bf16)
    att = _nsa_prompt(q, sm, cmp_p.astype(bf16), pool_p, kvs.astype(bf16), kvw_pad)
    gates_c = sm[:, g0:g0 + 2 * NH]
    mh, c_p, n_p, m_p = _mlstm_prompt(mq, mk, mv, gates_c, gates_c.T, MLSTM_CHUNK)
    h1, xn_bf, qp = _mix_output(att, mh, mo, xp, ga, gm, w_out_bf, nf, w_q_bf, 256)
    y_p = _channel_mix_and_norm(h1, xn_bf, qp, sub_keys, u_bf, v_bf, nfin, 256, 128, PROMPT_TM, 256)
    w_keep = min(WINDOW, t)
    outs_p = (kvc.reshape((1, 1, t) + row), kvs.reshape((1, 1, t) + row),
              kvw[t - w_keep:].reshape((1, 1, w_keep) + row),
              c_p[None, None], n_p[:NH][None, None], m_p[:NH, 0][None, None])

    xs = x_sample.reshape(nb, d)
    q, kvc_s, kvs_s, kvw_s, mq, mk, mv, mo, sm = _project(xs, nm, w_re, bias, nb)
    cmp_s = _compress_paged(page_table, cache_cmp_kv[0].reshape(n_pool, _PAGE_CHUNKS, _CHUNK_COLS), cw)
    pool_s = _pool_matrix(past // CMP_STRIDE, past // SLC_BLOCK)
    o_c, ids = _nsa_sample_cmp(q, cmp_s, pool_s, past)
    win_buf = cache_win_kv[0].reshape(nb, w_buf, KV_COLS)
    att = _nsa_sample_sel(ids[:, :, :N_SELECT].reshape(-1), page_table,
                          cache_slc_kv[0].reshape(n_pool, PAGE_ROWS, KV_COLS), q, sm, kvs_s, kvw_s, win_buf, o_c,
                          past)
    mh, c_s, n_s, m_s = _mlstm_sample(mq, mk, mv, sm[:, g0:g0 + 2 * NH], state_mlstm_C[0], state_mlstm_n[0],
                                      state_mlstm_m[0])
    h1, xn_bf, qp = _mix_output(att, mh, mo, xs, ga, gm, w_out_bf, nf, w_q_bf, nb)
    padr = lambda a: jnp.pad(a, ((0, SAMPLE_PAD - nb), (0, 0)))
    y_s = _channel_mix_and_norm(padr(h1), padr(xn_bf), padr(qp), sub_keys, u_bf, v_bf, nfin, SAMPLE_PAD,
                                SAMPLE_PAD, SAMPLE_PAD, 256)[:nb]
    win_all = jnp.concatenate([win_buf, kvw_s[:, None, :]], axis=1)
    w_keep_s = min(WINDOW, w_buf + 1)
    outs_s = (kvc_s.reshape((1, nb, 1) + row), kvs_s.reshape((1, nb, 1) + row),
              win_all[:, w_buf + 1 - w_keep_s:].reshape((1, nb, w_keep_s) + row),
              c_s[None], n_s[None], m_s[None])

    return (y_p.reshape(1, t, d), y_s.reshape(nb, 1, d),
            outs_p[0], outs_s[0], outs_p[1], outs_s[1], outs_p[2], outs_s[2],
            outs_p[3], outs_s[3], outs_p[4], outs_s[4], outs_p[5], outs_s[5])
```

```python
import functools

import jax
import jax.numpy as jnp
import numpy as np
from jax import lax
from jax.experimental import pallas as pl
from jax.experimental.pallas import tpu as pltpu

f32 = jnp.float32
bf16 = jnp.bfloat16
HIGHEST = lax.Precision.HIGHEST

D_MODEL = 1024
HEAD_DIM = 64
N_ATT_HEADS = 8
N_KV_HEADS = 2
GQA_REP = 4
D_ATT = 512
KV_COLS = 256
CMP_BLOCK = 32
CMP_STRIDE = 16
CMP_HIDDEN = 128
SLC_BLOCK = 64
N_SELECT = 16
WINDOW = 512
Q_BLOCK = 128
FORCE_BONUS = 1000.0
N_MLSTM_HEADS = 4
MLSTM_HEAD_DIM = 128
D_MLSTM = 512
N_GATE_COLS = 3 * N_ATT_HEADS
SPLIT_SIZES = (D_ATT, KV_COLS, KV_COLS, KV_COLS, N_GATE_COLS, D_MLSTM, D_MLSTM, D_MLSTM, D_MLSTM,
               N_MLSTM_HEADS, N_MLSTM_HEADS)
SMALL_COLS = 128
PEER_N_KEYS = 128
PEER_HEADS = 8
PEER_TOPK = 16
PEER_D_KEY = 128
NORM_EPS = 1e-6
NEG_INF = -1e30
VMEM_LIMIT = 56 * 1024 * 1024


def _cparams(*sem):
    return pltpu.CompilerParams(dimension_semantics=sem, vmem_limit_bytes=VMEM_LIMIT)


def _rms(x, w):
    return x * lax.rsqrt(jnp.mean(x * x, axis=-1, keepdims=True) + NORM_EPS) * w


_P_Q, _P_KVC, _P_KVS, _P_KVW, _P_MQ, _P_MK, _P_MV, _P_MO, _P_SM, _P_END = (
    0, 512, 768, 1024, 1280, 1792, 2304, 2816, 3328, 3456)


def _relayout_w_in(w_in):
    pts = np.cumsum(SPLIT_SIZES)[:-1].tolist()
    q, kc, ks, kw, ga, mq, mk, mv, mo, ig, fg = jnp.split(w_in, pts, axis=-1)
    pad = jnp.zeros((w_in.shape[0], SMALL_COLS - N_GATE_COLS - 2 * N_MLSTM_HEADS), w_in.dtype)
    return jnp.concatenate([q, kc, ks, kw, mq, mk, mv, mo, ga, ig, fg, pad], axis=-1).astype(bf16)


def _bias_row(b_i, b_f):
    pad = jnp.zeros((SMALL_COLS - N_GATE_COLS - 2 * N_MLSTM_HEADS,), f32)
    return jnp.concatenate([jnp.zeros((N_GATE_COLS,), f32), b_i, b_f, pad])[None, :]


def _proj_kernel(x_ref, nw_ref, w_ref, b_ref, q_ref, kvc_ref, kvs_ref, kvw_ref, mq_ref, mk_ref, mv_ref,
                 mo_ref, sm_ref):
    xb = _rms(x_ref[...], nw_ref[...]).astype(bf16)

    def mm(lo, hi):
        return jnp.dot(xb, w_ref[:, lo:hi], preferred_element_type=f32)

    q_ref[...] = mm(_P_Q, _P_KVC)
    kvc_ref[...] = mm(_P_KVC, _P_KVS)
    kvs_ref[...] = mm(_P_KVS, _P_KVW)
    kvw_ref[...] = mm(_P_KVW, _P_MQ)
    mq_ref[...] = mm(_P_MQ, _P_MK)
    mk_ref[...] = mm(_P_MK, _P_MV)
    mv_ref[...] = mm(_P_MV, _P_MO)
    mo_ref[...] = jax.nn.sigmoid(mm(_P_MO, _P_SM))
    s = mm(_P_SM, _P_END) + b_ref[...]
    col = lax.broadcasted_iota(jnp.int32, s.shape, 1)
    sm_ref[...] = jnp.where(col < N_GATE_COLS, jax.nn.sigmoid(s), s)


def _project(x, norm_w, w_re, bias_row, tm):
    n = x.shape[0]
    widths = (D_ATT, KV_COLS, KV_COLS, KV_COLS, D_MLSTM, D_MLSTM, D_MLSTM, D_MLSTM, SMALL_COLS)
    return pl.pallas_call(
        _proj_kernel,
        grid=(n // tm,),
        in_specs=[pl.BlockSpec((tm, D_MODEL), lambda i: (i, 0)),
                  pl.BlockSpec((1, D_MODEL), lambda i: (0, 0)),
                  pl.BlockSpec((D_MODEL, _P_END), lambda i: (0, 0)),
                  pl.BlockSpec((1, SMALL_COLS), lambda i: (0, 0))],
        out_specs=[pl.BlockSpec((tm, w), lambda i: (i, 0)) for w in widths],
        out_shape=[jax.ShapeDtypeStruct((n, w), f32) for w in widths],
        compiler_params=_cparams("parallel"),
        name="proj",
    )(x, norm_w, w_re, bias_row)


_CHUNK_COLS = CMP_STRIDE * KV_COLS
_HID_COLS = 2 * N_KV_HEADS * CMP_HIDDEN


def _compress_weights(pe, w1, w2):
    eye = jnp.eye(2, dtype=f32)

    def big1(w):
        return jnp.einsum('cldh,ce,gf->lefdcgh', w, eye, eye).reshape(_CHUNK_COLS, _HID_COLS).astype(bf16)

    def pe_row(p):
        return jnp.broadcast_to(p[:, :, None, :], (CMP_STRIDE, 2, N_KV_HEADS, HEAD_DIM)).reshape(1, _CHUNK_COLS)

    w2_big = jnp.einsum('chd,ce,gf->efhcgd', w2, eye, eye).reshape(_HID_COLS, KV_COLS).astype(bf16)
    pe_rows = jnp.concatenate([pe_row(pe[:CMP_STRIDE]), jnp.zeros((7, _CHUNK_COLS), f32),
                               pe_row(pe[CMP_STRIDE:]), jnp.zeros((7, _CHUNK_COLS), f32)], axis=0).astype(bf16)
    return big1(w1[:, :CMP_STRIDE]), big1(w1[:, CMP_STRIDE:]), pe_rows, w2_big


_CMP_ROWS = 512


def _compress_math(x_ref, w1a_ref, w1b_ref, pe_ref, w2_ref):
    n_chunks = x_ref.shape[0]
    step = min(_CMP_ROWS, n_chunks)
    first, second = [], []
    for r0 in range(0, n_chunks, step):
        x = x_ref[r0:r0 + step, :].astype(bf16)
        first.append(jnp.dot(x, w1a_ref[...], preferred_element_type=f32))
        second.append(jnp.dot(x, w1b_ref[...], preferred_element_type=f32))
    first = jnp.concatenate(first, axis=0)
    second = jnp.concatenate(second, axis=0)
    pe_term = (jnp.dot(pe_ref[0:8, :], w1a_ref[...], preferred_element_type=f32)
               + jnp.dot(pe_ref[8:16, :], w1b_ref[...], preferred_element_type=f32))[0:1]
    h = jax.nn.gelu(first + pltpu.roll(second, n_chunks - 1, 0) + pe_term)
    return jnp.dot(h.astype(bf16), w2_ref[...], preferred_element_type=f32)


def _compress_kernel(x_ref, w1a_ref, w1b_ref, pe_ref, w2_ref, o_ref):
    o_ref[0] = _compress_math(x_ref.at[0], w1a_ref, w1b_ref, pe_ref, w2_ref)


def _compress_paged_kernel(pt_ref, pool_ref, w1a_ref, w1b_ref, pe_ref, w2_ref, o_ref, x_s, sem):
    b = pl.program_id(0)
    n_pages = x_s.shape[0] // _PAGE_CHUNKS

    def page_copy(p):
        return pltpu.make_async_copy(pool_ref.at[pt_ref[b * n_pages + p]],
                                     x_s.at[pl.ds(pl.multiple_of(p * _PAGE_CHUNKS, _PAGE_CHUNKS), _PAGE_CHUNKS)],
                                     sem)

    def start(p, c):
        page_copy(p).start()
        return c

    def wait(p, c):
        page_copy(p).wait()
        return c

    lax.fori_loop(0, n_pages, start, 0)
    lax.fori_loop(0, n_pages, wait, 0)
    o_ref[0] = _compress_math(x_s, w1a_ref, w1b_ref, pe_ref, w2_ref)


_PAGE_CHUNKS = 8


def _compress_paged(page_table, pool_chunks, cw):
    nb, n_pages = page_table.shape
    n_chunks = n_pages * _PAGE_CHUNKS
    full = lambda a: pl.BlockSpec(a.shape, lambda b, pt: (0,) * a.ndim)
    return pl.pallas_call(
        _compress_paged_kernel,
        grid_spec=pltpu.PrefetchScalarGridSpec(
            num_scalar_prefetch=1,
            grid=(nb,),
            in_specs=[pl.BlockSpec(memory_space=pl.ANY)] + [full(a) for a in cw],
            out_specs=pl.BlockSpec((1, n_chunks, KV_COLS), lambda b, pt: (b, 0, 0)),
            scratch_shapes=[pltpu.VMEM((n_chunks, _CHUNK_COLS), f32), pltpu.SemaphoreType.DMA(())]),
        out_shape=jax.ShapeDtypeStruct((nb, n_chunks, KV_COLS), f32),
        compiler_params=_cparams("arbitrary"),
        name="compress_paged",
    )(page_table.reshape(-1), pool_chunks, *cw)


def _compress_prompt(kv_chunks, cw):
    nb, n_chunks, _ = kv_chunks.shape
    full = lambda a: pl.BlockSpec(a.shape, lambda b: (0,) * a.ndim)
    return pl.pallas_call(
        _compress_kernel,
        grid=(nb,),
        in_specs=[pl.BlockSpec((1, n_chunks, _CHUNK_COLS), lambda b: (b, 0, 0))] + [full(a) for a in cw],
        out_specs=pl.BlockSpec((1, n_chunks, KV_COLS), lambda b: (b, 0, 0)),
        out_shape=jax.ShapeDtypeStruct((nb, n_chunks, KV_COLS), f32),
        compiler_params=_cparams("parallel"),
        name="compress_prompt",
    )(kv_chunks, *cw)


def _dot_t0(a, b, **kw):
    return lax.dot_general(a, b, (((0,), (0,)), ((), ())), preferred_element_type=f32, **kw)


def _dot_nt(a, b, **kw):
    return lax.dot_general(a, b, (((1,), (1,)), ((), ())), preferred_element_type=f32, **kw)


def _mlstm_chunk_kernel(q_ref, k_ref, v_ref, gc_ref, gr_ref, h_ref, c_out, n_out, m_out, c_s, n_s, m_s):
    L = q_ref.shape[0]
    NH, DH = N_MLSTM_HEADS, MLSTM_HEAD_DIM

    @pl.when(pl.program_id(0) == 0)
    def _():
        c_s[...] = jnp.zeros_like(c_s)
        n_s[...] = jnp.zeros_like(n_s)
        m_s[...] = jnp.zeros_like(m_s)

    row = lax.broadcasted_iota(jnp.int32, (L, L), 0)
    col = lax.broadcasted_iota(jnp.int32, (L, L), 1)
    causal = col <= row
    gc = gc_ref[...]
    gr = gr_ref[...]
    lf_c = jax.nn.log_sigmoid(gc[:, NH:2 * NH])
    lf_r = jax.nn.log_sigmoid(gr[NH:2 * NH, :])
    f_c = jnp.dot(causal.astype(f32), lf_c, preferred_element_type=f32, precision=HIGHEST)
    f_r = jnp.dot(lf_r, (row <= col).astype(f32), preferred_element_type=f32, precision=HIGHEST)
    for h in range(NH):
        sl = slice(h * DH, (h + 1) * DH)
        fc, fr = f_c[:, h:h + 1], f_r[h:h + 1, :]
        ic, ir = gc[:, h:h + 1], gr[h:h + 1, :]
        m_prev = m_s[h:h + 1, 0:1]
        qh = q_ref[:, sl].astype(bf16)
        kh = k_ref[:, sl] * (DH ** -0.5)
        vh = v_ref[:, sl].astype(bf16)
        log_d = fc - fr + ir
        m_t = jnp.maximum(fc + m_prev, jnp.max(jnp.where(causal, log_d, NEG_INF), axis=1, keepdims=True))
        w = jnp.where(causal, jnp.exp(log_d - m_t), 0.0) * _dot_nt(qh, kh.astype(bf16))
        inter = jnp.exp(fc + m_prev - m_t)
        c_old = c_s[h]
        n_old = n_s[h:h + 1, :]
        num = (jnp.dot(w.astype(bf16), vh, preferred_element_type=f32)
               + inter * jnp.dot(qh, c_old.astype(bf16), preferred_element_type=f32))
        den = (jnp.sum(w, axis=1, keepdims=True)
               + inter * jnp.sum(q_ref[:, sl] * n_old, axis=1, keepdims=True))
        h_ref[:, sl] = num / jnp.maximum(jnp.abs(den), jnp.exp(-m_t))
        f_tot = fc[L - 1:L, :]
        m_new = m_t[L - 1:L, :]
        kw = kh * jnp.exp(f_tot - fc + ic - m_new)
        decay = jnp.exp(f_tot + m_prev - m_new)
        c_new = decay * c_old + _dot_t0(kw.astype(bf16), vh)
        n_new = decay * n_old + jnp.sum(kw, axis=0, keepdims=True)
        c_s[h] = c_new
        n_s[h:h + 1, :] = n_new
        m_s[h:h + 1, :] = jnp.broadcast_to(m_new, (1, DH))
        c_out[h] = c_new
    n_out[...] = n_s[...]
    m_out[...] = m_s[...]


def _mlstm_prompt(mq, mk, mv, gates_c, gates_r, chunk):
    t = mq.shape[0]
    NH, DH = N_MLSTM_HEADS, MLSTM_HEAD_DIM
    tok = lambda w: pl.BlockSpec((chunk, w), lambda c: (c, 0))
    return pl.pallas_call(
        _mlstm_chunk_kernel,
        grid=(t // chunk,),
        in_specs=[tok(D_MLSTM), tok(D_MLSTM), tok(D_MLSTM), tok(2 * NH),
                  pl.BlockSpec((2 * NH, chunk), lambda c: (0, c))],
        out_specs=[tok(D_MLSTM),
                   pl.BlockSpec((NH, DH, DH), lambda c: (0, 0, 0)),
                   pl.BlockSpec((8, DH), lambda c: (0, 0)),
                   pl.BlockSpec((8, DH), lambda c: (0, 0))],
        out_shape=[jax.ShapeDtypeStruct((t, D_MLSTM), f32),
                   jax.ShapeDtypeStruct((NH, DH, DH), f32),
                   jax.ShapeDtypeStruct((8, DH), f32),
                   jax.ShapeDtypeStruct((8, DH), f32)],
        scratch_shapes=[pltpu.VMEM((NH, DH, DH), f32), pltpu.VMEM((8, DH), f32), pltpu.VMEM((8, DH), f32)],
        compiler_params=_cparams("arbitrary"),
        name="mlstm_prompt",
    )(mq, mk, mv, gates_c, gates_r)


def _mlstm_step_kernel(q_ref, k_ref, v_ref, g_ref, c_ref, n_ref, m_ref, h_ref, c_out, n_out, m_out):
    NH, DH = N_MLSTM_HEADS, MLSTM_HEAD_DIM
    b = pl.program_id(0)
    row8 = lax.broadcasted_iota(jnp.int32, (8, DH), 0)
    g = g_ref[pl.ds(b, 1), :]
    m_row = m_ref[pl.ds(b, 1), :]
    m_new_row = jnp.zeros((1, NH), f32)
    lane4 = lax.broadcasted_iota(jnp.int32, (1, NH), 1)
    q_row, k_row, v_row = q_ref[pl.ds(b, 1), :], k_ref[pl.ds(b, 1), :], v_ref[pl.ds(b, 1), :]
    h_parts = []
    for h in range(NH):
        sl = slice(h * DH, (h + 1) * DH)
        q = q_row[:, sl]
        k = k_row[:, sl] * (DH ** -0.5)
        v = v_row[:, sl]
        ig = g[:, h:h + 1]
        lf = jax.nn.log_sigmoid(g[:, NH + h:NH + h + 1])
        m_prev = m_row[:, h:h + 1]
        c_old = c_ref[0, h]
        n_old = n_ref[0, h:h + 1, :]
        m_t = jnp.maximum(lf + m_prev, ig)
        w = jnp.exp(ig - m_t) * jnp.sum(q * k, axis=1, keepdims=True)
        inter = jnp.exp(lf + m_prev - m_t)
        q8 = jnp.where(row8 == 0, q, 0.0)
        qc = jnp.dot(q8, c_old, preferred_element_type=f32, precision=HIGHEST)[0:1]
        num = w * v + inter * qc
        den = w + inter * jnp.sum(q * n_old, axis=1, keepdims=True)
        h_parts.append(num / jnp.maximum(jnp.abs(den), jnp.exp(-m_t)))
        kw = k * jnp.exp(ig - m_t)
        decay = jnp.exp(lf + m_prev - m_t)
        kw8 = jnp.where(row8 == 0, kw, 0.0)
        v8 = jnp.where(row8 == 0, v, 0.0)
        c_out[0, h] = decay * c_old + _dot_t0(kw8, v8, precision=HIGHEST)
        n_out[0, h:h + 1, :] = decay * n_old + kw
        m_new_row = jnp.where(lane4 == h, m_t, m_new_row)
    h_ref[pl.ds(b, 1), :] = jnp.concatenate(h_parts, axis=1)
    m_out[pl.ds(b, 1), :] = m_new_row


def _mlstm_sample(mq, mk, mv, gates, c0, n0, m0):
    nb = mq.shape[0]
    NH, DH = N_MLSTM_HEADS, MLSTM_HEAD_DIM
    full = lambda a: pl.BlockSpec(a.shape, lambda b: (0,) * a.ndim)
    return pl.pallas_call(
        _mlstm_step_kernel,
        grid=(nb,),
        in_specs=[full(mq), full(mk), full(mv), full(gates),
                  pl.BlockSpec((1, NH, DH, DH), lambda b: (b, 0, 0, 0)),
                  pl.BlockSpec((1, NH, DH), lambda b: (b, 0, 0)),
                  full(m0)],
        out_specs=[pl.BlockSpec((nb, D_MLSTM), lambda b: (0, 0)),
                   pl.BlockSpec((1, NH, DH, DH), lambda b: (b, 0, 0, 0)),
                   pl.BlockSpec((1, NH, DH), lambda b: (b, 0, 0)),
                   pl.BlockSpec((nb, NH), lambda b: (0, 0))],
        out_shape=[jax.ShapeDtypeStruct((nb, D_MLSTM), f32),
                   jax.ShapeDtypeStruct((nb, NH, DH, DH), f32),
                   jax.ShapeDtypeStruct((nb, NH, DH), f32),
                   jax.ShapeDtypeStruct((nb, NH), f32)],
        compiler_params=_cparams("arbitrary"),
        name="mlstm_sample",
    )(mq, mk, mv, gates, c0, n0, m0)


SLC_TILE = 512


def _masked_softmax(s, mask):
    sm = jnp.where(mask, s, NEG_INF)
    mx = jnp.max(sm, axis=-1, keepdims=True)
    e = jnp.exp(sm - mx)
    p = e / jnp.sum(e, axis=-1, keepdims=True)
    return jnp.where(mx > 0.5 * NEG_INF, p, 0.0)


def _pool_matrix(n_cmp_rows, n_slc):
    i = np.arange(n_cmp_rows)[:, None]
    j = np.arange(n_slc)[None, :]
    ratio = SLC_BLOCK // CMP_STRIDE
    return jnp.asarray(((i >= ratio * j - 1) & (i <= ratio * j + ratio - 1)).astype(np.float32))


def _select_blocks(p_slc, cur, n_valid_lanes):
    nb = p_slc.shape[1]
    blk = lax.broadcasted_iota(jnp.int32, p_slc.shape, 1)
    valid = blk <= cur
    forced = (blk == 0) | (blk == cur) | (blk == cur - 1)
    score = jnp.where(valid, p_slc + jnp.where(forced, FORCE_BONUS, 0.0), -1.0)
    score = jnp.where(blk < n_valid_lanes, score, -2.0)
    blk_f = blk.astype(f32)
    sel = jnp.zeros(p_slc.shape, f32)
    picks = []
    for _ in range(N_SELECT):
        mx = jnp.max(score, axis=1, keepdims=True)
        idx = jnp.min(jnp.where(score == mx, blk_f, float(nb)), axis=1, keepdims=True)
        hit = blk_f == idx
        sel = jnp.where(hit, 1.0, sel)
        score = jnp.where(hit, -3.0, score)
        picks.append(idx.astype(jnp.int32))
    return sel, picks


def _group_queries(q, g, scale):
    return jnp.concatenate(
        [q[:, (GQA_REP * g + r) * HEAD_DIM:(GQA_REP * g + r + 1) * HEAD_DIM] for r in range(GQA_REP)],
        axis=0) * scale


def _group_slopes(g, tq):
    r = lax.broadcasted_iota(jnp.int32, (GQA_REP * tq, 1), 0) // tq
    head1 = (GQA_REP * g + 1 + r).astype(f32)
    return jnp.exp2(-8.0 * head1 / N_ATT_HEADS)


def _nsa_prompt_kernel(q_ref, gate_ref, cmp_ref, pool_ref, kvs_ref, kvw_ref, o_ref, m_s, l_s, acc_s):
    tq = q_ref.shape[0]
    n_cmp = cmp_ref.shape[0]
    n_slc = pool_ref.shape[1]
    i = pl.program_id(0)
    start = pl.multiple_of(i * tq, tq)
    R4 = GQA_REP * tq
    qpos = start + lax.broadcasted_iota(jnp.int32, (tq, 1), 0)
    qpos4 = jnp.concatenate([qpos] * GQA_REP, axis=0)
    q = q_ref[...]
    gates = gate_ref[...]
    n_tiles = (start + tq + SLC_TILE - 1) // SLC_TILE
    for g in range(N_KV_HEADS):
        kcol = slice(g * HEAD_DIM, (g + 1) * HEAD_DIM)
        vcol = slice((N_KV_HEADS + g) * HEAD_DIM, (N_KV_HEADS + g + 1) * HEAD_DIM)
        q4 = _group_queries(q, g, HEAD_DIM ** -0.5).astype(bf16)
        slope = _group_slopes(g, tq)

        cmp_end = lax.broadcasted_iota(jnp.int32, (1, n_cmp), 1) * CMP_STRIDE + (CMP_BLOCK - 1)
        d_c = qpos4 - cmp_end
        s_c = _dot_nt(q4, cmp_ref[:, kcol]) - slope * d_c.astype(f32)
        p_c = _masked_softmax(s_c, d_c >= 0)
        o_c = jnp.dot(p_c.astype(bf16), cmp_ref[:, vcol], preferred_element_type=f32)
        p_sum = p_c[0:tq] + p_c[tq:2 * tq] + p_c[2 * tq:3 * tq] + p_c[3 * tq:4 * tq]
        p_slc = jnp.dot(p_sum, pool_ref[...], preferred_element_type=f32, precision=HIGHEST)
        sel_f = _select_blocks(p_slc, qpos // SLC_BLOCK, n_slc)[0]
        sel = sel_f.astype(bf16)
        blk_used = jnp.max(sel_f, axis=0, keepdims=True)
        tile_of_blk = lax.broadcasted_iota(jnp.int32, (1, n_slc), 1) // (SLC_TILE // SLC_BLOCK)
        m_s[...] = jnp.full((R4, 1), NEG_INF, f32)
        l_s[...] = jnp.zeros((R4, 1), f32)
        acc_s[...] = jnp.zeros((R4, HEAD_DIM), f32)

        def tile_step(j, carry):
            tile_used = jnp.max(jnp.where(tile_of_blk == j, blk_used, 0.0)) > 0.0

            @pl.when(tile_used)
            def _():
                _slc_tile(j)

            return carry

        def _slc_tile(j):
            m, l, acc = m_s[...], l_s[...], acc_s[...]
            k0 = pl.multiple_of(j * SLC_TILE, SLC_TILE)
            kpos = k0 + lax.broadcasted_iota(jnp.int32, (1, SLC_TILE), 1)
            blk_of_key = k0 // SLC_BLOCK + lax.broadcasted_iota(jnp.int32, (n_slc, SLC_TILE), 1) // SLC_BLOCK
            expand = jnp.where(lax.broadcasted_iota(jnp.int32, (n_slc, SLC_TILE), 0) == blk_of_key, 1.0, 0.0)
            sel_k = jnp.dot(sel, expand.astype(bf16), preferred_element_type=f32)
            sel_k4 = jnp.concatenate([sel_k] * GQA_REP, axis=0)
            d_s = qpos4 - kpos
            mask = (sel_k4 > 0.5) & (d_s >= 0)
            s = _dot_nt(q4, kvs_ref[pl.ds(k0, SLC_TILE), kcol]) - slope * d_s.astype(f32)
            s = jnp.where(mask, s, NEG_INF)
            m_new = jnp.maximum(m, jnp.max(s, axis=1, keepdims=True))
            p = jnp.where(mask, jnp.exp(s - m_new), 0.0)
            alpha = jnp.exp(m - m_new)
            l_s[...] = alpha * l + jnp.sum(p, axis=1, keepdims=True)
            acc_s[...] = alpha * acc + jnp.dot(p.astype(bf16), kvs_ref[pl.ds(k0, SLC_TILE), vcol],
                                               preferred_element_type=f32)
            m_s[...] = m_new

        lax.fori_loop(0, n_tiles, tile_step, 0)
        l_fin = l_s[...]
        o_s = jnp.where(l_fin > 0.0, acc_s[...] / jnp.where(l_fin > 0.0, l_fin, 1.0), 0.0)

        n_win = WINDOW + tq
        win_pos = start - WINDOW + lax.broadcasted_iota(jnp.int32, (1, n_win), 1)
        d_w = qpos4 - win_pos
        m_w = (d_w >= 0) & (d_w < WINDOW) & (win_pos >= 0)
        s_w = _dot_nt(q4, kvw_ref[pl.ds(start, n_win), kcol]) - slope * d_w.astype(f32)
        p_w = _masked_softmax(s_w, m_w)
        o_w = jnp.dot(p_w.astype(bf16), kvw_ref[pl.ds(start, n_win), vcol], preferred_element_type=f32)

        for r in range(GQA_REP):
            head = GQA_REP * g + r
            rows = slice(r * tq, (r + 1) * tq)
            o_ref[:, head * HEAD_DIM:(head + 1) * HEAD_DIM] = (
                gates[:, 3 * head:3 * head + 1] * o_c[rows]
                + gates[:, 3 * head + 1:3 * head + 2] * o_s[rows]
                + gates[:, 3 * head + 2:3 * head + 3] * o_w[rows])


def _nsa_prompt(q, small, cmp_bf, pool, kvs_bf, kvw_pad_bf):
    t = q.shape[0]
    full = lambda a: pl.BlockSpec(a.shape, lambda i: (0,) * a.ndim)
    return pl.pallas_call(
        _nsa_prompt_kernel,
        grid=(t // Q_BLOCK,),
        in_specs=[pl.BlockSpec((Q_BLOCK, D_ATT), lambda i: (i, 0)),
                  pl.BlockSpec((Q_BLOCK, SMALL_COLS), lambda i: (i, 0)),
                  full(cmp_bf), full(pool), full(kvs_bf), full(kvw_pad_bf)],
        out_specs=pl.BlockSpec((Q_BLOCK, D_ATT), lambda i: (i, 0)),
        out_shape=jax.ShapeDtypeStruct((t, D_ATT), f32),
        scratch_shapes=[pltpu.VMEM((GQA_REP * Q_BLOCK, 1), f32), pltpu.VMEM((GQA_REP * Q_BLOCK, 1), f32),
                        pltpu.VMEM((GQA_REP * Q_BLOCK, HEAD_DIM), f32)],
        compiler_params=_cparams("parallel"),
        name="nsa_prompt",
    )(q, small, cmp_bf, pool, kvs_bf, kvw_pad_bf)


_QROWS = 16
_IDS_LANES = 128
PAGE_ROWS = 128


def _group_queries_1(q, g):
    rows = [q[:, (GQA_REP * g + r) * HEAD_DIM:(GQA_REP * g + r + 1) * HEAD_DIM] for r in range(GQA_REP)]
    rows.append(jnp.zeros((_QROWS - GQA_REP, HEAD_DIM), f32))
    return jnp.concatenate(rows, axis=0) * (HEAD_DIM ** -0.5)


def _group_slopes_1(g):
    r = jnp.minimum(lax.broadcasted_iota(jnp.int32, (_QROWS, 1), 0), GQA_REP - 1)
    return jnp.exp2(-8.0 * (GQA_REP * g + 1 + r).astype(f32) / N_ATT_HEADS)


def _nsa_sample_cmp_kernel(q_ref, cmp_ref, pool_ref, oc_ref, ids_ref, *, past):
    b = pl.program_id(0)
    n_cmp = cmp_ref.shape[1]
    n_old = pool_ref.shape[1]
    q = q_ref[pl.ds(b, 1), :]
    cmpb = cmp_ref[0].astype(bf16)
    d_c = past - (lax.broadcasted_iota(jnp.int32, (1, n_cmp), 1) * CMP_STRIDE + (CMP_BLOCK - 1))
    lane = lax.broadcasted_iota(jnp.int32, (1, _IDS_LANES), 1)
    cur = jnp.full((1, 1), past // SLC_BLOCK, jnp.int32)
    oc_parts = []
    for g in range(N_KV_HEADS):
        q16 = _group_queries_1(q, g).astype(bf16)
        s = _dot_nt(q16, cmpb[:, g * HEAD_DIM:(g + 1) * HEAD_DIM]) - _group_slopes_1(g) * d_c.astype(f32)
        p = _masked_softmax(s, jnp.broadcast_to(d_c >= 0, s.shape))
        o_c = jnp.dot(p.astype(bf16), cmpb[:, (N_KV_HEADS + g) * HEAD_DIM:(N_KV_HEADS + g + 1) * HEAD_DIM],
                      preferred_element_type=f32)
        oc_parts.extend(o_c[r:r + 1] for r in range(GQA_REP))
        p_sum = jnp.sum(p[0:GQA_REP], axis=0, keepdims=True)
        p_slc = jnp.dot(jnp.broadcast_to(p_sum, (8, n_cmp)), pool_ref[...], preferred_element_type=f32,
                        precision=HIGHEST)[0:1]
        p_ext = jnp.concatenate([p_slc, jnp.zeros((1, 128), f32)], axis=1)
        _, picks = _select_blocks(p_ext, cur, past // SLC_BLOCK + 1)
        row = jnp.zeros((1, _IDS_LANES), jnp.int32)
        for k, pick in enumerate(picks):
            row = jnp.where(lane == k, pick, row)
        ids_ref[0, g:g + 1, :] = row
    oc_ref[pl.ds(b, 1), :] = jnp.concatenate(oc_parts, axis=1)


def _nsa_sample_cmp(q, cmp_s, pool, past):
    nb = q.shape[0]
    return pl.pallas_call(
        functools.partial(_nsa_sample_cmp_kernel, past=past),
        grid=(nb,),
        in_specs=[pl.BlockSpec(q.shape, lambda b: (0, 0)),
                  pl.BlockSpec((1,) + cmp_s.shape[1:], lambda b: (b, 0, 0)),
                  pl.BlockSpec(pool.shape, lambda b: (0, 0))],
        out_specs=[pl.BlockSpec((nb, D_ATT), lambda b: (0, 0)),
                   pl.BlockSpec((1, N_KV_HEADS, _IDS_LANES), lambda b: (b, 0, 0))],
        out_shape=[jax.ShapeDtypeStruct((nb, D_ATT), f32),
                   jax.ShapeDtypeStruct((nb, N_KV_HEADS, _IDS_LANES), jnp.int32)],
        compiler_params=_cparams("arbitrary"),
        name="nsa_sample_cmp",
    )(q, cmp_s, pool)


def _softmax_with_new(s, mask, s_new, new_ok):
    s = jnp.where(mask, s, NEG_INF)
    s_new = jnp.where(new_ok, s_new, NEG_INF)
    m = jnp.maximum(jnp.max(s, axis=1, keepdims=True), s_new)
    p = jnp.where(mask, jnp.exp(s - m), 0.0)
    p_new = jnp.where(new_ok, jnp.exp(s_new - m), 0.0)
    l = jnp.sum(p, axis=1, keepdims=True) + p_new
    inv = jnp.where(l > 0.0, 1.0 / jnp.where(l > 0.0, l, 1.0), 0.0)
    return p * inv, p_new * inv


def _nsa_sample_sel_kernel(ids_ref, pt_ref, pool_ref, q_ref, sm_ref, kvs_ref, kvw_ref, win_ref, oc_ref, o_ref,
                           buf, sem, *, past):
    b = pl.program_id(0)
    n_pages = past // PAGE_ROWS
    last_blk = past // SLC_BLOCK
    blocks_per_page = PAGE_ROWS // SLC_BLOCK

    def block_copy(g, k):
        blk = jnp.minimum(ids_ref[(b * N_KV_HEADS + g) * N_SELECT + k], last_blk - 1)
        page = pt_ref[b * n_pages + blk // blocks_per_page]
        off = pl.multiple_of((blk % blocks_per_page) * SLC_BLOCK, SLC_BLOCK)
        return pltpu.make_async_copy(pool_ref.at[page, pl.ds(off, SLC_BLOCK), :],
                                     buf.at[g, pl.ds(k * SLC_BLOCK, SLC_BLOCK), :], sem)

    for g in range(N_KV_HEADS):
        for k in range(N_SELECT):
            block_copy(g, k).start()
    for g in range(N_KV_HEADS):
        for k in range(N_SELECT):
            block_copy(g, k).wait()

    q = q_ref[pl.ds(b, 1), :]
    gates = sm_ref[pl.ds(b, 1), :]
    new_s = kvs_ref[pl.ds(b, 1), :]
    new_w = kvw_ref[pl.ds(b, 1), :]
    n_sel = N_SELECT * SLC_BLOCK
    lane = lax.broadcasted_iota(jnp.int32, (1, n_sel), 1)
    w_buf = win_ref.shape[1]
    d_w = w_buf - lax.broadcasted_iota(jnp.int32, (1, w_buf), 1)
    mask_w = jnp.broadcast_to((d_w < WINDOW) & (past - d_w >= 0), (_QROWS, w_buf))
    always = jnp.full((_QROWS, 1), True)
    o_c = oc_ref[pl.ds(b, 1), :]
    out_parts = []
    for g in range(N_KV_HEADS):
        kcol = slice(g * HEAD_DIM, (g + 1) * HEAD_DIM)
        vcol = slice((N_KV_HEADS + g) * HEAD_DIM, (N_KV_HEADS + g + 1) * HEAD_DIM)
        q16f = _group_queries_1(q, g)
        q16 = q16f.astype(bf16)
        slope = _group_slopes_1(g)

        blk_vec = jnp.zeros((1, n_sel), jnp.int32)
        for k in range(N_SELECT):
            blk_vec = jnp.where(lane // SLC_BLOCK == k, ids_ref[(b * N_KV_HEADS + g) * N_SELECT + k], blk_vec)
        d_s = past - (blk_vec * SLC_BLOCK + lane % SLC_BLOCK)
        mask_s = jnp.broadcast_to((d_s >= 0) & (blk_vec < last_blk), (_QROWS, n_sel))
        has_new = jnp.max(jnp.where(blk_vec == last_blk, 1, 0), axis=1, keepdims=True) > 0
        kb = buf[g].astype(bf16)
        s = _dot_nt(q16, kb[:, kcol]) - slope * d_s.astype(f32)
        s_new = jnp.sum(q16f * new_s[:, kcol], axis=1, keepdims=True)
        p, p_new = _softmax_with_new(s, mask_s, s_new, jnp.broadcast_to(has_new, (_QROWS, 1)))
        o_s = jnp.dot(p.astype(bf16), kb[:, vcol], preferred_element_type=f32) + p_new * new_s[:, vcol]

        wb = win_ref[0].astype(bf16)
        s = _dot_nt(q16, wb[:, kcol]) - slope * d_w.astype(f32)
        s_new = jnp.sum(q16f * new_w[:, kcol], axis=1, keepdims=True)
        p, p_new = _softmax_with_new(s, mask_w, s_new, always)
        o_w = jnp.dot(p.astype(bf16), wb[:, vcol], preferred_element_type=f32) + p_new * new_w[:, vcol]

        for r in range(GQA_REP):
            head = GQA_REP * g + r
            hs = slice(head * HEAD_DIM, (head + 1) * HEAD_DIM)
            out_parts.append(gates[:, 3 * head:3 * head + 1] * o_c[:, hs]
                             + gates[:, 3 * head + 1:3 * head + 2] * o_s[r:r + 1]
                             + gates[:, 3 * head + 2:3 * head + 3] * o_w[r:r + 1])
    o_ref[pl.ds(b, 1), :] = jnp.concatenate(out_parts, axis=1)


def _nsa_sample_sel(ids, page_table, pool_slc, q, small, kvs_new, kvw_new, win_buf, o_c, past):
    nb = q.shape[0]
    full = lambda a: pl.BlockSpec(a.shape, lambda b, ids, pt: (0,) * a.ndim)
    return pl.pallas_call(
        functools.partial(_nsa_sample_sel_kernel, past=past),
        grid_spec=pltpu.PrefetchScalarGridSpec(
            num_scalar_prefetch=2,
            grid=(nb,),
            in_specs=[pl.BlockSpec(memory_space=pl.ANY), full(q), full(small), full(kvs_new), full(kvw_new),
                      pl.BlockSpec((1,) + win_buf.shape[1:], lambda b, ids, pt: (b, 0, 0)), full(o_c)],
            out_specs=pl.BlockSpec((nb, D_ATT), lambda b, ids, pt: (0, 0)),
            scratch_shapes=[pltpu.VMEM((N_KV_HEADS, N_SELECT * SLC_BLOCK, KV_COLS), f32),
                            pltpu.SemaphoreType.DMA(())]),
        out_shape=jax.ShapeDtypeStruct((nb, D_ATT), f32),
        compiler_params=_cparams("arbitrary"),
        name="nsa_sample_sel",
    )(ids, page_table.reshape(-1), pool_slc, q, small, kvs_new, kvw_new, win_buf, o_c)


def _mixout_kernel(att_ref, mh_ref, mo_ref, x_ref, ga_ref, gm_ref, wo_ref, nf_ref, wq_ref, h_ref, xn_ref,
                   qp_ref):
    parts = []
    for h in range(N_ATT_HEADS):
        sl = slice(h * HEAD_DIM, (h + 1) * HEAD_DIM)
        parts.append(_rms(att_ref[:, sl], ga_ref[:, sl]))
    for h in range(N_MLSTM_HEADS):
        sl = slice(h * MLSTM_HEAD_DIM, (h + 1) * MLSTM_HEAD_DIM)
        parts.append(mo_ref[:, sl] * _rms(mh_ref[:, sl], gm_ref[:, sl]))
    cat = jnp.concatenate(parts, axis=1).astype(bf16)
    h1 = x_ref[...] + jnp.dot(cat, wo_ref[...], preferred_element_type=f32)
    h_ref[...] = h1
    xn = _rms(h1, nf_ref[...]).astype(bf16)
    xn_ref[...] = xn
    qp_ref[...] = jnp.dot(xn, wq_ref[...], preferred_element_type=f32)


def _mix_output(att, mh, mo, x, g_att, g_ml, w_out_bf, norm_ffn, w_q_bf, tm):
    n = x.shape[0]
    tok = lambda w: pl.BlockSpec((tm, w), lambda i: (i, 0))
    full = lambda a: pl.BlockSpec(a.shape, lambda i: (0,) * a.ndim)
    return pl.pallas_call(
        _mixout_kernel,
        grid=(n // tm,),
        in_specs=[tok(D_ATT), tok(D_MLSTM), tok(D_MLSTM), tok(D_MODEL), full(g_att), full(g_ml),
                  full(w_out_bf), full(norm_ffn), full(w_q_bf)],
        out_specs=[tok(D_MODEL), tok(D_MODEL), tok(D_MODEL)],
        out_shape=[jax.ShapeDtypeStruct((n, D_MODEL), f32), jax.ShapeDtypeStruct((n, D_MODEL), bf16),
                   jax.ShapeDtypeStruct((n, D_MODEL), f32)],
        compiler_params=_cparams("parallel"),
        name="mix_output",
    )(att, mh, mo, x, g_att, g_ml, w_out_bf, norm_ffn, w_q_bf)


def _topk_rows(s, k):
    n = s.shape[0]
    rows = lax.broadcasted_iota(jnp.int32, s.shape, 0)
    vals, idxs = [], []
    for _ in range(k):
        mx = jnp.max(s, axis=0, keepdims=True)
        idx = jnp.min(jnp.where(s == mx, rows, n), axis=0, keepdims=True)
        vals.append(mx)
        idxs.append(idx)
        s = jnp.where(rows == idx, NEG_INF, s)
    return jnp.concatenate(vals, axis=0), jnp.concatenate(idxs, axis=0)


def _peer_topk_kernel(qp_ref, sub_ref, ei_ref, ej_ref, g_ref):
    K = PEER_TOPK
    half = PEER_D_KEY // 2
    ei, ej, gg = [], [], []
    for h in range(PEER_HEADS):
        sv, si = [], []
        for c in range(2):
            qhc = qp_ref[:, (2 * h + c) * half:(2 * h + c + 1) * half]
            s = _dot_nt(sub_ref[h, c], qhc, precision=HIGHEST)
            v, i = _topk_rows(s, K)
            sv.append(v)
            si.append(i)
        n_t = sv[0].shape[1]
        counts = [K // (a + 1) for a in range(K)]
        n_cand = -(-sum(counts) // 8) * 8
        pad = n_cand - sum(counts)
        cand = jnp.concatenate([sv[0][a:a + 1] + sv[1][0:counts[a]] for a in range(K)]
                               + [jnp.full((pad, n_t), NEG_INF, f32)], axis=0)
        pos_i = jnp.concatenate([jnp.broadcast_to(si[0][a:a + 1], (counts[a], n_t)) for a in range(K)]
                                + [jnp.zeros((pad, n_t), jnp.int32)], axis=0)
        pos_j = jnp.concatenate([si[1][0:counts[a]] for a in range(K)] + [jnp.zeros((pad, n_t), jnp.int32)],
                                axis=0)
        rows = lax.broadcasted_iota(jnp.int32, cand.shape, 0)
        best, bi, bj = [], [], []
        for _ in range(K):
            mx = jnp.max(cand, axis=0, keepdims=True)
            pos = jnp.min(jnp.where(cand == mx, rows, n_cand), axis=0, keepdims=True)
            hit = rows == pos
            best.append(mx)
            bi.append(jnp.max(jnp.where(hit, pos_i, -1), axis=0, keepdims=True))
            bj.append(jnp.max(jnp.where(hit, pos_j, -1), axis=0, keepdims=True))
            cand = jnp.where(hit, NEG_INF, cand)
        best = jnp.concatenate(best, axis=0)
        e = jnp.exp(best - best[0:1])
        gg.append(e / jnp.sum(e, axis=0, keepdims=True))
        ei.append(jnp.concatenate(bi, axis=0))
        ej.append(jnp.concatenate(bj, axis=0))
    ei_ref[...] = jnp.concatenate(ei, axis=0).astype(f32).T
    ej_ref[...] = jnp.concatenate(ej, axis=0).astype(f32).T
    g_ref[...] = jnp.concatenate(gg, axis=0).T


def _peer_topk(qp, sub_keys, tm):
    n = qp.shape[0]
    hk = PEER_HEADS * PEER_TOPK
    return pl.pallas_call(
        _peer_topk_kernel,
        grid=(n // tm,),
        in_specs=[pl.BlockSpec((tm, D_MODEL), lambda i: (i, 0)),
                  pl.BlockSpec(sub_keys.shape, lambda i: (0, 0, 0, 0))],
        out_specs=[pl.BlockSpec((tm, hk), lambda i: (i, 0))] * 3,
        out_shape=[jax.ShapeDtypeStruct((n, hk), f32)] * 3,
        compiler_params=_cparams("parallel"),
        name="peer_topk",
    )(qp, sub_keys)


_GATE_UNROLL = 16


def _peer_dense_kernel(ei_ref, ej_ref, gg_ref, x_ref, u_ref, v_ref, h_ref, nf_ref, o_ref, acc_ref, g_s):
    e = pl.program_id(1)
    tm = x_ref.shape[0]
    et = u_ref.shape[0]
    nk = PEER_N_KEYS
    half = nk // 2
    assert half % (et // nk) == 0

    @pl.when(e == 0)
    def _():
        acc_ref[...] = jnp.zeros_like(acc_ref)
        sub = lax.broadcasted_iota(jnp.int32, (nk, ei_ref.shape[1]), 0).astype(f32)

        def body(t, carry):
            a = jnp.where(sub == ei_ref[pl.ds(t, 1), :], 1.0, 0.0).astype(bf16)
            b = jnp.where(sub == ej_ref[pl.ds(t, 1), :], gg_ref[pl.ds(t, 1), :], 0.0).astype(bf16)
            gt = _dot_nt(a, b).astype(bf16).astype(f32)
            hi = pltpu.bitcast(gt[0:half], jnp.uint32)
            lo = pltpu.bitcast(gt[half:nk], jnp.uint32)
            g_s[pl.ds(pl.multiple_of(t * half, half), half), :] = hi | (lo >> 16)
            return carry

        lax.fori_loop(0, tm, body, 0, unroll=_GATE_UNROLL)

    rows = et // nk
    row0 = e * rows
    shift = jnp.where(row0 >= half, 16, 0).astype(jnp.uint32)
    words = jnp.concatenate([g_s[pl.ds(row0 % half + r, tm, stride=half), :] for r in range(rows)], axis=1)
    g = pltpu.bitcast((words << shift) & jnp.uint32(0xFFFF0000), f32)
    act = jax.nn.gelu(_dot_nt(x_ref[...], u_ref[...]))
    acc_ref[...] += jnp.dot((g * act).astype(bf16), v_ref[...], preferred_element_type=f32)

    @pl.when(e == pl.num_programs(1) - 1)
    def _():
        o_ref[...] = _rms(h_ref[...] + acc_ref[...], nf_ref[...])


def _peer_dense(ei, ej, gg, xn_bf, u_bf, v_bf, h1, norm_final, tm, et):
    n = xn_bf.shape[0]
    n_exp = u_bf.shape[0]
    hk = ei.shape[1]
    tok = lambda w: pl.BlockSpec((tm, w), lambda i, e: (i, 0))
    return pl.pallas_call(
        _peer_dense_kernel,
        grid=(n // tm, n_exp // et),
        in_specs=[tok(hk), tok(hk), tok(hk), tok(D_MODEL),
                  pl.BlockSpec((et, D_MODEL), lambda i, e: (e, 0)),
                  pl.BlockSpec((et, D_MODEL), lambda i, e: (e, 0)),
                  tok(D_MODEL),
                  pl.BlockSpec((1, D_MODEL), lambda i, e: (0, 0))],
        out_specs=tok(D_MODEL),
        out_shape=jax.ShapeDtypeStruct((n, D_MODEL), f32),
        scratch_shapes=[pltpu.VMEM((tm, D_MODEL), f32),
                        pltpu.VMEM((tm * PEER_N_KEYS // 2, PEER_N_KEYS), jnp.uint32)],
        compiler_params=_cparams("parallel", "arbitrary"),
        name="peer_dense",
    )(ei, ej, gg, xn_bf, u_bf, v_bf, h1, norm_final)


def _channel_mix_and_norm(h1, xn_bf, qp, sub_keys, u_bf, v_bf, norm_final, tm_topk, tm, et):
    ei, ej, gg = _peer_topk(qp, sub_keys, tm_topk)
    return _peer_dense(ei, ej, gg, xn_bf, u_bf, v_bf, h1, norm_final, tm, et)


PROMPT_TM = 512
PEER_TM = 512
PEER_ET = 1024
MLSTM_CHUNK = 256
SAMPLE_PAD = 128


def kernel(x_prompt, x_sample, cache_cmp_kv, cache_slc_kv, cache_win_kv, state_mlstm_C, state_mlstm_n,
           state_mlstm_m, page_table, norm_mix, w_in, b_igate, b_fgate, pe_cmp, w_cmp1, w_cmp2, norm_att_out,
           norm_mlstm_out, w_out, norm_ffn, peer_wq, peer_subkeys, peer_u, peer_v, norm_final):
    assert x_prompt.shape[0] == 1 and x_sample.shape[1] == 1 and w_in.shape[0] == 1
    _, t, d = x_prompt.shape
    nb = x_sample.shape[0]
    n_pool = cache_cmp_kv.shape[1]
    past = page_table.shape[1] * PAGE_ROWS
    w_buf = cache_win_kv.shape[2]
    row = (2, N_KV_HEADS, HEAD_DIM)
    NH, DH = N_MLSTM_HEADS, MLSTM_HEAD_DIM
    g0 = N_GATE_COLS

    w_re = _relayout_w_in(w_in[0])
    bias = _bias_row(b_igate[0], b_fgate[0])
    cw = _compress_weights(pe_cmp[0], w_cmp1[0], w_cmp2[0])
    nm = norm_mix[0][None]
    ga, gm, nf, nfin = norm_att_out[0][None], norm_mlstm_out[0][None], norm_ffn[0][None], norm_final[None]
    w_out_bf, w_q_bf = w_out[0].astype(bf16), peer_wq[0].astype(bf16)
    u_bf, v_bf = peer_u[0].astype(bf16), peer_v[0].astype(bf16)
    sub_keys = peer_subkeys[0]

    xp = x_prompt.reshape(t, d)
    q, kvc, kvs, kvw, mq, mk, mv, mo, sm = _project(xp, nm, w_re, bias, PROMPT_TM)
    cmp_p = _compress_prompt(kvc.reshape(1, t // CMP_STRIDE, _CHUNK_COLS), cw)[0]
    pool_p = _pool_matrix(t // CMP_STRIDE, t // SLC_BLOCK)
    kvw_pad = jnp.pad(kvw, ((WINDOW, 0), (0, 0))).astype(bf16)
    att = _nsa_prompt(q, sm, cmp_p.astype(bf16), pool_p, kvs.astype(bf16), kvw_pad)
    gates_c = sm[:, g0:g0 + 2 * NH]
    mh, c_p, n_p, m_p = _mlstm_prompt(mq, mk, mv, gates_c, gates_c.T, MLSTM_CHUNK)
    h1, xn_bf, qp = _mix_output(att, mh, mo, xp, ga, gm, w_out_bf, nf, w_q_bf, 256)
    y_p = _channel_mix_and_norm(h1, xn_bf, qp, sub_keys, u_bf, v_bf, nfin, 256, PEER_TM, PEER_ET)
    w_keep = min(WINDOW, t)
    outs_p = (kvc.reshape((1, 1, t) + row), kvs.reshape((1, 1, t) + row),
              kvw[t - w_keep:].reshape((1, 1, w_keep) + row),
              c_p[None, None], n_p[:NH][None, None], m_p[:NH, 0][None, None])

    xs = x_sample.reshape(nb, d)
    q, kvc_s, kvs_s, kvw_s, mq, mk, mv, mo, sm = _project(xs, nm, w_re, bias, nb)
    cmp_s = _compress_paged(page_table, cache_cmp_kv[0].reshape(n_pool, _PAGE_CHUNKS, _CHUNK_COLS), cw)
    pool_s = _pool_matrix(past // CMP_STRIDE, past // SLC_BLOCK)
    o_c, ids = _nsa_sample_cmp(q, cmp_s, pool_s, past)
    win_buf = cache_win_kv[0].reshape(nb, w_buf, KV_COLS)
    att = _nsa_sample_sel(ids[:, :, :N_SELECT].reshape(-1), page_table,
                          cache_slc_kv[0].reshape(n_pool, PAGE_ROWS, KV_COLS), q, sm, kvs_s, kvw_s, win_buf, o_c,
                          past)
    mh, c_s, n_s, m_s = _mlstm_sample(mq, mk, mv, sm[:, g0:g0 + 2 * NH], state_mlstm_C[0], state_mlstm_n[0],
                                      state_mlstm_m[0])
    h1, xn_bf, qp = _mix_output(att, mh, mo, xs, ga, gm, w_out_bf, nf, w_q_bf, nb)
    padr = lambda a: jnp.pad(a, ((0, SAMPLE_PAD - nb), (0, 0)))
    y_s = _channel_mix_and_norm(padr(h1), padr(xn_bf), padr(qp), sub_keys, u_bf, v_bf, nfin, SAMPLE_PAD,
                                SAMPLE_PAD, PEER_ET)[:nb]
    win_all = jnp.concatenate([win_buf, kvw_s[:, None, :]], axis=1)
    w_keep_s = min(WINDOW, w_buf + 1)
    outs_s = (kvc_s.reshape((1, nb, 1) + row), kvs_s.reshape((1, nb, 1) + row),
              win_all[:, w_buf + 1 - w_keep_s:].reshape((1, nb, w_keep_s) + row),
              c_s[None], n_s[None], m_s[None])

    return (y_p.reshape(1, t, d), y_s.reshape(nb, 1, d),
            outs_p[0], outs_s[0], outs_p[1], outs_s[1], outs_p[2], outs_s[2],
            outs_p[3], outs_s[3], outs_p[4], outs_s[4], outs_p[5], outs_s[5])
```

```python
import functools

import jax
import jax.numpy as jnp
import numpy as np
from jax import lax
from jax.experimental import pallas as pl
from jax.experimental.pallas import tpu as pltpu

f32 = jnp.float32
bf16 = jnp.bfloat16
HIGHEST = lax.Precision.HIGHEST

D_MODEL = 1024
HEAD_DIM = 64
N_ATT_HEADS = 8
N_KV_HEADS = 2
GQA_REP = 4
D_ATT = 512
KV_COLS = 256
CMP_BLOCK = 32
CMP_STRIDE = 16
CMP_HIDDEN = 128
SLC_BLOCK = 64
N_SELECT = 16
WINDOW = 512
Q_BLOCK = 128
FORCE_BONUS = 1000.0
N_MLSTM_HEADS = 4
MLSTM_HEAD_DIM = 128
D_MLSTM = 512
N_GATE_COLS = 3 * N_ATT_HEADS
SPLIT_SIZES = (D_ATT, KV_COLS, KV_COLS, KV_COLS, N_GATE_COLS, D_MLSTM, D_MLSTM, D_MLSTM, D_MLSTM,
               N_MLSTM_HEADS, N_MLSTM_HEADS)
SMALL_COLS = 128
PEER_N_KEYS = 128
PEER_HEADS = 8
PEER_TOPK = 16
PEER_D_KEY = 128
NORM_EPS = 1e-6
NEG_INF = -1e30
VMEM_LIMIT = 56 * 1024 * 1024


def _cparams(*sem):
    return pltpu.CompilerParams(dimension_semantics=sem, vmem_limit_bytes=VMEM_LIMIT)


def _rms(x, w):
    return x * lax.rsqrt(jnp.mean(x * x, axis=-1, keepdims=True) + NORM_EPS) * w


_P_Q, _P_KVC, _P_KVS, _P_KVW, _P_MQ, _P_MK, _P_MV, _P_MO, _P_SM, _P_END = (
    0, 512, 768, 1024, 1280, 1792, 2304, 2816, 3328, 3456)


def _relayout_w_in(w_in):
    pts = np.cumsum(SPLIT_SIZES)[:-1].tolist()
    q, kc, ks, kw, ga, mq, mk, mv, mo, ig, fg = jnp.split(w_in, pts, axis=-1)
    pad = jnp.zeros((w_in.shape[0], SMALL_COLS - N_GATE_COLS - 2 * N_MLSTM_HEADS), w_in.dtype)
    return jnp.concatenate([q, kc, ks, kw, mq, mk, mv, mo, ga, ig, fg, pad], axis=-1).astype(bf16)


def _bias_row(b_i, b_f):
    pad = jnp.zeros((SMALL_COLS - N_GATE_COLS - 2 * N_MLSTM_HEADS,), f32)
    return jnp.concatenate([jnp.zeros((N_GATE_COLS,), f32), b_i, b_f, pad])[None, :]


def _proj_kernel(x_ref, nw_ref, w_ref, b_ref, q_ref, kvc_ref, kvs_ref, kvw_ref, mq_ref, mk_ref, mv_ref,
                 mo_ref, sm_ref):
    xb = _rms(x_ref[...], nw_ref[...]).astype(bf16)

    def mm(lo, hi):
        return jnp.dot(xb, w_ref[:, lo:hi], preferred_element_type=f32)

    q_ref[...] = mm(_P_Q, _P_KVC)
    kvc_ref[...] = mm(_P_KVC, _P_KVS)
    kvs_ref[...] = mm(_P_KVS, _P_KVW)
    kvw_ref[...] = mm(_P_KVW, _P_MQ)
    mq_ref[...] = mm(_P_MQ, _P_MK)
    mk_ref[...] = mm(_P_MK, _P_MV)
    mv_ref[...] = mm(_P_MV, _P_MO)
    mo_ref[...] = jax.nn.sigmoid(mm(_P_MO, _P_SM))
    s = mm(_P_SM, _P_END) + b_ref[...]
    col = lax.broadcasted_iota(jnp.int32, s.shape, 1)
    sm_ref[...] = jnp.where(col < N_GATE_COLS, jax.nn.sigmoid(s), s)


def _project(x, norm_w, w_re, bias_row, tm):
    n = x.shape[0]
    widths = (D_ATT, KV_COLS, KV_COLS, KV_COLS, D_MLSTM, D_MLSTM, D_MLSTM, D_MLSTM, SMALL_COLS)
    return pl.pallas_call(
        _proj_kernel,
        grid=(n // tm,),
        in_specs=[pl.BlockSpec((tm, D_MODEL), lambda i: (i, 0)),
                  pl.BlockSpec((1, D_MODEL), lambda i: (0, 0)),
                  pl.BlockSpec((D_MODEL, _P_END), lambda i: (0, 0)),
                  pl.BlockSpec((1, SMALL_COLS), lambda i: (0, 0))],
        out_specs=[pl.BlockSpec((tm, w), lambda i: (i, 0)) for w in widths],
        out_shape=[jax.ShapeDtypeStruct((n, w), f32) for w in widths],
        compiler_params=_cparams("parallel"),
        name="proj",
    )(x, norm_w, w_re, bias_row)


_HID_COLS = 2 * N_KV_HEADS * CMP_HIDDEN
PAGE_ROWS = 128


def _compress_weights(pe, w1, w2):
    eye = jnp.eye(2, dtype=f32)

    def big1(w):
        return jnp.einsum('cldh,ce,gf->lefdcgh', w, eye, eye).reshape(CMP_STRIDE, KV_COLS, _HID_COLS).astype(bf16)

    def pe_rows(p):
        row = jnp.broadcast_to(p[:, :, None, :], (CMP_STRIDE, 2, N_KV_HEADS, HEAD_DIM)).reshape(CMP_STRIDE, 1, KV_COLS)
        return jnp.concatenate([row, jnp.zeros((CMP_STRIDE, 7, KV_COLS), f32)], axis=1)

    w2_big = jnp.einsum('chd,ce,gf->efhcgd', w2, eye, eye).reshape(_HID_COLS, KV_COLS).astype(bf16)
    pe_all = jnp.stack([pe_rows(pe[:CMP_STRIDE]), pe_rows(pe[CMP_STRIDE:])], axis=0).astype(bf16)
    return big1(w1[:, :CMP_STRIDE]), big1(w1[:, CMP_STRIDE:]), pe_all, w2_big


def _compress_math(x_lo, x_hi, w1a_ref, w1b_ref, pe_ref, w2_ref):
    n_chunks = x_lo.shape[0] // CMP_STRIDE
    first = jnp.zeros((n_chunks, _HID_COLS), f32)
    second = jnp.zeros((n_chunks, _HID_COLS), f32)
    pe_term = jnp.zeros((8, _HID_COLS), f32)
    for l in range(CMP_STRIDE):
        rows_l = pl.ds(l, n_chunks, stride=CMP_STRIDE)
        x = jnp.concatenate([x_lo[rows_l, :], x_hi[rows_l, :]], axis=1).astype(bf16)
        first += jnp.dot(x, w1a_ref[l], preferred_element_type=f32)
        second += jnp.dot(x, w1b_ref[l], preferred_element_type=f32)
        pe_term += (jnp.dot(pe_ref[0, l], w1a_ref[l], preferred_element_type=f32)
                    + jnp.dot(pe_ref[1, l], w1b_ref[l], preferred_element_type=f32))
    h = jax.nn.gelu(first + pltpu.roll(second, n_chunks - 1, 0) + pe_term[0:1])
    return jnp.dot(h.astype(bf16), w2_ref[...], preferred_element_type=f32)


def _compress_kernel(x_lo, x_hi, w1a_ref, w1b_ref, pe_ref, w2_ref, o_ref):
    o_ref[0] = _compress_math(x_lo.at[0], x_hi.at[0], w1a_ref, w1b_ref, pe_ref, w2_ref)


def _compress_paged_kernel(pt_ref, pool_ref, w1a_ref, w1b_ref, pe_ref, w2_ref, o_ref, x_s, sem):
    b = pl.program_id(0)
    n_pages = x_s.shape[1] // PAGE_ROWS
    half = KV_COLS // 2

    def page_copy(p, hf):
        return pltpu.make_async_copy(pool_ref.at[pt_ref[b * n_pages + p], :, pl.ds(hf * half, half)],
                                     x_s.at[hf, pl.ds(pl.multiple_of(p * PAGE_ROWS, PAGE_ROWS), PAGE_ROWS)],
                                     sem)

    def start(p, c):
        page_copy(p, 0).start()
        page_copy(p, 1).start()
        return c

    def wait(p, c):
        page_copy(p, 0).wait()
        page_copy(p, 1).wait()
        return c

    lax.fori_loop(0, n_pages, start, 0)
    lax.fori_loop(0, n_pages, wait, 0)
    o_ref[0] = _compress_math(x_s.at[0], x_s.at[1], w1a_ref, w1b_ref, pe_ref, w2_ref)


def _compress_paged(page_table, pool_rows, cw):
    nb, n_pages = page_table.shape
    n_chunks = n_pages * PAGE_ROWS // CMP_STRIDE
    full = lambda a: pl.BlockSpec(a.shape, lambda b, pt: (0,) * a.ndim)
    return pl.pallas_call(
        _compress_paged_kernel,
        grid_spec=pltpu.PrefetchScalarGridSpec(
            num_scalar_prefetch=1,
            grid=(nb,),
            in_specs=[pl.BlockSpec(memory_space=pl.ANY)] + [full(a) for a in cw],
            out_specs=pl.BlockSpec((1, n_chunks, KV_COLS), lambda b, pt: (b, 0, 0)),
            scratch_shapes=[pltpu.VMEM((2, n_pages * PAGE_ROWS, KV_COLS // 2), f32),
                            pltpu.SemaphoreType.DMA(())]),
        out_shape=jax.ShapeDtypeStruct((nb, n_chunks, KV_COLS), f32),
        compiler_params=_cparams("arbitrary"),
        name="compress_paged",
    )(page_table.reshape(-1), pool_rows, *cw)


def _compress_prompt(kv_rows, cw):
    nb, n_rows, _ = kv_rows.shape
    n_chunks = n_rows // CMP_STRIDE
    full = lambda a: pl.BlockSpec(a.shape, lambda b: (0,) * a.ndim)
    return pl.pallas_call(
        _compress_kernel,
        grid=(nb,),
        in_specs=[pl.BlockSpec((1, n_rows, KV_COLS // 2), lambda b: (b, 0, 0)),
                  pl.BlockSpec((1, n_rows, KV_COLS // 2), lambda b: (b, 0, 1))] + [full(a) for a in cw],
        out_specs=pl.BlockSpec((1, n_chunks, KV_COLS), lambda b: (b, 0, 0)),
        out_shape=jax.ShapeDtypeStruct((nb, n_chunks, KV_COLS), f32),
        compiler_params=_cparams("parallel"),
        name="compress_prompt",
    )(kv_rows, kv_rows, *cw)


def _dot_t0(a, b, **kw):
    return lax.dot_general(a, b, (((0,), (0,)), ((), ())), preferred_element_type=f32, **kw)


def _dot_nt(a, b, **kw):
    return lax.dot_general(a, b, (((1,), (1,)), ((), ())), preferred_element_type=f32, **kw)


def _mlstm_chunk_kernel(q_ref, k_ref, v_ref, gc_ref, gr_ref, h_ref, c_out, n_out, m_out, c_s, n_s, m_s):
    L = q_ref.shape[0]
    NH, DH = N_MLSTM_HEADS, MLSTM_HEAD_DIM

    @pl.when(pl.program_id(0) == 0)
    def _():
        c_s[...] = jnp.zeros_like(c_s)
        n_s[...] = jnp.zeros_like(n_s)
        m_s[...] = jnp.zeros_like(m_s)

    row = lax.broadcasted_iota(jnp.int32, (L, L), 0)
    col = lax.broadcasted_iota(jnp.int32, (L, L), 1)
    causal = col <= row
    gc = gc_ref[...]
    gr = gr_ref[...]
    lf_c = jax.nn.log_sigmoid(gc[:, NH:2 * NH])
    lf_r = jax.nn.log_sigmoid(gr[NH:2 * NH, :])
    f_c = jnp.dot(causal.astype(f32), lf_c, preferred_element_type=f32, precision=HIGHEST)
    f_r = jnp.dot(lf_r, (row <= col).astype(f32), preferred_element_type=f32, precision=HIGHEST)
    for h in range(NH):
        sl = slice(h * DH, (h + 1) * DH)
        fc, fr = f_c[:, h:h + 1], f_r[h:h + 1, :]
        ic, ir = gc[:, h:h + 1], gr[h:h + 1, :]
        m_prev = m_s[h:h + 1, 0:1]
        qh = q_ref[:, sl].astype(bf16)
        kh = k_ref[:, sl] * (DH ** -0.5)
        vh = v_ref[:, sl].astype(bf16)
        log_d = fc - fr + ir
        m_t = jnp.maximum(fc + m_prev, jnp.max(jnp.where(causal, log_d, NEG_INF), axis=1, keepdims=True))
        w = jnp.where(causal, jnp.exp(log_d - m_t), 0.0) * _dot_nt(qh, kh.astype(bf16))
        inter = jnp.exp(fc + m_prev - m_t)
        c_old = c_s[h]
        n_old = n_s[h:h + 1, :]
        num = (jnp.dot(w.astype(bf16), vh, preferred_element_type=f32)
               + inter * jnp.dot(qh, c_old.astype(bf16), preferred_element_type=f32))
        den = (jnp.sum(w, axis=1, keepdims=True)
               + inter * jnp.sum(q_ref[:, sl] * n_old, axis=1, keepdims=True))
        h_ref[:, sl] = num / jnp.maximum(jnp.abs(den), jnp.exp(-m_t))
        f_tot = fc[L - 1:L, :]
        m_new = m_t[L - 1:L, :]
        kw = kh * jnp.exp(f_tot - fc + ic - m_new)
        decay = jnp.exp(f_tot + m_prev - m_new)
        c_new = decay * c_old + _dot_t0(kw.astype(bf16), vh)
        n_new = decay * n_old + jnp.sum(kw, axis=0, keepdims=True)
        c_s[h] = c_new
        n_s[h:h + 1, :] = n_new
        m_s[h:h + 1, :] = jnp.broadcast_to(m_new, (1, DH))
        c_out[h] = c_new
    n_out[...] = n_s[...]
    m_out[...] = m_s[...]


def _mlstm_prompt(mq, mk, mv, gates_c, gates_r, chunk):
    t = mq.shape[0]
    NH, DH = N_MLSTM_HEADS, MLSTM_HEAD_DIM
    tok = lambda w: pl.BlockSpec((chunk, w), lambda c: (c, 0))
    return pl.pallas_call(
        _mlstm_chunk_kernel,
        grid=(t // chunk,),
        in_specs=[tok(D_MLSTM), tok(D_MLSTM), tok(D_MLSTM), tok(2 * NH),
                  pl.BlockSpec((2 * NH, chunk), lambda c: (0, c))],
        out_specs=[tok(D_MLSTM),
                   pl.BlockSpec((NH, DH, DH), lambda c: (0, 0, 0)),
                   pl.BlockSpec((8, DH), lambda c: (0, 0)),
                   pl.BlockSpec((8, DH), lambda c: (0, 0))],
        out_shape=[jax.ShapeDtypeStruct((t, D_MLSTM), f32),
                   jax.ShapeDtypeStruct((NH, DH, DH), f32),
                   jax.ShapeDtypeStruct((8, DH), f32),
                   jax.ShapeDtypeStruct((8, DH), f32)],
        scratch_shapes=[pltpu.VMEM((NH, DH, DH), f32), pltpu.VMEM((8, DH), f32), pltpu.VMEM((8, DH), f32)],
        compiler_params=_cparams("arbitrary"),
        name="mlstm_prompt",
    )(mq, mk, mv, gates_c, gates_r)


def _mlstm_step_kernel(q_ref, k_ref, v_ref, g_ref, c_ref, n_ref, m_ref, h_ref, c_out, n_out, m_out):
    NH, DH = N_MLSTM_HEADS, MLSTM_HEAD_DIM
    b = pl.program_id(0)
    row8 = lax.broadcasted_iota(jnp.int32, (8, DH), 0)
    g = g_ref[pl.ds(b, 1), :]
    m_row = m_ref[pl.ds(b, 1), :]
    m_new_row = jnp.zeros((1, NH), f32)
    lane4 = lax.broadcasted_iota(jnp.int32, (1, NH), 1)
    q_row, k_row, v_row = q_ref[pl.ds(b, 1), :], k_ref[pl.ds(b, 1), :], v_ref[pl.ds(b, 1), :]
    h_parts = []
    for h in range(NH):
        sl = slice(h * DH, (h + 1) * DH)
        q = q_row[:, sl]
        k = k_row[:, sl] * (DH ** -0.5)
        v = v_row[:, sl]
        ig = g[:, h:h + 1]
        lf = jax.nn.log_sigmoid(g[:, NH + h:NH + h + 1])
        m_prev = m_row[:, h:h + 1]
        c_old = c_ref[0, h]
        n_old = n_ref[0, h:h + 1, :]
        m_t = jnp.maximum(lf + m_prev, ig)
        w = jnp.exp(ig - m_t) * jnp.sum(q * k, axis=1, keepdims=True)
        inter = jnp.exp(lf + m_prev - m_t)
        q8 = jnp.where(row8 == 0, q, 0.0)
        qc = jnp.dot(q8, c_old, preferred_element_type=f32, precision=HIGHEST)[0:1]
        num = w * v + inter * qc
        den = w + inter * jnp.sum(q * n_old, axis=1, keepdims=True)
        h_parts.append(num / jnp.maximum(jnp.abs(den), jnp.exp(-m_t)))
        kw = k * jnp.exp(ig - m_t)
        decay = jnp.exp(lf + m_prev - m_t)
        kw8 = jnp.where(row8 == 0, kw, 0.0)
        v8 = jnp.where(row8 == 0, v, 0.0)
        c_out[0, h] = decay * c_old + _dot_t0(kw8, v8, precision=HIGHEST)
        n_out[0, h:h + 1, :] = decay * n_old + kw
        m_new_row = jnp.where(lane4 == h, m_t, m_new_row)
    h_ref[pl.ds(b, 1), :] = jnp.concatenate(h_parts, axis=1)
    m_out[pl.ds(b, 1), :] = m_new_row


def _mlstm_sample(mq, mk, mv, gates, c0, n0, m0):
    nb = mq.shape[0]
    NH, DH = N_MLSTM_HEADS, MLSTM_HEAD_DIM
    full = lambda a: pl.BlockSpec(a.shape, lambda b: (0,) * a.ndim)
    return pl.pallas_call(
        _mlstm_step_kernel,
        grid=(nb,),
        in_specs=[full(mq), full(mk), full(mv), full(gates),
                  pl.BlockSpec((1, NH, DH, DH), lambda b: (b, 0, 0, 0)),
                  pl.BlockSpec((1, NH, DH), lambda b: (b, 0, 0)),
                  full(m0)],
        out_specs=[pl.BlockSpec((nb, D_MLSTM), lambda b: (0, 0)),
                   pl.BlockSpec((1, NH, DH, DH), lambda b: (b, 0, 0, 0)),
                   pl.BlockSpec((1, NH, DH), lambda b: (b, 0, 0)),
                   pl.BlockSpec((nb, NH), lambda b: (0, 0))],
        out_shape=[jax.ShapeDtypeStruct((nb, D_MLSTM), f32),
                   jax.ShapeDtypeStruct((nb, NH, DH, DH), f32),
                   jax.ShapeDtypeStruct((nb, NH, DH), f32),
                   jax.ShapeDtypeStruct((nb, NH), f32)],
        compiler_params=_cparams("arbitrary"),
        name="mlstm_sample",
    )(mq, mk, mv, gates, c0, n0, m0)


SLC_TILE = 512


def _masked_softmax(s, mask):
    sm = jnp.where(mask, s, NEG_INF)
    mx = jnp.max(sm, axis=-1, keepdims=True)
    e = jnp.exp(sm - mx)
    p = e / jnp.sum(e, axis=-1, keepdims=True)
    return jnp.where(mx > 0.5 * NEG_INF, p, 0.0)


def _pool_matrix(n_cmp_rows, n_slc):
    i = np.arange(n_cmp_rows)[:, None]
    j = np.arange(n_slc)[None, :]
    ratio = SLC_BLOCK // CMP_STRIDE
    return jnp.asarray(((i >= ratio * j - 1) & (i <= ratio * j + ratio - 1)).astype(np.float32))


def _select_blocks(p_slc, cur, n_valid_lanes):
    nb = p_slc.shape[1]
    blk = lax.broadcasted_iota(jnp.int32, p_slc.shape, 1)
    valid = blk <= cur
    forced = (blk == 0) | (blk == cur) | (blk == cur - 1)
    score = jnp.where(valid, p_slc + jnp.where(forced, FORCE_BONUS, 0.0), -1.0)
    score = jnp.where(blk < n_valid_lanes, score, -2.0)
    blk_f = blk.astype(f32)
    sel = jnp.zeros(p_slc.shape, f32)
    picks = []
    for _ in range(N_SELECT):
        mx = jnp.max(score, axis=1, keepdims=True)
        idx = jnp.min(jnp.where(score == mx, blk_f, float(nb)), axis=1, keepdims=True)
        hit = blk_f == idx
        sel = jnp.where(hit, 1.0, sel)
        score = jnp.where(hit, -3.0, score)
        picks.append(idx.astype(jnp.int32))
    return sel, picks


_SUB = 128
_M_FLOOR = -1e20


def _with_pos_feature(k, index0=0):
    n = k.shape[0]
    pos = ((jnp.arange(n) + index0) % _SUB).astype(f32)[:, None]
    return jnp.concatenate([k, pos, jnp.zeros((n, _SUB - HEAD_DIM - 1), f32)], axis=1).astype(bf16)


def _colmax8(x):
    out = x[0:8]
    for r in range(8, x.shape[0], 8):
        out = jnp.maximum(out, x[r:r + 8])
    return out


def _colsum8(x):
    out = x[0:8]
    for r in range(8, x.shape[0], 8):
        out = out + x[r:r + 8]
    return out


_PSUM_PAD = 8
_SLC_CHAINS = 1


def _nsa_prompt_t_kernel(qT_ref, gT_ref, kc_ref, vcT_ref, ks_ref, vsT_ref, kw_ref, vwT_ref, o_ref,
                         sc_s, psum_s, sel_s, m_s, l_s, acc_s):
    tq = qT_ref.shape[1]
    L = GQA_REP * tq
    n_cmp = kc_ref.shape[1]
    n_slc = sel_s.shape[0]
    assert tq == _SUB and n_cmp * CMP_STRIDE == n_slc * SLC_BLOCK
    i = pl.program_id(0)
    start = pl.multiple_of(i * tq, tq)
    lane = lax.broadcasted_iota(jnp.int32, (1, L), 1)
    t_lane = lane % tq
    qpos = start + t_lane
    qpos_f = qpos.astype(f32)
    qpos_t = qpos[:, 0:tq]
    sub_l = lax.broadcasted_iota(jnp.int32, (_SUB, L), 0)
    sub_t = lax.broadcasted_iota(jnp.int32, (_SUB, tq), 0)
    feat_row = lax.broadcasted_iota(jnp.int32, (_SUB - HEAD_DIM, L), 0) == 0
    n_tiles = (start + tq + SLC_TILE - 1) // SLC_TILE
    nc_blocks = (start // CMP_STRIDE + (tq - CMP_BLOCK) // CMP_STRIDE) // _SUB + 1
    blocks_per_tile = SLC_TILE // SLC_BLOCK

    for g in range(N_KV_HEADS):
        vrows = slice(g * HEAD_DIM, (g + 1) * HEAD_DIM)
        slope = jnp.exp2(-8.0 * (GQA_REP * g + 1 + lane // tq).astype(f32) / N_ATT_HEADS)
        q_rows = jnp.concatenate(
            [qT_ref[(GQA_REP * g + r) * HEAD_DIM:(GQA_REP * g + r + 1) * HEAD_DIM, :] for r in range(GQA_REP)],
            axis=1).astype(f32) * (HEAD_DIM ** -0.5)
        q_pos = jnp.concatenate([q_rows, jnp.where(feat_row, slope, 0.0)], axis=0).astype(bf16)
        q_cmp = jnp.concatenate([q_rows, jnp.where(feat_row, slope * CMP_STRIDE, 0.0)], axis=0).astype(bf16)

        def cmp_scores(cb, m8):
            r0 = pl.multiple_of(cb * _SUB, _SUB)
            end0 = r0 * CMP_STRIDE + (CMP_BLOCK - 1)
            off = slope * (end0.astype(f32) - qpos_f)
            vis = (end0 + sub_l * CMP_STRIDE) <= qpos
            s = jnp.dot(kc_ref[g, pl.ds(r0, _SUB), :], q_cmp, preferred_element_type=f32)
            s = jnp.where(vis, s + off, NEG_INF)
            sc_s[pl.ds(r0, _SUB), :] = s
            return jnp.maximum(m8, _colmax8(s))

        m8 = lax.fori_loop(0, nc_blocks, cmp_scores, jnp.full((8, L), NEG_INF, f32))
        m_c = jnp.maximum(jnp.max(m8, axis=0, keepdims=True), _M_FLOOR)

        def cmp_exp(cb, carry):
            l8, o_acc = carry
            r0 = pl.multiple_of(cb * _SUB, _SUB)
            e = jnp.exp(sc_s[pl.ds(r0, _SUB), :] - m_c)
            sc_s[pl.ds(r0, _SUB), :] = e
            o_acc = o_acc + jnp.dot(vcT_ref[vrows, pl.ds(r0, _SUB)], e.astype(bf16), preferred_element_type=f32)
            return l8 + _colsum8(e), o_acc

        l8, o_c = lax.fori_loop(0, nc_blocks, cmp_exp,
                                (jnp.zeros((8, L), f32), jnp.zeros((HEAD_DIM, L), f32)))
        l_c = jnp.sum(l8, axis=0, keepdims=True)
        inv_c = jnp.where(l_c > 0.0, 1.0 / jnp.where(l_c > 0.0, l_c, 1.0), 0.0)
        o_c = o_c * inv_c
        psum_s[...] = jnp.zeros_like(psum_s)

        def cmp_group_sum(cb, carry):
            r0 = pl.multiple_of(cb * _SUB, _SUB)
            p = sc_s[pl.ds(r0, _SUB), :] * inv_c
            psum_s[pl.ds(r0 + _PSUM_PAD, _SUB), :] = ((p[:, 0:tq] + p[:, tq:2 * tq])
                                                      + (p[:, 2 * tq:3 * tq] + p[:, 3 * tq:4 * tq]))
            return carry

        lax.fori_loop(0, nc_blocks, cmp_group_sum, 0)
        ratio = SLC_BLOCK // CMP_STRIDE
        p_slc = psum_s[pl.ds(_PSUM_PAD - 1, n_slc, stride=ratio), :]
        for c in range(ratio):
            p_slc = p_slc + psum_s[pl.ds(_PSUM_PAD + c, n_slc, stride=ratio), :]

        blk = lax.broadcasted_iota(jnp.int32, (n_slc, tq), 0).astype(f32)
        cur = (qpos_t // SLC_BLOCK).astype(f32)
        forced = (blk == 0.0) | (blk == cur) | (blk == cur - 1.0)
        score = jnp.where(blk <= cur, p_slc + jnp.where(forced, FORCE_BONUS, 0.0), -1.0)
        sel = jnp.zeros((n_slc, tq), f32)
        for _ in range(N_SELECT):
            mx = jnp.max(score, axis=0, keepdims=True)
            idx = jnp.min(jnp.where(score == mx, blk, float(n_slc)), axis=0, keepdims=True)
            hit = blk == idx
            sel = jnp.where(hit, 1.0, sel)
            score = jnp.where(hit, -3.0, score)
        sel_s[...] = sel

        n_chain = m_s.shape[0]
        half_keys = SLC_TILE // n_chain
        per_blk = _SUB // SLC_BLOCK
        m_s[...] = jnp.full(m_s.shape, NEG_INF, f32)
        l_s[...] = jnp.zeros(l_s.shape, f32)
        acc_s[...] = jnp.zeros(acc_s.shape, f32)

        def slc_tile(j, carry):
            picked = sel_s[pl.ds(pl.multiple_of(j * blocks_per_tile, blocks_per_tile), blocks_per_tile), :]

            @pl.when(jnp.max(picked) > 0.0)
            def _():
                for c in range(n_chain):
                    k0 = pl.multiple_of(j * SLC_TILE + c * half_keys, half_keys)
                    s = jnp.dot(ks_ref[g, pl.ds(k0, half_keys), :], q_pos, preferred_element_type=f32)
                    parts = []
                    for w in range(half_keys // _SUB):
                        base = k0 + w * _SUB
                        b0 = (c * half_keys + w * _SUB) // SLC_BLOCK
                        picked_k = jnp.concatenate(
                            [jnp.broadcast_to(picked[b0 + u:b0 + u + 1, :], (SLC_BLOCK, tq)) for u in range(per_blk)],
                            axis=0)
                        ok = (picked_k > 0.5) & (base + sub_t <= qpos_t)
                        bias = jnp.where(ok, 0.0, NEG_INF)
                        bias = jnp.concatenate([bias] * GQA_REP, axis=1) + slope * (base.astype(f32) - qpos_f)
                        parts.append(s[w * _SUB:(w + 1) * _SUB] + bias)
                    s = jnp.concatenate(parts, axis=0)
                    m_old = m_s[c:c + 1, :]
                    m_new = jnp.maximum(jnp.maximum(m_old, jnp.max(_colmax8(s), axis=0, keepdims=True)), _M_FLOOR)
                    p = jnp.exp(s - m_new)
                    alpha = jnp.exp(m_old - m_new)
                    l_s[c:c + 1, :] = alpha * l_s[c:c + 1, :] + jnp.sum(_colsum8(p), axis=0, keepdims=True)
                    acc_s[c] = alpha * acc_s[c] + jnp.dot(vsT_ref[vrows, pl.ds(k0, half_keys)], p.astype(bf16),
                                                          preferred_element_type=f32)
                    m_s[c:c + 1, :] = m_new

            return carry

        lax.fori_loop(0, n_tiles, slc_tile, 0)
        m_fin = jnp.max(m_s[...], axis=0, keepdims=True)
        l_fin = jnp.zeros((1, L), f32)
        o_s = jnp.zeros((HEAD_DIM, L), f32)
        for c in range(n_chain):
            w_c = jnp.exp(m_s[c:c + 1, :] - jnp.maximum(m_fin, _M_FLOOR))
            l_fin = l_fin + w_c * l_s[c:c + 1, :]
            o_s = o_s + w_c * acc_s[c]
        o_s = o_s * jnp.where(l_fin > 0.0, 1.0 / jnp.where(l_fin > 0.0, l_fin, 1.0), 0.0)

        n_win_sub = (WINDOW + tq) // _SUB
        s_parts = []
        for w in range(n_win_sub):
            base = start - WINDOW + w * _SUB
            s = jnp.dot(kw_ref[g, pl.ds(start + w * _SUB, _SUB), :], q_pos, preferred_element_type=f32)
            if w == 0:
                ok = sub_l > t_lane
            elif w == n_win_sub - 1:
                ok = sub_l <= t_lane
            else:
                ok = None
            bias = slope * (base.astype(f32) - qpos_f) + jnp.where(base >= 0, 0.0, NEG_INF)
            s = s + bias
            s_parts.append(s if ok is None else jnp.where(ok, s, NEG_INF))
        m8 = _colmax8(s_parts[0])
        for s in s_parts[1:]:
            m8 = jnp.maximum(m8, _colmax8(s))
        m_w = jnp.maximum(jnp.max(m8, axis=0, keepdims=True), _M_FLOOR)
        l8 = jnp.zeros((8, L), f32)
        o_w = jnp.zeros((HEAD_DIM, L), f32)
        for w, s in enumerate(s_parts):
            e = jnp.exp(s - m_w)
            l8 = l8 + _colsum8(e)
            o_w = o_w + jnp.dot(vwT_ref[vrows, pl.ds(start + w * _SUB, _SUB)], e.astype(bf16),
                                preferred_element_type=f32)
        l_w = jnp.sum(l8, axis=0, keepdims=True)
        o_w = o_w * jnp.where(l_w > 0.0, 1.0 / jnp.where(l_w > 0.0, l_w, 1.0), 0.0)

        for r in range(GQA_REP):
            head = GQA_REP * g + r
            cols = slice(r * tq, (r + 1) * tq)
            o_ref[head * HEAD_DIM:(head + 1) * HEAD_DIM, :] = (
                gT_ref[3 * head:3 * head + 1, :] * o_c[:, cols]
                + gT_ref[3 * head + 1:3 * head + 2, :] * o_s[:, cols]
                + gT_ref[3 * head + 2:3 * head + 3, :] * o_w[:, cols])


def _nsa_prompt_t(q, small, kv_cmp, kvs, kvw):
    t = q.shape[0]
    n_cmp = kv_cmp.shape[0]
    n_slc = t // SLC_BLOCK
    kcol = lambda g: slice(g * HEAD_DIM, (g + 1) * HEAD_DIM)
    vT = lambda a: a[:, N_KV_HEADS * HEAD_DIM:].T.astype(bf16)
    qT = q.T.astype(bf16)
    gT = small[:, :32].T
    kc = jnp.stack([_with_pos_feature(kv_cmp[:, kcol(g)]) for g in range(N_KV_HEADS)])
    ks = jnp.stack([_with_pos_feature(kvs[:, kcol(g)]) for g in range(N_KV_HEADS)])
    kvw_pad = jnp.pad(kvw, ((WINDOW, 0), (0, 0)))
    kw = jnp.stack([_with_pos_feature(kvw_pad[:, kcol(g)]) for g in range(N_KV_HEADS)])
    operands = (qT, gT, kc, vT(kv_cmp), ks, vT(kvs), kw, vT(kvw_pad))
    L = GQA_REP * Q_BLOCK
    const = lambda a: pl.BlockSpec(a.shape, lambda i: (0,) * a.ndim, pipeline_mode=pl.Buffered(1))
    return pl.pallas_call(
        _nsa_prompt_t_kernel,
        grid=(t // Q_BLOCK,),
        in_specs=[pl.BlockSpec((D_ATT, Q_BLOCK), lambda i: (0, i)),
                  pl.BlockSpec((32, Q_BLOCK), lambda i: (0, i))] + [const(a) for a in operands[2:]],
        out_specs=pl.BlockSpec((D_ATT, Q_BLOCK), lambda i: (0, i)),
        out_shape=jax.ShapeDtypeStruct((D_ATT, t), f32),
        scratch_shapes=[pltpu.VMEM((n_cmp, L), f32), pltpu.VMEM((n_cmp + _PSUM_PAD, Q_BLOCK), f32),
                        pltpu.VMEM((n_slc, Q_BLOCK), f32), pltpu.VMEM((_SLC_CHAINS, L), f32),
                        pltpu.VMEM((_SLC_CHAINS, L), f32), pltpu.VMEM((_SLC_CHAINS, HEAD_DIM, L), f32)],
        compiler_params=_cparams("parallel"),
        name="nsa_prompt",
    )(*operands)


_QROWS = 16
_IDS_LANES = 128


def _group_queries_1(q, g):
    rows = [q[:, (GQA_REP * g + r) * HEAD_DIM:(GQA_REP * g + r + 1) * HEAD_DIM] for r in range(GQA_REP)]
    rows.append(jnp.zeros((_QROWS - GQA_REP, HEAD_DIM), f32))
    return jnp.concatenate(rows, axis=0) * (HEAD_DIM ** -0.5)


def _group_slopes_1(g):
    r = jnp.minimum(lax.broadcasted_iota(jnp.int32, (_QROWS, 1), 0), GQA_REP - 1)
    return jnp.exp2(-8.0 * (GQA_REP * g + 1 + r).astype(f32) / N_ATT_HEADS)


def _nsa_sample_cmp_kernel(q_ref, cmp_ref, pool_ref, oc_ref, ids_ref, *, past):
    b = pl.program_id(0)
    n_cmp = cmp_ref.shape[1]
    n_old = pool_ref.shape[1]
    q = q_ref[pl.ds(b, 1), :]
    cmpb = cmp_ref[0].astype(bf16)
    d_c = past - (lax.broadcasted_iota(jnp.int32, (1, n_cmp), 1) * CMP_STRIDE + (CMP_BLOCK - 1))
    lane = lax.broadcasted_iota(jnp.int32, (1, _IDS_LANES), 1)
    cur = jnp.full((1, 1), past // SLC_BLOCK, jnp.int32)
    oc_parts = []
    for g in range(N_KV_HEADS):
        q16 = _group_queries_1(q, g).astype(bf16)
        s = _dot_nt(q16, cmpb[:, g * HEAD_DIM:(g + 1) * HEAD_DIM]) - _group_slopes_1(g) * d_c.astype(f32)
        p = _masked_softmax(s, jnp.broadcast_to(d_c >= 0, s.shape))
        o_c = jnp.dot(p.astype(bf16), cmpb[:, (N_KV_HEADS + g) * HEAD_DIM:(N_KV_HEADS + g + 1) * HEAD_DIM],
                      preferred_element_type=f32)
        oc_parts.extend(o_c[r:r + 1] for r in range(GQA_REP))
        p_sum = jnp.sum(p[0:GQA_REP], axis=0, keepdims=True)
        p_slc = jnp.dot(jnp.broadcast_to(p_sum, (8, n_cmp)), pool_ref[...], preferred_element_type=f32,
                        precision=HIGHEST)[0:1]
        p_ext = jnp.concatenate([p_slc, jnp.zeros((1, 128), f32)], axis=1)
        _, picks = _select_blocks(p_ext, cur, past // SLC_BLOCK + 1)
        row = jnp.zeros((1, _IDS_LANES), jnp.int32)
        for k, pick in enumerate(picks):
            row = jnp.where(lane == k, pick, row)
        ids_ref[0, g:g + 1, :] = row
    oc_ref[pl.ds(b, 1), :] = jnp.concatenate(oc_parts, axis=1)


def _nsa_sample_cmp(q, cmp_s, pool, past):
    nb = q.shape[0]
    return pl.pallas_call(
        functools.partial(_nsa_sample_cmp_kernel, past=past),
        grid=(nb,),
        in_specs=[pl.BlockSpec(q.shape, lambda b: (0, 0)),
                  pl.BlockSpec((1,) + cmp_s.shape[1:], lambda b: (b, 0, 0)),
                  pl.BlockSpec(pool.shape, lambda b: (0, 0))],
        out_specs=[pl.BlockSpec((nb, D_ATT), lambda b: (0, 0)),
                   pl.BlockSpec((1, N_KV_HEADS, _IDS_LANES), lambda b: (b, 0, 0))],
        out_shape=[jax.ShapeDtypeStruct((nb, D_ATT), f32),
                   jax.ShapeDtypeStruct((nb, N_KV_HEADS, _IDS_LANES), jnp.int32)],
        compiler_params=_cparams("arbitrary"),
        name="nsa_sample_cmp",
    )(q, cmp_s, pool)


def _softmax_with_new(s, mask, s_new, new_ok):
    s = jnp.where(mask, s, NEG_INF)
    s_new = jnp.where(new_ok, s_new, NEG_INF)
    m = jnp.maximum(jnp.max(s, axis=1, keepdims=True), s_new)
    p = jnp.where(mask, jnp.exp(s - m), 0.0)
    p_new = jnp.where(new_ok, jnp.exp(s_new - m), 0.0)
    l = jnp.sum(p, axis=1, keepdims=True) + p_new
    inv = jnp.where(l > 0.0, 1.0 / jnp.where(l > 0.0, l, 1.0), 0.0)
    return p * inv, p_new * inv


def _nsa_sample_sel_kernel(ids_ref, pt_ref, pool_ref, q_ref, sm_ref, kvs_ref, kvw_ref, win_ref, oc_ref, o_ref,
                           buf, sem, *, past):
    b = pl.program_id(0)
    n_pages = past // PAGE_ROWS
    last_blk = past // SLC_BLOCK
    blocks_per_page = PAGE_ROWS // SLC_BLOCK

    def page_copy(g, k, c):
        blk = jnp.minimum(ids_ref[(b * N_KV_HEADS + g) * N_SELECT + k], last_blk - 1)
        page = pt_ref[b * n_pages + blk // blocks_per_page]
        return pltpu.make_async_copy(pool_ref.at[page, c, g], buf.at[g, c, k], sem)

    copies = [(g, k, c) for g in range(N_KV_HEADS) for k in range(N_SELECT) for c in range(2)]
    for g, k, c in copies:
        page_copy(g, k, c).start()
    for g, k, c in copies:
        page_copy(g, k, c).wait()

    q = q_ref[pl.ds(b, 1), :]
    gates = sm_ref[pl.ds(b, 1), :]
    new_s = kvs_ref[pl.ds(b, 1), :]
    new_w = kvw_ref[pl.ds(b, 1), :]
    n_sel = N_SELECT * PAGE_ROWS
    lane = lax.broadcasted_iota(jnp.int32, (1, n_sel), 1)
    w_buf = win_ref.shape[1]
    d_w = w_buf - lax.broadcasted_iota(jnp.int32, (1, w_buf), 1)
    mask_w = jnp.broadcast_to((d_w < WINDOW) & (past - d_w >= 0), (_QROWS, w_buf))
    always = jnp.full((_QROWS, 1), True)
    o_c = oc_ref[pl.ds(b, 1), :]
    out_parts = []
    for g in range(N_KV_HEADS):
        kcol = slice(g * HEAD_DIM, (g + 1) * HEAD_DIM)
        vcol = slice((N_KV_HEADS + g) * HEAD_DIM, (N_KV_HEADS + g + 1) * HEAD_DIM)
        q16f = _group_queries_1(q, g)
        q16 = q16f.astype(bf16)
        slope = _group_slopes_1(g)

        blk_vec = jnp.zeros((1, n_sel), jnp.int32)
        for k in range(N_SELECT):
            blk_vec = jnp.where(lane // PAGE_ROWS == k, ids_ref[(b * N_KV_HEADS + g) * N_SELECT + k], blk_vec)
        kpos = (blk_vec // blocks_per_page) * PAGE_ROWS + lane % PAGE_ROWS
        d_s = past - kpos
        mask_s = jnp.broadcast_to((d_s >= 0) & (blk_vec < last_blk) & (kpos // SLC_BLOCK == blk_vec),
                                  (_QROWS, n_sel))
        has_new = jnp.max(jnp.where(blk_vec == last_blk, 1, 0), axis=1, keepdims=True) > 0
        kt = jnp.concatenate([buf[g, 0, k] for k in range(N_SELECT)], axis=1).astype(bf16)
        vt = jnp.concatenate([buf[g, 1, k] for k in range(N_SELECT)], axis=1).astype(bf16)
        s = jnp.dot(q16, kt, preferred_element_type=f32) - slope * d_s.astype(f32)
        s_new = jnp.sum(q16f * new_s[:, kcol], axis=1, keepdims=True)
        p, p_new = _softmax_with_new(s, mask_s, s_new, jnp.broadcast_to(has_new, (_QROWS, 1)))
        o_s = _dot_nt(p.astype(bf16), vt) + p_new * new_s[:, vcol]

        wb = win_ref[0].astype(bf16)
        s = _dot_nt(q16, wb[:, kcol]) - slope * d_w.astype(f32)
        s_new = jnp.sum(q16f * new_w[:, kcol], axis=1, keepdims=True)
        p, p_new = _softmax_with_new(s, mask_w, s_new, always)
        o_w = jnp.dot(p.astype(bf16), wb[:, vcol], preferred_element_type=f32) + p_new * new_w[:, vcol]

        for r in range(GQA_REP):
            head = GQA_REP * g + r
            hs = slice(head * HEAD_DIM, (head + 1) * HEAD_DIM)
            out_parts.append(gates[:, 3 * head:3 * head + 1] * o_c[:, hs]
                             + gates[:, 3 * head + 1:3 * head + 2] * o_s[r:r + 1]
                             + gates[:, 3 * head + 2:3 * head + 3] * o_w[r:r + 1])
    o_ref[pl.ds(b, 1), :] = jnp.concatenate(out_parts, axis=1)


def _nsa_sample_sel(ids, page_table, pool_slc, q, small, kvs_new, kvw_new, win_buf, o_c, past):
    nb = q.shape[0]
    full = lambda a: pl.BlockSpec(a.shape, lambda b, ids, pt: (0,) * a.ndim)
    return pl.pallas_call(
        functools.partial(_nsa_sample_sel_kernel, past=past),
        grid_spec=pltpu.PrefetchScalarGridSpec(
            num_scalar_prefetch=2,
            grid=(nb,),
            in_specs=[pl.BlockSpec(memory_space=pl.ANY), full(q), full(small), full(kvs_new), full(kvw_new),
                      pl.BlockSpec((1,) + win_buf.shape[1:], lambda b, ids, pt: (b, 0, 0)), full(o_c)],
            out_specs=pl.BlockSpec((nb, D_ATT), lambda b, ids, pt: (0, 0)),
            scratch_shapes=[pltpu.VMEM((N_KV_HEADS, 2, N_SELECT, HEAD_DIM, PAGE_ROWS), f32),
                            pltpu.SemaphoreType.DMA(())]),
        out_shape=jax.ShapeDtypeStruct((nb, D_ATT), f32),
        compiler_params=_cparams("arbitrary"),
        name="nsa_sample_sel",
    )(ids, page_table.reshape(-1), pool_slc, q, small, kvs_new, kvw_new, win_buf, o_c)


def _mixout_kernel(att_ref, mh_ref, mo_ref, x_ref, ga_ref, gm_ref, wo_ref, nf_ref, wq_ref, h_ref, xn_ref,
                   qp_ref):
    parts = []
    for h in range(N_ATT_HEADS):
        sl = slice(h * HEAD_DIM, (h + 1) * HEAD_DIM)
        parts.append(_rms(att_ref[:, sl], ga_ref[:, sl]))
    for h in range(N_MLSTM_HEADS):
        sl = slice(h * MLSTM_HEAD_DIM, (h + 1) * MLSTM_HEAD_DIM)
        parts.append(mo_ref[:, sl] * _rms(mh_ref[:, sl], gm_ref[:, sl]))
    cat = jnp.concatenate(parts, axis=1).astype(bf16)
    h1 = x_ref[...] + jnp.dot(cat, wo_ref[...], preferred_element_type=f32)
    h_ref[...] = h1
    xn = _rms(h1, nf_ref[...]).astype(bf16)
    xn_ref[...] = xn
    qp_ref[...] = jnp.dot(xn, wq_ref[...], preferred_element_type=f32)


def _mix_output(att, mh, mo, x, g_att, g_ml, w_out_bf, norm_ffn, w_q_bf, tm):
    n = x.shape[0]
    tok = lambda w: pl.BlockSpec((tm, w), lambda i: (i, 0))
    full = lambda a: pl.BlockSpec(a.shape, lambda i: (0,) * a.ndim)
    return pl.pallas_call(
        _mixout_kernel,
        grid=(n // tm,),
        in_specs=[tok(D_ATT), tok(D_MLSTM), tok(D_MLSTM), tok(D_MODEL), full(g_att), full(g_ml),
                  full(w_out_bf), full(norm_ffn), full(w_q_bf)],
        out_specs=[tok(D_MODEL), tok(D_MODEL), tok(D_MODEL)],
        out_shape=[jax.ShapeDtypeStruct((n, D_MODEL), f32), jax.ShapeDtypeStruct((n, D_MODEL), bf16),
                   jax.ShapeDtypeStruct((n, D_MODEL), f32)],
        compiler_params=_cparams("parallel"),
        name="mix_output",
    )(att, mh, mo, x, g_att, g_ml, w_out_bf, norm_ffn, w_q_bf)


def _topk_rows(s, k):
    n = s.shape[0]
    rows = lax.broadcasted_iota(jnp.int32, s.shape, 0)
    vals, idxs = [], []
    for _ in range(k):
        mx = jnp.max(s, axis=0, keepdims=True)
        idx = jnp.min(jnp.where(s == mx, rows, n), axis=0, keepdims=True)
        vals.append(mx)
        idxs.append(idx)
        s = jnp.where(rows == idx, NEG_INF, s)
    return jnp.concatenate(vals, axis=0), jnp.concatenate(idxs, axis=0)


def _peer_topk_kernel(qp_ref, sub_ref, ei_ref, ej_ref, g_ref):
    K = PEER_TOPK
    half = PEER_D_KEY // 2
    ei, ej, gg = [], [], []
    for h in range(PEER_HEADS):
        sv, si = [], []
        for c in range(2):
            qhc = qp_ref[:, (2 * h + c) * half:(2 * h + c + 1) * half]
            s = _dot_nt(sub_ref[h, c], qhc, precision=HIGHEST)
            v, i = _topk_rows(s, K)
            sv.append(v)
            si.append(i)
        n_t = sv[0].shape[1]
        counts = [K // (a + 1) for a in range(K)]
        n_cand = -(-sum(counts) // 8) * 8
        pad = n_cand - sum(counts)
        cand = jnp.concatenate([sv[0][a:a + 1] + sv[1][0:counts[a]] for a in range(K)]
                               + [jnp.full((pad, n_t), NEG_INF, f32)], axis=0)
        pos_i = jnp.concatenate([jnp.broadcast_to(si[0][a:a + 1], (counts[a], n_t)) for a in range(K)]
                                + [jnp.zeros((pad, n_t), jnp.int32)], axis=0)
        pos_j = jnp.concatenate([si[1][0:counts[a]] for a in range(K)] + [jnp.zeros((pad, n_t), jnp.int32)],
                                axis=0)
        rows = lax.broadcasted_iota(jnp.int32, cand.shape, 0)
        best, bi, bj = [], [], []
        for _ in range(K):
            mx = jnp.max(cand, axis=0, keepdims=True)
            pos = jnp.min(jnp.where(cand == mx, rows, n_cand), axis=0, keepdims=True)
            hit = rows == pos
            best.append(mx)
            bi.append(jnp.max(jnp.where(hit, pos_i, -1), axis=0, keepdims=True))
            bj.append(jnp.max(jnp.where(hit, pos_j, -1), axis=0, keepdims=True))
            cand = jnp.where(hit, NEG_INF, cand)
        best = jnp.concatenate(best, axis=0)
        e = jnp.exp(best - best[0:1])
        gg.append(e / jnp.sum(e, axis=0, keepdims=True))
        ei.append(jnp.concatenate(bi, axis=0))
        ej.append(jnp.concatenate(bj, axis=0))
    ei_ref[...] = jnp.concatenate(ei, axis=0).astype(f32).T
    ej_ref[...] = jnp.concatenate(ej, axis=0).astype(f32).T
    g_ref[...] = jnp.concatenate(gg, axis=0).T


def _peer_topk(qp, sub_keys, tm):
    n = qp.shape[0]
    hk = PEER_HEADS * PEER_TOPK
    return pl.pallas_call(
        _peer_topk_kernel,
        grid=(n // tm,),
        in_specs=[pl.BlockSpec((tm, D_MODEL), lambda i: (i, 0)),
                  pl.BlockSpec(sub_keys.shape, lambda i: (0, 0, 0, 0))],
        out_specs=[pl.BlockSpec((tm, hk), lambda i: (i, 0))] * 3,
        out_shape=[jax.ShapeDtypeStruct((n, hk), f32)] * 3,
        compiler_params=_cparams("parallel"),
        name="peer_topk",
    )(qp, sub_keys)


_GATE_UNROLL = 16


def _peer_dense_kernel(ei_ref, ej_ref, gg_ref, x_ref, u_ref, v_ref, h_ref, nf_ref, o_ref, acc_ref, g_s):
    e = pl.program_id(1)
    tm = x_ref.shape[0]
    et = u_ref.shape[0]
    nk = PEER_N_KEYS

    @pl.when(e == 0)
    def _():
        acc_ref[...] = jnp.zeros_like(acc_ref)
        sub = lax.broadcasted_iota(jnp.int32, (nk, ei_ref.shape[1]), 0).astype(f32)

        def body(t, carry):
            a = jnp.where(sub == ei_ref[pl.ds(t, 1), :], 1.0, 0.0).astype(bf16)
            b = jnp.where(sub == ej_ref[pl.ds(t, 1), :], gg_ref[pl.ds(t, 1), :], 0.0).astype(bf16)
            g_s[pl.ds(pl.multiple_of(t * nk, nk), nk), :] = _dot_nt(a, b)
            return carry

        lax.fori_loop(0, tm, body, 0, unroll=_GATE_UNROLL)

    rows = et // nk
    g = jnp.concatenate([g_s[pl.ds(e * rows + r, tm, stride=nk), :] for r in range(rows)], axis=1)
    act = jax.nn.gelu(_dot_nt(x_ref[...], u_ref[...]))
    acc_ref[...] += jnp.dot((g * act).astype(bf16), v_ref[...], preferred_element_type=f32)

    @pl.when(e == pl.num_programs(1) - 1)
    def _():
        o_ref[...] = _rms(h_ref[...] + acc_ref[...], nf_ref[...])


def _peer_dense(ei, ej, gg, xn_bf, u_bf, v_bf, h1, norm_final, tm, et):
    n = xn_bf.shape[0]
    n_exp = u_bf.shape[0]
    hk = ei.shape[1]
    tok = lambda w: pl.BlockSpec((tm, w), lambda i, e: (i, 0))
    return pl.pallas_call(
        _peer_dense_kernel,
        grid=(n // tm, n_exp // et),
        in_specs=[tok(hk), tok(hk), tok(hk), tok(D_MODEL),
                  pl.BlockSpec((et, D_MODEL), lambda i, e: (e, 0)),
                  pl.BlockSpec((et, D_MODEL), lambda i, e: (e, 0)),
                  tok(D_MODEL),
                  pl.BlockSpec((1, D_MODEL), lambda i, e: (0, 0))],
        out_specs=tok(D_MODEL),
        out_shape=jax.ShapeDtypeStruct((n, D_MODEL), f32),
        scratch_shapes=[pltpu.VMEM((tm, D_MODEL), f32), pltpu.VMEM((tm * PEER_N_KEYS, PEER_N_KEYS), f32)],
        compiler_params=_cparams("parallel", "arbitrary"),
        name="peer_dense",
    )(ei, ej, gg, xn_bf, u_bf, v_bf, h1, norm_final)


def _channel_mix_and_norm(h1, xn_bf, qp, sub_keys, u_bf, v_bf, norm_final, tm_topk, tm, et):
    ei, ej, gg = _peer_topk(qp, sub_keys, tm_topk)
    return _peer_dense(ei, ej, gg, xn_bf, u_bf, v_bf, h1, norm_final, tm, et)


PROMPT_TM = 512
PEER_TM = 256
PEER_ET = 1024
MLSTM_CHUNK = 256
SAMPLE_PAD = 128


def kernel(x_prompt, x_sample, cache_cmp_kv, cache_slc_kv, cache_win_kv, state_mlstm_C, state_mlstm_n,
           state_mlstm_m, page_table, norm_mix, w_in, b_igate, b_fgate, pe_cmp, w_cmp1, w_cmp2, norm_att_out,
           norm_mlstm_out, w_out, norm_ffn, peer_wq, peer_subkeys, peer_u, peer_v, norm_final):
    assert x_prompt.shape[0] == 1 and x_sample.shape[1] == 1 and w_in.shape[0] == 1
    _, t, d = x_prompt.shape
    nb = x_sample.shape[0]
    n_pool = cache_cmp_kv.shape[1]
    past = page_table.shape[1] * PAGE_ROWS
    w_buf = cache_win_kv.shape[2]
    row = (2, N_KV_HEADS, HEAD_DIM)
    NH, DH = N_MLSTM_HEADS, MLSTM_HEAD_DIM
    g0 = N_GATE_COLS

    w_re = _relayout_w_in(w_in[0])
    bias = _bias_row(b_igate[0], b_fgate[0])
    cw = _compress_weights(pe_cmp[0], w_cmp1[0], w_cmp2[0])
    nm = norm_mix[0][None]
    ga, gm, nf, nfin = norm_att_out[0][None], norm_mlstm_out[0][None], norm_ffn[0][None], norm_final[None]
    w_out_bf, w_q_bf = w_out[0].astype(bf16), peer_wq[0].astype(bf16)
    u_bf, v_bf = peer_u[0].astype(bf16), peer_v[0].astype(bf16)
    sub_keys = peer_subkeys[0]

    xp = x_prompt.reshape(t, d)
    q, kvc, kvs, kvw, mq, mk, mv, mo, sm = _project(xp, nm, w_re, bias, PROMPT_TM)
    cmp_p = _compress_prompt(kvc[None], cw)[0]
    att = _nsa_prompt_t(q, sm, cmp_p, kvs, kvw).T
    gates_c = sm[:, g0:g0 + 2 * NH]
    mh, c_p, n_p, m_p = _mlstm_prompt(mq, mk, mv, gates_c, gates_c.T, MLSTM_CHUNK)
    h1, xn_bf, qp = _mix_output(att, mh, mo, xp, ga, gm, w_out_bf, nf, w_q_bf, 256)
    y_p = _channel_mix_and_norm(h1, xn_bf, qp, sub_keys, u_bf, v_bf, nfin, 256, PEER_TM, PEER_ET)
    w_keep = min(WINDOW, t)
    outs_p = (kvc.reshape((1, 1, t) + row), kvs.reshape((1, 1, t) + row),
              kvw[t - w_keep:].reshape((1, 1, w_keep) + row),
              c_p[None, None], n_p[:NH][None, None], m_p[:NH, 0][None, None])

    xs = x_sample.reshape(nb, d)
    q, kvc_s, kvs_s, kvw_s, mq, mk, mv, mo, sm = _project(xs, nm, w_re, bias, nb)
    cmp_s = _compress_paged(page_table, cache_cmp_kv[0].reshape(n_pool, PAGE_ROWS, KV_COLS), cw)
    pool_s = _pool_matrix(past // CMP_STRIDE, past // SLC_BLOCK)
    o_c, ids = _nsa_sample_cmp(q, cmp_s, pool_s, past)
    win_buf = cache_win_kv[0].reshape(nb, w_buf, KV_COLS)
    att = _nsa_sample_sel(ids[:, :, :N_SELECT].reshape(-1), page_table,
                          jnp.transpose(cache_slc_kv[0], (0, 2, 3, 4, 1)), q, sm, kvs_s, kvw_s, win_buf, o_c, past)
    mh, c_s, n_s, m_s = _mlstm_sample(mq, mk, mv, sm[:, g0:g0 + 2 * NH], state_mlstm_C[0], state_mlstm_n[0],
                                      state_mlstm_m[0])
    h1, xn_bf, qp = _mix_output(att, mh, mo, xs, ga, gm, w_out_bf, nf, w_q_bf, nb)
    padr = lambda a: jnp.pad(a, ((0, SAMPLE_PAD - nb), (0, 0)))
    y_s = _channel_mix_and_norm(padr(h1), padr(xn_bf), padr(qp), sub_keys, u_bf, v_bf, nfin, SAMPLE_PAD,
                                SAMPLE_PAD, PEER_ET)[:nb]
    win_all = jnp.concatenate([win_buf, kvw_s[:, None, :]], axis=1)
    w_keep_s = min(WINDOW, w_buf + 1)
    outs_s = (kvc_s.reshape((1, nb, 1) + row), kvs_s.reshape((1, nb, 1) + row),
              win_all[:, w_buf + 1 - w_keep_s:].reshape((1, nb, w_keep_s) + row),
              c_s[None], n_s[None], m_s[None])

    return (y_p.reshape(1, t, d), y_s.reshape(nb, 1, d),
            outs_p[0], outs_s[0], outs_p[1], outs_s[1], outs_p[2], outs_s[2],
            outs_p[3], outs_s[3], outs_p[4], outs_s[4], outs_p[5], outs_s[5])
```

```python
import functools

import jax
import jax.numpy as jnp
import numpy as np
from jax import lax
from jax.experimental import pallas as pl
from jax.experimental.pallas import tpu as pltpu

f32 = jnp.float32
bf16 = jnp.bfloat16
HIGHEST = lax.Precision.HIGHEST

D_MODEL = 1024
HEAD_DIM = 64
N_ATT_HEADS = 8
N_KV_HEADS = 2
GQA_REP = 4
D_ATT = 512
KV_COLS = 256
CMP_BLOCK = 32
CMP_STRIDE = 16
CMP_HIDDEN = 128
SLC_BLOCK = 64
N_SELECT = 16
WINDOW = 512
Q_BLOCK = 128
FORCE_BONUS = 1000.0
N_MLSTM_HEADS = 4
MLSTM_HEAD_DIM = 128
D_MLSTM = 512
N_GATE_COLS = 3 * N_ATT_HEADS
SPLIT_SIZES = (D_ATT, KV_COLS, KV_COLS, KV_COLS, N_GATE_COLS, D_MLSTM, D_MLSTM, D_MLSTM, D_MLSTM,
               N_MLSTM_HEADS, N_MLSTM_HEADS)
SMALL_COLS = 128
PEER_N_KEYS = 128
PEER_HEADS = 8
PEER_TOPK = 16
PEER_D_KEY = 128
NORM_EPS = 1e-6
NEG_INF = -1e30
VMEM_LIMIT = 56 * 1024 * 1024


def _cparams(*sem):
    return pltpu.CompilerParams(dimension_semantics=sem, vmem_limit_bytes=VMEM_LIMIT)


def _rms(x, w):
    return x * lax.rsqrt(jnp.mean(x * x, axis=-1, keepdims=True) + NORM_EPS) * w


_P_Q, _P_KVC, _P_KVS, _P_KVW, _P_MQ, _P_MK, _P_MV, _P_MO, _P_SM, _P_END = (
    0, 512, 768, 1024, 1280, 1792, 2304, 2816, 3328, 3456)


def _relayout_w_in(w_in):
    pts = np.cumsum(SPLIT_SIZES)[:-1].tolist()
    q, kc, ks, kw, ga, mq, mk, mv, mo, ig, fg = jnp.split(w_in, pts, axis=-1)
    pad = jnp.zeros((w_in.shape[0], SMALL_COLS - N_GATE_COLS - 2 * N_MLSTM_HEADS), w_in.dtype)
    return jnp.concatenate([q, kc, ks, kw, mq, mk, mv, mo, ga, ig, fg, pad], axis=-1).astype(bf16)


def _bias_row(b_i, b_f):
    pad = jnp.zeros((SMALL_COLS - N_GATE_COLS - 2 * N_MLSTM_HEADS,), f32)
    return jnp.concatenate([jnp.zeros((N_GATE_COLS,), f32), b_i, b_f, pad])[None, :]


def _proj_kernel(x_ref, nw_ref, w_ref, b_ref, q_ref, kvc_ref, kvs_ref, kvw_ref, mq_ref, mk_ref, mv_ref,
                 mo_ref, sm_ref):
    xb = _rms(x_ref[...], nw_ref[...]).astype(bf16)

    def mm(lo, hi):
        return jnp.dot(xb, w_ref[:, lo:hi], preferred_element_type=f32)

    q_ref[...] = mm(_P_Q, _P_KVC)
    kvc_ref[...] = mm(_P_KVC, _P_KVS)
    kvs_ref[...] = mm(_P_KVS, _P_KVW)
    kvw_ref[...] = mm(_P_KVW, _P_MQ)
    mq_ref[...] = mm(_P_MQ, _P_MK)
    mk_ref[...] = mm(_P_MK, _P_MV)
    mv_ref[...] = mm(_P_MV, _P_MO)
    mo_ref[...] = jax.nn.sigmoid(mm(_P_MO, _P_SM))
    s = mm(_P_SM, _P_END) + b_ref[...]
    col = lax.broadcasted_iota(jnp.int32, s.shape, 1)
    sm_ref[...] = jnp.where(col < N_GATE_COLS, jax.nn.sigmoid(s), s)


def _project(x, norm_w, w_re, bias_row, tm):
    n = x.shape[0]
    widths = (D_ATT, KV_COLS, KV_COLS, KV_COLS, D_MLSTM, D_MLSTM, D_MLSTM, D_MLSTM, SMALL_COLS)
    return pl.pallas_call(
        _proj_kernel,
        grid=(n // tm,),
        in_specs=[pl.BlockSpec((tm, D_MODEL), lambda i: (i, 0)),
                  pl.BlockSpec((1, D_MODEL), lambda i: (0, 0)),
                  pl.BlockSpec((D_MODEL, _P_END), lambda i: (0, 0)),
                  pl.BlockSpec((1, SMALL_COLS), lambda i: (0, 0))],
        out_specs=[pl.BlockSpec((tm, w), lambda i: (i, 0)) for w in widths],
        out_shape=[jax.ShapeDtypeStruct((n, w), f32) for w in widths],
        compiler_params=_cparams("parallel"),
        name="proj",
    )(x, norm_w, w_re, bias_row)


_HID_COLS = 2 * N_KV_HEADS * CMP_HIDDEN
PAGE_ROWS = 128


def _compress_weights(pe, w1, w2):
    eye = jnp.eye(2, dtype=f32)

    def big1(w):
        return jnp.einsum('cldh,gf->clfdgh', w, eye).reshape(2, CMP_STRIDE * KV_COLS // 2, _HID_COLS // 2).astype(bf16)

    def pe_rows(p):
        row = jnp.broadcast_to(p.transpose(1, 0, 2)[:, :, None, :], (2, CMP_STRIDE, N_KV_HEADS, HEAD_DIM))
        row = row.reshape(2, 1, -1)
        return jnp.concatenate([row, jnp.zeros((2, 7, row.shape[-1]), f32)], axis=1)

    w2_big = jnp.einsum('chd,ce,gf->efhcgd', w2, eye, eye).reshape(_HID_COLS, KV_COLS).astype(bf16)
    pe_all = jnp.stack([pe_rows(pe[:CMP_STRIDE]), pe_rows(pe[CMP_STRIDE:])], axis=0).astype(bf16)
    return big1(w1[:, :CMP_STRIDE]), big1(w1[:, CMP_STRIDE:]), pe_all, w2_big


_CMP_ROWS = 256


def _compress_math(x_lo, x_hi, w1a_ref, w1b_ref, pe_ref, w2_ref):
    n_chunks = x_lo.shape[0] // CMP_STRIDE
    step = min(_CMP_ROWS, n_chunks)
    first, second = [], []
    for c0 in range(0, n_chunks, step):
        f_kv, s_kv = [], []
        for kv, x_ref in enumerate((x_lo, x_hi)):
            x = jnp.concatenate([x_ref[pl.ds(c0 * CMP_STRIDE + l, step, stride=CMP_STRIDE), :]
                                 for l in range(CMP_STRIDE)], axis=1).astype(bf16)
            f_kv.append(jnp.dot(x, w1a_ref[kv], preferred_element_type=f32))
            s_kv.append(jnp.dot(x, w1b_ref[kv], preferred_element_type=f32))
        first.append(jnp.concatenate(f_kv, axis=1))
        second.append(jnp.concatenate(s_kv, axis=1))
    first = jnp.concatenate(first, axis=0)
    second = jnp.concatenate(second, axis=0)
    pe_term = jnp.concatenate(
        [(jnp.dot(pe_ref[0, kv], w1a_ref[kv], preferred_element_type=f32)
          + jnp.dot(pe_ref[1, kv], w1b_ref[kv], preferred_element_type=f32))[0:1] for kv in range(2)], axis=1)
    h = jax.nn.gelu(first + pltpu.roll(second, n_chunks - 1, 0) + pe_term)
    return jnp.dot(h.astype(bf16), w2_ref[...], preferred_element_type=f32)


def _compress_kernel(x_lo, x_hi, w1a_ref, w1b_ref, pe_ref, w2_ref, o_ref):
    o_ref[0] = _compress_math(x_lo.at[0], x_hi.at[0], w1a_ref, w1b_ref, pe_ref, w2_ref)


def _compress_paged_kernel(pt_ref, pool_ref, w1a_ref, w1b_ref, pe_ref, w2_ref, o_ref, raw_s, x_s, sem):
    b = pl.program_id(0)
    n_pages = raw_s.shape[0]
    lanes = N_KV_HEADS * HEAD_DIM

    def page_copy(seq, p):
        return pltpu.make_async_copy(pool_ref.at[pt_ref[seq * n_pages + p]], raw_s.at[p], sem)

    def start_pages(seq):
        def start(p, c):
            page_copy(seq, p).start()
            return c
        lax.fori_loop(0, n_pages, start, 0)

    def wait(p, c):
        page_copy(b, p).wait()
        return c

    def to_rows(p, c):
        r0 = pl.multiple_of(p * PAGE_ROWS, PAGE_ROWS)
        for kv in range(2):
            x_s[kv, pl.ds(r0, PAGE_ROWS), :] = raw_s[p, kv].reshape(lanes, PAGE_ROWS).T
        return c

    @pl.when(b == 0)
    def _():
        start_pages(b)

    lax.fori_loop(0, n_pages, wait, 0)
    lax.fori_loop(0, n_pages, to_rows, 0, unroll=4)

    @pl.when(b + 1 < pl.num_programs(0))
    def _():
        start_pages(b + 1)

    o_ref[0] = _compress_math(x_s.at[0], x_s.at[1], w1a_ref, w1b_ref, pe_ref, w2_ref)


def _compress_paged(page_table, pool_t, cw):
    nb, n_pages = page_table.shape
    n_chunks = n_pages * PAGE_ROWS // CMP_STRIDE
    full = lambda a: pl.BlockSpec(a.shape, lambda b, pt: (0,) * a.ndim, pipeline_mode=pl.Buffered(1))
    return pl.pallas_call(
        _compress_paged_kernel,
        grid_spec=pltpu.PrefetchScalarGridSpec(
            num_scalar_prefetch=1,
            grid=(nb,),
            in_specs=[pl.BlockSpec(memory_space=pl.ANY)] + [full(a) for a in cw],
            out_specs=pl.BlockSpec((1, n_chunks, KV_COLS), lambda b, pt: (b, 0, 0)),
            scratch_shapes=[pltpu.VMEM((n_pages,) + pool_t.shape[1:], f32),
                            pltpu.VMEM((2, n_pages * PAGE_ROWS, KV_COLS // 2), f32),
                            pltpu.SemaphoreType.DMA(())]),
        out_shape=jax.ShapeDtypeStruct((nb, n_chunks, KV_COLS), f32),
        compiler_params=_cparams("arbitrary"),
        name="compress_paged",
    )(page_table.reshape(-1), pool_t, *cw)


def _compress_prompt(kv_rows, cw):
    nb, n_rows, _ = kv_rows.shape
    n_chunks = n_rows // CMP_STRIDE
    full = lambda a: pl.BlockSpec(a.shape, lambda b: (0,) * a.ndim)
    return pl.pallas_call(
        _compress_kernel,
        grid=(nb,),
        in_specs=[pl.BlockSpec((1, n_rows, KV_COLS // 2), lambda b: (b, 0, 0)),
                  pl.BlockSpec((1, n_rows, KV_COLS // 2), lambda b: (b, 0, 1))] + [full(a) for a in cw],
        out_specs=pl.BlockSpec((1, n_chunks, KV_COLS), lambda b: (b, 0, 0)),
        out_shape=jax.ShapeDtypeStruct((nb, n_chunks, KV_COLS), f32),
        compiler_params=_cparams("parallel"),
        name="compress_prompt",
    )(kv_rows, kv_rows, *cw)


def _dot_t0(a, b, **kw):
    return lax.dot_general(a, b, (((0,), (0,)), ((), ())), preferred_element_type=f32, **kw)


def _dot_nt(a, b, **kw):
    return lax.dot_general(a, b, (((1,), (1,)), ((), ())), preferred_element_type=f32, **kw)


def _mlstm_chunk_kernel(q_ref, k_ref, v_ref, gc_ref, gr_ref, h_ref, c_out, n_out, m_out, c_s, n_s, m_s):
    L = q_ref.shape[0]
    NH, DH = N_MLSTM_HEADS, MLSTM_HEAD_DIM

    @pl.when(pl.program_id(0) == 0)
    def _():
        c_s[...] = jnp.zeros_like(c_s)
        n_s[...] = jnp.zeros_like(n_s)
        m_s[...] = jnp.zeros_like(m_s)

    row = lax.broadcasted_iota(jnp.int32, (L, L), 0)
    col = lax.broadcasted_iota(jnp.int32, (L, L), 1)
    causal = col <= row
    gc = gc_ref[...]
    gr = gr_ref[...]
    lf_c = jax.nn.log_sigmoid(gc[:, NH:2 * NH])
    lf_r = jax.nn.log_sigmoid(gr[NH:2 * NH, :])
    f_c = jnp.dot(causal.astype(f32), lf_c, preferred_element_type=f32, precision=HIGHEST)
    f_r = jnp.dot(lf_r, (row <= col).astype(f32), preferred_element_type=f32, precision=HIGHEST)
    for h in range(NH):
        sl = slice(h * DH, (h + 1) * DH)
        fc, fr = f_c[:, h:h + 1], f_r[h:h + 1, :]
        ic, ir = gc[:, h:h + 1], gr[h:h + 1, :]
        m_prev = m_s[h:h + 1, 0:1]
        qh = q_ref[:, sl].astype(bf16)
        kh = k_ref[:, sl] * (DH ** -0.5)
        vh = v_ref[:, sl].astype(bf16)
        log_d = fc - fr + ir
        m_t = jnp.maximum(fc + m_prev, jnp.max(jnp.where(causal, log_d, NEG_INF), axis=1, keepdims=True))
        w = jnp.where(causal, jnp.exp(log_d - m_t), 0.0) * _dot_nt(qh, kh.astype(bf16))
        inter = jnp.exp(fc + m_prev - m_t)
        c_old = c_s[h]
        n_old = n_s[h:h + 1, :]
        num = (jnp.dot(w.astype(bf16), vh, preferred_element_type=f32)
               + inter * jnp.dot(qh, c_old.astype(bf16), preferred_element_type=f32))
        den = (jnp.sum(w, axis=1, keepdims=True)
               + inter * jnp.sum(q_ref[:, sl] * n_old, axis=1, keepdims=True))
        h_ref[:, sl] = num / jnp.maximum(jnp.abs(den), jnp.exp(-m_t))
        f_tot = fc[L - 1:L, :]
        m_new = m_t[L - 1:L, :]
        kw = kh * jnp.exp(f_tot - fc + ic - m_new)
        decay = jnp.exp(f_tot + m_prev - m_new)
        c_new = decay * c_old + _dot_t0(kw.astype(bf16), vh)
        n_new = decay * n_old + jnp.sum(kw, axis=0, keepdims=True)
        c_s[h] = c_new
        n_s[h:h + 1, :] = n_new
        m_s[h:h + 1, :] = jnp.broadcast_to(m_new, (1, DH))
        c_out[h] = c_new
    n_out[...] = n_s[...]
    m_out[...] = m_s[...]


def _mlstm_prompt(mq, mk, mv, gates_c, gates_r, chunk):
    t = mq.shape[0]
    NH, DH = N_MLSTM_HEADS, MLSTM_HEAD_DIM
    tok = lambda w: pl.BlockSpec((chunk, w), lambda c: (c, 0))
    return pl.pallas_call(
        _mlstm_chunk_kernel,
        grid=(t // chunk,),
        in_specs=[tok(D_MLSTM), tok(D_MLSTM), tok(D_MLSTM), tok(2 * NH),
                  pl.BlockSpec((2 * NH, chunk), lambda c: (0, c))],
        out_specs=[tok(D_MLSTM),
                   pl.BlockSpec((NH, DH, DH), lambda c: (0, 0, 0)),
                   pl.BlockSpec((8, DH), lambda c: (0, 0)),
                   pl.BlockSpec((8, DH), lambda c: (0, 0))],
        out_shape=[jax.ShapeDtypeStruct((t, D_MLSTM), f32),
                   jax.ShapeDtypeStruct((NH, DH, DH), f32),
                   jax.ShapeDtypeStruct((8, DH), f32),
                   jax.ShapeDtypeStruct((8, DH), f32)],
        scratch_shapes=[pltpu.VMEM((NH, DH, DH), f32), pltpu.VMEM((8, DH), f32), pltpu.VMEM((8, DH), f32)],
        compiler_params=_cparams("arbitrary"),
        name="mlstm_prompt",
    )(mq, mk, mv, gates_c, gates_r)


def _mlstm_step_kernel(q_ref, k_ref, v_ref, g_ref, c_ref, n_ref, m_ref, h_ref, c_out, n_out, m_out):
    NH, DH = N_MLSTM_HEADS, MLSTM_HEAD_DIM
    b = pl.program_id(0)
    row8 = lax.broadcasted_iota(jnp.int32, (8, DH), 0)
    g = g_ref[pl.ds(b, 1), :]
    m_row = m_ref[pl.ds(b, 1), :]
    m_new_row = jnp.zeros((1, NH), f32)
    lane4 = lax.broadcasted_iota(jnp.int32, (1, NH), 1)
    q_row, k_row, v_row = q_ref[pl.ds(b, 1), :], k_ref[pl.ds(b, 1), :], v_ref[pl.ds(b, 1), :]
    h_parts = []
    for h in range(NH):
        sl = slice(h * DH, (h + 1) * DH)
        q = q_row[:, sl]
        k = k_row[:, sl] * (DH ** -0.5)
        v = v_row[:, sl]
        ig = g[:, h:h + 1]
        lf = jax.nn.log_sigmoid(g[:, NH + h:NH + h + 1])
        m_prev = m_row[:, h:h + 1]
        c_old = c_ref[0, h]
        n_old = n_ref[0, h:h + 1, :]
        m_t = jnp.maximum(lf + m_prev, ig)
        w = jnp.exp(ig - m_t) * jnp.sum(q * k, axis=1, keepdims=True)
        inter = jnp.exp(lf + m_prev - m_t)
        q8 = jnp.where(row8 == 0, q, 0.0)
        qc = jnp.dot(q8, c_old, preferred_element_type=f32, precision=HIGHEST)[0:1]
        num = w * v + inter * qc
        den = w + inter * jnp.sum(q * n_old, axis=1, keepdims=True)
        h_parts.append(num / jnp.maximum(jnp.abs(den), jnp.exp(-m_t)))
        kw = k * jnp.exp(ig - m_t)
        decay = jnp.exp(lf + m_prev - m_t)
        kw8 = jnp.where(row8 == 0, kw, 0.0)
        v8 = jnp.where(row8 == 0, v, 0.0)
        c_out[0, h] = decay * c_old + _dot_t0(kw8, v8, precision=HIGHEST)
        n_out[0, h:h + 1, :] = decay * n_old + kw
        m_new_row = jnp.where(lane4 == h, m_t, m_new_row)
    h_ref[pl.ds(b, 1), :] = jnp.concatenate(h_parts, axis=1)
    m_out[pl.ds(b, 1), :] = m_new_row


def _mlstm_sample(mq, mk, mv, gates, c0, n0, m0):
    nb = mq.shape[0]
    NH, DH = N_MLSTM_HEADS, MLSTM_HEAD_DIM
    full = lambda a: pl.BlockSpec(a.shape, lambda b: (0,) * a.ndim)
    return pl.pallas_call(
        _mlstm_step_kernel,
        grid=(nb,),
        in_specs=[full(mq), full(mk), full(mv), full(gates),
                  pl.BlockSpec((1, NH, DH, DH), lambda b: (b, 0, 0, 0)),
                  pl.BlockSpec((1, NH, DH), lambda b: (b, 0, 0)),
                  full(m0)],
        out_specs=[pl.BlockSpec((nb, D_MLSTM), lambda b: (0, 0)),
                   pl.BlockSpec((1, NH, DH, DH), lambda b: (b, 0, 0, 0)),
                   pl.BlockSpec((1, NH, DH), lambda b: (b, 0, 0)),
                   pl.BlockSpec((nb, NH), lambda b: (0, 0))],
        out_shape=[jax.ShapeDtypeStruct((nb, D_MLSTM), f32),
                   jax.ShapeDtypeStruct((nb, NH, DH, DH), f32),
                   jax.ShapeDtypeStruct((nb, NH, DH), f32),
                   jax.ShapeDtypeStruct((nb, NH), f32)],
        compiler_params=_cparams("arbitrary"),
        name="mlstm_sample",
    )(mq, mk, mv, gates, c0, n0, m0)


SLC_TILE = 512


def _masked_softmax(s, mask):
    sm = jnp.where(mask, s, NEG_INF)
    mx = jnp.max(sm, axis=-1, keepdims=True)
    e = jnp.exp(sm - mx)
    p = e / jnp.sum(e, axis=-1, keepdims=True)
    return jnp.where(mx > 0.5 * NEG_INF, p, 0.0)


def _pool_matrix(n_cmp_rows, n_slc):
    i = np.arange(n_cmp_rows)[:, None]
    j = np.arange(n_slc)[None, :]
    ratio = SLC_BLOCK // CMP_STRIDE
    return jnp.asarray(((i >= ratio * j - 1) & (i <= ratio * j + ratio - 1)).astype(np.float32))


def _select_blocks(p_slc, cur, n_valid_lanes):
    nb = p_slc.shape[1]
    blk = lax.broadcasted_iota(jnp.int32, p_slc.shape, 1)
    valid = blk <= cur
    forced = (blk == 0) | (blk == cur) | (blk == cur - 1)
    score = jnp.where(valid, p_slc + jnp.where(forced, FORCE_BONUS, 0.0), -1.0)
    score = jnp.where(blk < n_valid_lanes, score, -2.0)
    blk_f = blk.astype(f32)
    sel = jnp.zeros(p_slc.shape, f32)
    picks = []
    for _ in range(N_SELECT):
        mx = jnp.max(score, axis=1, keepdims=True)
        idx = jnp.min(jnp.where(score == mx, blk_f, float(nb)), axis=1, keepdims=True)
        hit = blk_f == idx
        sel = jnp.where(hit, 1.0, sel)
        score = jnp.where(hit, -3.0, score)
        picks.append(idx.astype(jnp.int32))
    return sel, picks


_SUB = 128
_M_FLOOR = -1e20


def _with_pos_feature(k, index0=0):
    n = k.shape[0]
    pos = ((jnp.arange(n) + index0) % _SUB).astype(f32)[:, None]
    return jnp.concatenate([k, pos, jnp.zeros((n, _SUB - HEAD_DIM - 1), f32)], axis=1).astype(bf16)


def _colmax8(x):
    out = x[0:8]
    for r in range(8, x.shape[0], 8):
        out = jnp.maximum(out, x[r:r + 8])
    return out


def _colsum8(x):
    out = x[0:8]
    for r in range(8, x.shape[0], 8):
        out = out + x[r:r + 8]
    return out


_PSUM_PAD = 8
_SLC_CHAINS = 1


def _nsa_prompt_t_kernel(qT_ref, gT_ref, kc_ref, vcT_ref, ks_ref, vsT_ref, kw_ref, vwT_ref, o_ref,
                         sc_s, psum_s, sel_s, m_s, l_s, acc_s):
    tq = qT_ref.shape[1]
    L = GQA_REP * tq
    n_cmp = kc_ref.shape[1]
    n_slc = sel_s.shape[0]
    assert tq == _SUB and n_cmp * CMP_STRIDE == n_slc * SLC_BLOCK
    i = pl.program_id(0)
    start = pl.multiple_of(i * tq, tq)
    lane = lax.broadcasted_iota(jnp.int32, (1, L), 1)
    t_lane = lane % tq
    qpos = start + t_lane
    qpos_f = qpos.astype(f32)
    qpos_t = qpos[:, 0:tq]
    sub_l = lax.broadcasted_iota(jnp.int32, (_SUB, L), 0)
    sub_t = lax.broadcasted_iota(jnp.int32, (_SUB, tq), 0)
    feat_row = lax.broadcasted_iota(jnp.int32, (_SUB - HEAD_DIM, L), 0) == 0
    n_tiles = (start + tq + SLC_TILE - 1) // SLC_TILE
    nc_blocks = (start // CMP_STRIDE + (tq - CMP_BLOCK) // CMP_STRIDE) // _SUB + 1
    blocks_per_tile = SLC_TILE // SLC_BLOCK

    for g in range(N_KV_HEADS):
        vrows = slice(g * HEAD_DIM, (g + 1) * HEAD_DIM)
        slope = jnp.exp2(-8.0 * (GQA_REP * g + 1 + lane // tq).astype(f32) / N_ATT_HEADS)
        q_rows = jnp.concatenate(
            [qT_ref[(GQA_REP * g + r) * HEAD_DIM:(GQA_REP * g + r + 1) * HEAD_DIM, :] for r in range(GQA_REP)],
            axis=1).astype(f32) * (HEAD_DIM ** -0.5)
        q_pos = jnp.concatenate([q_rows, jnp.where(feat_row, slope, 0.0)], axis=0).astype(bf16)
        q_cmp = jnp.concatenate([q_rows, jnp.where(feat_row, slope * CMP_STRIDE, 0.0)], axis=0).astype(bf16)

        def cmp_scores(cb, m8):
            r0 = pl.multiple_of(cb * _SUB, _SUB)
            end0 = r0 * CMP_STRIDE + (CMP_BLOCK - 1)
            off = slope * (end0.astype(f32) - qpos_f)
            vis = (end0 + sub_l * CMP_STRIDE) <= qpos
            s = jnp.dot(kc_ref[g, pl.ds(r0, _SUB), :], q_cmp, preferred_element_type=f32)
            s = jnp.where(vis, s + off, NEG_INF)
            sc_s[pl.ds(r0, _SUB), :] = s
            return jnp.maximum(m8, _colmax8(s))

        m8 = lax.fori_loop(0, nc_blocks, cmp_scores, jnp.full((8, L), NEG_INF, f32))
        m_c = jnp.maximum(jnp.max(m8, axis=0, keepdims=True), _M_FLOOR)

        def cmp_exp(cb, carry):
            l8, o_acc = carry
            r0 = pl.multiple_of(cb * _SUB, _SUB)
            e = jnp.exp(sc_s[pl.ds(r0, _SUB), :] - m_c)
            sc_s[pl.ds(r0, _SUB), :] = e
            o_acc = o_acc + jnp.dot(vcT_ref[vrows, pl.ds(r0, _SUB)], e.astype(bf16), preferred_element_type=f32)
            return l8 + _colsum8(e), o_acc

        l8, o_c = lax.fori_loop(0, nc_blocks, cmp_exp,
                                (jnp.zeros((8, L), f32), jnp.zeros((HEAD_DIM, L), f32)))
        l_c = jnp.sum(l8, axis=0, keepdims=True)
        inv_c = jnp.where(l_c > 0.0, 1.0 / jnp.where(l_c > 0.0, l_c, 1.0), 0.0)
        o_c = o_c * inv_c
        psum_s[...] = jnp.zeros_like(psum_s)

        def cmp_group_sum(cb, carry):
            r0 = pl.multiple_of(cb * _SUB, _SUB)
            p = sc_s[pl.ds(r0, _SUB), :] * inv_c
            psum_s[pl.ds(r0 + _PSUM_PAD, _SUB), :] = ((p[:, 0:tq] + p[:, tq:2 * tq])
                                                      + (p[:, 2 * tq:3 * tq] + p[:, 3 * tq:4 * tq]))
            return carry

        lax.fori_loop(0, nc_blocks, cmp_group_sum, 0)
        ratio = SLC_BLOCK // CMP_STRIDE
        p_slc = psum_s[pl.ds(_PSUM_PAD - 1, n_slc, stride=ratio), :]
        for c in range(ratio):
            p_slc = p_slc + psum_s[pl.ds(_PSUM_PAD + c, n_slc, stride=ratio), :]

        blk = lax.broadcasted_iota(jnp.int32, (n_slc, tq), 0).astype(f32)
        cur = (qpos_t // SLC_BLOCK).astype(f32)
        forced = (blk == 0.0) | (blk == cur) | (blk == cur - 1.0)
        score = jnp.where(blk <= cur, p_slc + jnp.where(forced, FORCE_BONUS, 0.0), -1.0)
        sel = jnp.zeros((n_slc, tq), f32)
        for _ in range(N_SELECT):
            mx = jnp.max(score, axis=0, keepdims=True)
            idx = jnp.min(jnp.where(score == mx, blk, float(n_slc)), axis=0, keepdims=True)
            hit = blk == idx
            sel = jnp.where(hit, 1.0, sel)
            score = jnp.where(hit, -3.0, score)
        sel_s[...] = sel

        n_chain = m_s.shape[0]
        half_keys = SLC_TILE // n_chain
        per_blk = _SUB // SLC_BLOCK
        m_s[...] = jnp.full(m_s.shape, NEG_INF, f32)
        l_s[...] = jnp.zeros(l_s.shape, f32)
        acc_s[...] = jnp.zeros(acc_s.shape, f32)

        def slc_tile(j, carry):
            picked = sel_s[pl.ds(pl.multiple_of(j * blocks_per_tile, blocks_per_tile), blocks_per_tile), :]

            @pl.when(jnp.max(picked) > 0.0)
            def _():
                for c in range(n_chain):
                    k0 = pl.multiple_of(j * SLC_TILE + c * half_keys, half_keys)
                    s = jnp.dot(ks_ref[g, pl.ds(k0, half_keys), :], q_pos, preferred_element_type=f32)
                    parts = []
                    for w in range(half_keys // _SUB):
                        base = k0 + w * _SUB
                        b0 = (c * half_keys + w * _SUB) // SLC_BLOCK
                        picked_k = jnp.concatenate(
                            [jnp.broadcast_to(picked[b0 + u:b0 + u + 1, :], (SLC_BLOCK, tq)) for u in range(per_blk)],
                            axis=0)
                        ok = (picked_k > 0.5) & (base + sub_t <= qpos_t)
                        bias = jnp.where(ok, 0.0, NEG_INF)
                        bias = jnp.concatenate([bias] * GQA_REP, axis=1) + slope * (base.astype(f32) - qpos_f)
                        parts.append(s[w * _SUB:(w + 1) * _SUB] + bias)
                    s = jnp.concatenate(parts, axis=0)
                    m_old = m_s[c:c + 1, :]
                    m_new = jnp.maximum(jnp.maximum(m_old, jnp.max(_colmax8(s), axis=0, keepdims=True)), _M_FLOOR)
                    p = jnp.exp(s - m_new)
                    alpha = jnp.exp(m_old - m_new)
                    l_s[c:c + 1, :] = alpha * l_s[c:c + 1, :] + jnp.sum(_colsum8(p), axis=0, keepdims=True)
                    acc_s[c] = alpha * acc_s[c] + jnp.dot(vsT_ref[vrows, pl.ds(k0, half_keys)], p.astype(bf16),
                                                          preferred_element_type=f32)
                    m_s[c:c + 1, :] = m_new

            return carry

        lax.fori_loop(0, n_tiles, slc_tile, 0)
        m_fin = jnp.max(m_s[...], axis=0, keepdims=True)
        l_fin = jnp.zeros((1, L), f32)
        o_s = jnp.zeros((HEAD_DIM, L), f32)
        for c in range(n_chain):
            w_c = jnp.exp(m_s[c:c + 1, :] - jnp.maximum(m_fin, _M_FLOOR))
            l_fin = l_fin + w_c * l_s[c:c + 1, :]
            o_s = o_s + w_c * acc_s[c]
        o_s = o_s * jnp.where(l_fin > 0.0, 1.0 / jnp.where(l_fin > 0.0, l_fin, 1.0), 0.0)

        n_win_sub = (WINDOW + tq) // _SUB
        s_parts = []
        for w in range(n_win_sub):
            base = start - WINDOW + w * _SUB
            s = jnp.dot(kw_ref[g, pl.ds(start + w * _SUB, _SUB), :], q_pos, preferred_element_type=f32)
            if w == 0:
                ok = sub_l > t_lane
            elif w == n_win_sub - 1:
                ok = sub_l <= t_lane
            else:
                ok = None
            bias = slope * (base.astype(f32) - qpos_f) + jnp.where(base >= 0, 0.0, NEG_INF)
            s = s + bias
            s_parts.append(s if ok is None else jnp.where(ok, s, NEG_INF))
        m8 = _colmax8(s_parts[0])
        for s in s_parts[1:]:
            m8 = jnp.maximum(m8, _colmax8(s))
        m_w = jnp.maximum(jnp.max(m8, axis=0, keepdims=True), _M_FLOOR)
        l8 = jnp.zeros((8, L), f32)
        o_w = jnp.zeros((HEAD_DIM, L), f32)
        for w, s in enumerate(s_parts):
            e = jnp.exp(s - m_w)
            l8 = l8 + _colsum8(e)
            o_w = o_w + jnp.dot(vwT_ref[vrows, pl.ds(start + w * _SUB, _SUB)], e.astype(bf16),
                                preferred_element_type=f32)
        l_w = jnp.sum(l8, axis=0, keepdims=True)
        o_w = o_w * jnp.where(l_w > 0.0, 1.0 / jnp.where(l_w > 0.0, l_w, 1.0), 0.0)

        for r in range(GQA_REP):
            head = GQA_REP * g + r
            cols = slice(r * tq, (r + 1) * tq)
            o_ref[head * HEAD_DIM:(head + 1) * HEAD_DIM, :] = (
                gT_ref[3 * head:3 * head + 1, :] * o_c[:, cols]
                + gT_ref[3 * head + 1:3 * head + 2, :] * o_s[:, cols]
                + gT_ref[3 * head + 2:3 * head + 3, :] * o_w[:, cols])


def _nsa_prompt_t(q, small, kv_cmp, kvs, kvw):
    t = q.shape[0]
    n_cmp = kv_cmp.shape[0]
    n_slc = t // SLC_BLOCK
    kcol = lambda g: slice(g * HEAD_DIM, (g + 1) * HEAD_DIM)
    vT = lambda a: a[:, N_KV_HEADS * HEAD_DIM:].T.astype(bf16)
    qT = q.T.astype(bf16)
    gT = small[:, :32].T
    kc = jnp.stack([_with_pos_feature(kv_cmp[:, kcol(g)]) for g in range(N_KV_HEADS)])
    ks = jnp.stack([_with_pos_feature(kvs[:, kcol(g)]) for g in range(N_KV_HEADS)])
    kvw_pad = jnp.pad(kvw, ((WINDOW, 0), (0, 0)))
    kw = jnp.stack([_with_pos_feature(kvw_pad[:, kcol(g)]) for g in range(N_KV_HEADS)])
    operands = (qT, gT, kc, vT(kv_cmp), ks, vT(kvs), kw, vT(kvw_pad))
    L = GQA_REP * Q_BLOCK
    const = lambda a: pl.BlockSpec(a.shape, lambda i: (0,) * a.ndim, pipeline_mode=pl.Buffered(1))
    return pl.pallas_call(
        _nsa_prompt_t_kernel,
        grid=(t // Q_BLOCK,),
        in_specs=[pl.BlockSpec((D_ATT, Q_BLOCK), lambda i: (0, i)),
                  pl.BlockSpec((32, Q_BLOCK), lambda i: (0, i))] + [const(a) for a in operands[2:]],
        out_specs=pl.BlockSpec((D_ATT, Q_BLOCK), lambda i: (0, i)),
        out_shape=jax.ShapeDtypeStruct((D_ATT, t), f32),
        scratch_shapes=[pltpu.VMEM((n_cmp, L), f32), pltpu.VMEM((n_cmp + _PSUM_PAD, Q_BLOCK), f32),
                        pltpu.VMEM((n_slc, Q_BLOCK), f32), pltpu.VMEM((_SLC_CHAINS, L), f32),
                        pltpu.VMEM((_SLC_CHAINS, L), f32), pltpu.VMEM((_SLC_CHAINS, HEAD_DIM, L), f32)],
        compiler_params=_cparams("parallel"),
        name="nsa_prompt",
    )(*operands)


_QROWS = 16
_IDS_LANES = 128


def _group_queries_1(q, g):
    rows = [q[:, (GQA_REP * g + r) * HEAD_DIM:(GQA_REP * g + r + 1) * HEAD_DIM] for r in range(GQA_REP)]
    rows.append(jnp.zeros((_QROWS - GQA_REP, HEAD_DIM), f32))
    return jnp.concatenate(rows, axis=0) * (HEAD_DIM ** -0.5)


def _group_slopes_1(g):
    r = jnp.minimum(lax.broadcasted_iota(jnp.int32, (_QROWS, 1), 0), GQA_REP - 1)
    return jnp.exp2(-8.0 * (GQA_REP * g + 1 + r).astype(f32) / N_ATT_HEADS)


def _nsa_sample_cmp_kernel(q_ref, cmp_ref, pool_ref, oc_ref, ids_ref, *, past):
    b = pl.program_id(0)
    n_cmp = cmp_ref.shape[1]
    n_old = pool_ref.shape[1]
    q = q_ref[pl.ds(b, 1), :]
    cmpb = cmp_ref[0].astype(bf16)
    d_c = past - (lax.broadcasted_iota(jnp.int32, (1, n_cmp), 1) * CMP_STRIDE + (CMP_BLOCK - 1))
    lane = lax.broadcasted_iota(jnp.int32, (1, _IDS_LANES), 1)
    cur = jnp.full((1, 1), past // SLC_BLOCK, jnp.int32)
    oc_parts = []
    for g in range(N_KV_HEADS):
        q16 = _group_queries_1(q, g).astype(bf16)
        s = _dot_nt(q16, cmpb[:, g * HEAD_DIM:(g + 1) * HEAD_DIM]) - _group_slopes_1(g) * d_c.astype(f32)
        p = _masked_softmax(s, jnp.broadcast_to(d_c >= 0, s.shape))
        o_c = jnp.dot(p.astype(bf16), cmpb[:, (N_KV_HEADS + g) * HEAD_DIM:(N_KV_HEADS + g + 1) * HEAD_DIM],
                      preferred_element_type=f32)
        oc_parts.extend(o_c[r:r + 1] for r in range(GQA_REP))
        p_sum = jnp.sum(p[0:GQA_REP], axis=0, keepdims=True)
        p_slc = jnp.dot(jnp.broadcast_to(p_sum, (8, n_cmp)), pool_ref[...], preferred_element_type=f32,
                        precision=HIGHEST)[0:1]
        p_ext = jnp.concatenate([p_slc, jnp.zeros((1, 128), f32)], axis=1)
        _, picks = _select_blocks(p_ext, cur, past // SLC_BLOCK + 1)
        row = jnp.zeros((1, _IDS_LANES), jnp.int32)
        for k, pick in enumerate(picks):
            row = jnp.where(lane == k, pick, row)
        ids_ref[0, g:g + 1, :] = row
    oc_ref[pl.ds(b, 1), :] = jnp.concatenate(oc_parts, axis=1)


def _nsa_sample_cmp(q, cmp_s, pool, past):
    nb = q.shape[0]
    return pl.pallas_call(
        functools.partial(_nsa_sample_cmp_kernel, past=past),
        grid=(nb,),
        in_specs=[pl.BlockSpec(q.shape, lambda b: (0, 0)),
                  pl.BlockSpec((1,) + cmp_s.shape[1:], lambda b: (b, 0, 0)),
                  pl.BlockSpec(pool.shape, lambda b: (0, 0))],
        out_specs=[pl.BlockSpec((nb, D_ATT), lambda b: (0, 0)),
                   pl.BlockSpec((1, N_KV_HEADS, _IDS_LANES), lambda b: (b, 0, 0))],
        out_shape=[jax.ShapeDtypeStruct((nb, D_ATT), f32),
                   jax.ShapeDtypeStruct((nb, N_KV_HEADS, _IDS_LANES), jnp.int32)],
        compiler_params=_cparams("arbitrary"),
        name="nsa_sample_cmp",
    )(q, cmp_s, pool)


def _softmax_with_new(s, mask, s_new, new_ok):
    s = jnp.where(mask, s, NEG_INF)
    s_new = jnp.where(new_ok, s_new, NEG_INF)
    m = jnp.maximum(jnp.max(s, axis=1, keepdims=True), s_new)
    p = jnp.where(mask, jnp.exp(s - m), 0.0)
    p_new = jnp.where(new_ok, jnp.exp(s_new - m), 0.0)
    l = jnp.sum(p, axis=1, keepdims=True) + p_new
    inv = jnp.where(l > 0.0, 1.0 / jnp.where(l > 0.0, l, 1.0), 0.0)
    return p * inv, p_new * inv


def _nsa_sample_sel_kernel(ids_ref, pt_ref, pool_ref, q_ref, sm_ref, kvs_ref, kvw_ref, win_ref, oc_ref, o_ref,
                           buf, sem, *, past):
    b = pl.program_id(0)
    n_pages = past // PAGE_ROWS
    last_blk = past // SLC_BLOCK
    blocks_per_page = PAGE_ROWS // SLC_BLOCK

    def page_copy(g, k, c):
        blk = jnp.minimum(ids_ref[(b * N_KV_HEADS + g) * N_SELECT + k], last_blk - 1)
        page = pt_ref[b * n_pages + blk // blocks_per_page]
        return pltpu.make_async_copy(pool_ref.at[page, c, g], buf.at[g, c, k], sem)

    copies = [(g, k, c) for g in range(N_KV_HEADS) for k in range(N_SELECT) for c in range(2)]
    for g, k, c in copies:
        page_copy(g, k, c).start()
    for g, k, c in copies:
        page_copy(g, k, c).wait()

    q = q_ref[pl.ds(b, 1), :]
    gates = sm_ref[pl.ds(b, 1), :]
    new_s = kvs_ref[pl.ds(b, 1), :]
    new_w = kvw_ref[pl.ds(b, 1), :]
    n_sel = N_SELECT * PAGE_ROWS
    lane = lax.broadcasted_iota(jnp.int32, (1, n_sel), 1)
    w_buf = win_ref.shape[1]
    d_w = w_buf - lax.broadcasted_iota(jnp.int32, (1, w_buf), 1)
    mask_w = jnp.broadcast_to((d_w < WINDOW) & (past - d_w >= 0), (_QROWS, w_buf))
    always = jnp.full((_QROWS, 1), True)
    o_c = oc_ref[pl.ds(b, 1), :]
    out_parts = []
    for g in range(N_KV_HEADS):
        kcol = slice(g * HEAD_DIM, (g + 1) * HEAD_DIM)
        vcol = slice((N_KV_HEADS + g) * HEAD_DIM, (N_KV_HEADS + g + 1) * HEAD_DIM)
        q16f = _group_queries_1(q, g)
        q16 = q16f.astype(bf16)
        slope = _group_slopes_1(g)

        blk_vec = jnp.zeros((1, n_sel), jnp.int32)
        for k in range(N_SELECT):
            blk_vec = jnp.where(lane // PAGE_ROWS == k, ids_ref[(b * N_KV_HEADS + g) * N_SELECT + k], blk_vec)
        kpos = (blk_vec // blocks_per_page) * PAGE_ROWS + lane % PAGE_ROWS
        d_s = past - kpos
        mask_s = jnp.broadcast_to((d_s >= 0) & (blk_vec < last_blk) & (kpos // SLC_BLOCK == blk_vec),
                                  (_QROWS, n_sel))
        has_new = jnp.max(jnp.where(blk_vec == last_blk, 1, 0), axis=1, keepdims=True) > 0
        kt = jnp.concatenate([buf[g, 0, k] for k in range(N_SELECT)], axis=1).astype(bf16)
        vt = jnp.concatenate([buf[g, 1, k] for k in range(N_SELECT)], axis=1).astype(bf16)
        s = jnp.dot(q16, kt, preferred_element_type=f32) - slope * d_s.astype(f32)
        s_new = jnp.sum(q16f * new_s[:, kcol], axis=1, keepdims=True)
        p, p_new = _softmax_with_new(s, mask_s, s_new, jnp.broadcast_to(has_new, (_QROWS, 1)))
        o_s = _dot_nt(p.astype(bf16), vt) + p_new * new_s[:, vcol]

        wb = win_ref[0].astype(bf16)
        s = _dot_nt(q16, wb[:, kcol]) - slope * d_w.astype(f32)
        s_new = jnp.sum(q16f * new_w[:, kcol], axis=1, keepdims=True)
        p, p_new = _softmax_with_new(s, mask_w, s_new, always)
        o_w = jnp.dot(p.astype(bf16), wb[:, vcol], preferred_element_type=f32) + p_new * new_w[:, vcol]

        for r in range(GQA_REP):
            head = GQA_REP * g + r
            hs = slice(head * HEAD_DIM, (head + 1) * HEAD_DIM)
            out_parts.append(gates[:, 3 * head:3 * head + 1] * o_c[:, hs]
                             + gates[:, 3 * head + 1:3 * head + 2] * o_s[r:r + 1]
                             + gates[:, 3 * head + 2:3 * head + 3] * o_w[r:r + 1])
    o_ref[pl.ds(b, 1), :] = jnp.concatenate(out_parts, axis=1)


def _nsa_sample_sel(ids, page_table, pool_slc, q, small, kvs_new, kvw_new, win_buf, o_c, past):
    nb = q.shape[0]
    full = lambda a: pl.BlockSpec(a.shape, lambda b, ids, pt: (0,) * a.ndim)
    return pl.pallas_call(
        functools.partial(_nsa_sample_sel_kernel, past=past),
        grid_spec=pltpu.PrefetchScalarGridSpec(
            num_scalar_prefetch=2,
            grid=(nb,),
            in_specs=[pl.BlockSpec(memory_space=pl.ANY), full(q), full(small), full(kvs_new), full(kvw_new),
                      pl.BlockSpec((1,) + win_buf.shape[1:], lambda b, ids, pt: (b, 0, 0)), full(o_c)],
            out_specs=pl.BlockSpec((nb, D_ATT), lambda b, ids, pt: (0, 0)),
            scratch_shapes=[pltpu.VMEM((N_KV_HEADS, 2, N_SELECT, HEAD_DIM, PAGE_ROWS), f32),
                            pltpu.SemaphoreType.DMA(())]),
        out_shape=jax.ShapeDtypeStruct((nb, D_ATT), f32),
        compiler_params=_cparams("arbitrary"),
        name="nsa_sample_sel",
    )(ids, page_table.reshape(-1), pool_slc, q, small, kvs_new, kvw_new, win_buf, o_c)


def _mixout_kernel(att_ref, mh_ref, mo_ref, x_ref, ga_ref, gm_ref, wo_ref, nf_ref, wq_ref, h_ref, xn_ref,
                   qp_ref):
    parts = []
    for h in range(N_ATT_HEADS):
        sl = slice(h * HEAD_DIM, (h + 1) * HEAD_DIM)
        parts.append(_rms(att_ref[:, sl], ga_ref[:, sl]))
    for h in range(N_MLSTM_HEADS):
        sl = slice(h * MLSTM_HEAD_DIM, (h + 1) * MLSTM_HEAD_DIM)
        parts.append(mo_ref[:, sl] * _rms(mh_ref[:, sl], gm_ref[:, sl]))
    cat = jnp.concatenate(parts, axis=1).astype(bf16)
    h1 = x_ref[...] + jnp.dot(cat, wo_ref[...], preferred_element_type=f32)
    h_ref[...] = h1
    xn = _rms(h1, nf_ref[...]).astype(bf16)
    xn_ref[...] = xn
    qp_ref[...] = jnp.dot(xn, wq_ref[...], preferred_element_type=f32)


def _mix_output(att, mh, mo, x, g_att, g_ml, w_out_bf, norm_ffn, w_q_bf, tm):
    n = x.shape[0]
    tok = lambda w: pl.BlockSpec((tm, w), lambda i: (i, 0))
    full = lambda a: pl.BlockSpec(a.shape, lambda i: (0,) * a.ndim)
    return pl.pallas_call(
        _mixout_kernel,
        grid=(n // tm,),
        in_specs=[tok(D_ATT), tok(D_MLSTM), tok(D_MLSTM), tok(D_MODEL), full(g_att), full(g_ml),
                  full(w_out_bf), full(norm_ffn), full(w_q_bf)],
        out_specs=[tok(D_MODEL), tok(D_MODEL), tok(D_MODEL)],
        out_shape=[jax.ShapeDtypeStruct((n, D_MODEL), f32), jax.ShapeDtypeStruct((n, D_MODEL), bf16),
                   jax.ShapeDtypeStruct((n, D_MODEL), f32)],
        compiler_params=_cparams("parallel"),
        name="mix_output",
    )(att, mh, mo, x, g_att, g_ml, w_out_bf, norm_ffn, w_q_bf)


def _topk_rows(s, k):
    n = s.shape[0]
    rows = lax.broadcasted_iota(jnp.int32, s.shape, 0)
    vals, idxs = [], []
    for _ in range(k):
        mx = jnp.max(s, axis=0, keepdims=True)
        idx = jnp.min(jnp.where(s == mx, rows, n), axis=0, keepdims=True)
        vals.append(mx)
        idxs.append(idx)
        s = jnp.where(rows == idx, NEG_INF, s)
    return jnp.concatenate(vals, axis=0), jnp.concatenate(idxs, axis=0)


def _peer_topk_kernel(qp_ref, sub_ref, ei_ref, ej_ref, g_ref):
    K = PEER_TOPK
    half = PEER_D_KEY // 2
    ei, ej, gg = [], [], []
    for h in range(PEER_HEADS):
        sv, si = [], []
        for c in range(2):
            qhc = qp_ref[:, (2 * h + c) * half:(2 * h + c + 1) * half]
            s = _dot_nt(sub_ref[h, c], qhc, precision=HIGHEST)
            v, i = _topk_rows(s, K)
            sv.append(v)
            si.append(i)
        n_t = sv[0].shape[1]
        counts = [K // (a + 1) for a in range(K)]
        n_cand = -(-sum(counts) // 8) * 8
        pad = n_cand - sum(counts)
        cand = jnp.concatenate([sv[0][a:a + 1] + sv[1][0:counts[a]] for a in range(K)]
                               + [jnp.full((pad, n_t), NEG_INF, f32)], axis=0)
        pos_i = jnp.concatenate([jnp.broadcast_to(si[0][a:a + 1], (counts[a], n_t)) for a in range(K)]
                                + [jnp.zeros((pad, n_t), jnp.int32)], axis=0)
        pos_j = jnp.concatenate([si[1][0:counts[a]] for a in range(K)] + [jnp.zeros((pad, n_t), jnp.int32)],
                                axis=0)
        rows = lax.broadcasted_iota(jnp.int32, cand.shape, 0)
        best, bi, bj = [], [], []
        for _ in range(K):
            mx = jnp.max(cand, axis=0, keepdims=True)
            pos = jnp.min(jnp.where(cand == mx, rows, n_cand), axis=0, keepdims=True)
            hit = rows == pos
            best.append(mx)
            bi.append(jnp.max(jnp.where(hit, pos_i, -1), axis=0, keepdims=True))
            bj.append(jnp.max(jnp.where(hit, pos_j, -1), axis=0, keepdims=True))
            cand = jnp.where(hit, NEG_INF, cand)
        best = jnp.concatenate(best, axis=0)
        e = jnp.exp(best - best[0:1])
        gg.append(e / jnp.sum(e, axis=0, keepdims=True))
        ei.append(jnp.concatenate(bi, axis=0))
        ej.append(jnp.concatenate(bj, axis=0))
    ei_ref[...] = jnp.concatenate(ei, axis=0).astype(f32).T
    ej_ref[...] = jnp.concatenate(ej, axis=0).astype(f32).T
    g_ref[...] = jnp.concatenate(gg, axis=0).T


def _peer_topk(qp, sub_keys, tm):
    n = qp.shape[0]
    hk = PEER_HEADS * PEER_TOPK
    return pl.pallas_call(
        _peer_topk_kernel,
        grid=(n // tm,),
        in_specs=[pl.BlockSpec((tm, D_MODEL), lambda i: (i, 0)),
                  pl.BlockSpec(sub_keys.shape, lambda i: (0, 0, 0, 0))],
        out_specs=[pl.BlockSpec((tm, hk), lambda i: (i, 0))] * 3,
        out_shape=[jax.ShapeDtypeStruct((n, hk), f32)] * 3,
        compiler_params=_cparams("parallel"),
        name="peer_topk",
    )(qp, sub_keys)


_GATE_UNROLL = 32
_GATE_PARTS = 2


def _peer_dense_kernel(ei_ref, ej_ref, gg_ref, x_ref, u_ref, v_ref, h_ref, nf_ref, o_ref, acc_ref, g_s):
    part, e = pl.program_id(1), pl.program_id(2)
    tm = x_ref.shape[0]
    et = u_ref.shape[0]
    nk = PEER_N_KEYS
    part_rows = nk // _GATE_PARTS

    @pl.when((part == 0) & (e == 0))
    def _():
        acc_ref[...] = jnp.zeros_like(acc_ref)

    @pl.when(e == 0)
    def _():
        i0 = (part * part_rows).astype(f32)
        sub_i = lax.broadcasted_iota(jnp.int32, (part_rows, ei_ref.shape[1]), 0).astype(f32) + i0
        sub_j = lax.broadcasted_iota(jnp.int32, (nk, ei_ref.shape[1]), 0).astype(f32)

        zero = jnp.zeros((part_rows, ei_ref.shape[1]), bf16)

        def body(tp, carry):
            t0 = pl.multiple_of(tp * 2, 2)
            ei, ej, gg = ei_ref[pl.ds(t0, 2), :], ej_ref[pl.ds(t0, 2), :], gg_ref[pl.ds(t0, 2), :]
            a = [jnp.where(sub_i == ei[u:u + 1], 1.0, 0.0).astype(bf16) for u in range(2)]
            b = [jnp.where(sub_j == ej[u:u + 1], gg[u:u + 1], 0.0).astype(bf16) for u in range(2)]
            lhs = jnp.concatenate([jnp.concatenate([a[0], zero], axis=1), jnp.concatenate([zero, a[1]], axis=1)],
                                  axis=0)
            g_s[pl.ds(pl.multiple_of(t0 * part_rows, 2 * part_rows), 2 * part_rows), :] = _dot_nt(
                lhs, jnp.concatenate(b, axis=1))
            return carry

        lax.fori_loop(0, tm // 2, body, 0, unroll=_GATE_UNROLL)

    rows = et // nk
    g = jnp.concatenate([g_s[pl.ds(e * rows + r, tm, stride=part_rows), :] for r in range(rows)], axis=1)
    act = jax.nn.gelu(_dot_nt(x_ref[...], u_ref[...]))
    acc_ref[...] += jnp.dot((g * act).astype(bf16), v_ref[...], preferred_element_type=f32)

    @pl.when((part == _GATE_PARTS - 1) & (e == pl.num_programs(2) - 1))
    def _():
        o_ref[...] = _rms(h_ref[...] + acc_ref[...], nf_ref[...])


def _peer_dense(ei, ej, gg, xn_bf, u_bf, v_bf, h1, norm_final, tm, et):
    n = xn_bf.shape[0]
    n_exp = u_bf.shape[0]
    hk = ei.shape[1]
    tok = lambda w: pl.BlockSpec((tm, w), lambda i, p, e: (i, 0))
    steps = n_exp // _GATE_PARTS // et
    expert_tile = pl.BlockSpec((et, D_MODEL), lambda i, p, e: (p * steps + e, 0))
    return pl.pallas_call(
        _peer_dense_kernel,
        grid=(n // tm, _GATE_PARTS, steps),
        in_specs=[tok(hk), tok(hk), tok(hk), tok(D_MODEL), expert_tile, expert_tile, tok(D_MODEL),
                  pl.BlockSpec((1, D_MODEL), lambda i, p, e: (0, 0))],
        out_specs=tok(D_MODEL),
        out_shape=jax.ShapeDtypeStruct((n, D_MODEL), f32),
        scratch_shapes=[pltpu.VMEM((tm, D_MODEL), f32),
                        pltpu.VMEM((tm * PEER_N_KEYS // _GATE_PARTS, PEER_N_KEYS), f32)],
        compiler_params=_cparams("parallel", "arbitrary", "arbitrary"),
        name="peer_dense",
    )(ei, ej, gg, xn_bf, u_bf, v_bf, h1, norm_final)


def _channel_mix_and_norm(h1, xn_bf, qp, sub_keys, u_bf, v_bf, norm_final, tm_topk, tm, et):
    ei, ej, gg = _peer_topk(qp, sub_keys, tm_topk)
    return _peer_dense(ei, ej, gg, xn_bf, u_bf, v_bf, h1, norm_final, tm, et)


PROMPT_TM = 512
PEER_TM = 512
PEER_ET = 1024
MLSTM_CHUNK = 256
SAMPLE_PAD = 128


def kernel(x_prompt, x_sample, cache_cmp_kv, cache_slc_kv, cache_win_kv, state_mlstm_C, state_mlstm_n,
           state_mlstm_m, page_table, norm_mix, w_in, b_igate, b_fgate, pe_cmp, w_cmp1, w_cmp2, norm_att_out,
           norm_mlstm_out, w_out, norm_ffn, peer_wq, peer_subkeys, peer_u, peer_v, norm_final):
    assert x_prompt.shape[0] == 1 and x_sample.shape[1] == 1 and w_in.shape[0] == 1
    _, t, d = x_prompt.shape
    nb = x_sample.shape[0]
    n_pool = cache_cmp_kv.shape[1]
    past = page_table.shape[1] * PAGE_ROWS
    w_buf = cache_win_kv.shape[2]
    row = (2, N_KV_HEADS, HEAD_DIM)
    NH, DH = N_MLSTM_HEADS, MLSTM_HEAD_DIM
    g0 = N_GATE_COLS

    w_re = _relayout_w_in(w_in[0])
    bias = _bias_row(b_igate[0], b_fgate[0])
    cw = _compress_weights(pe_cmp[0], w_cmp1[0], w_cmp2[0])
    nm = norm_mix[0][None]
    ga, gm, nf, nfin = norm_att_out[0][None], norm_mlstm_out[0][None], norm_ffn[0][None], norm_final[None]
    w_out_bf, w_q_bf = w_out[0].astype(bf16), peer_wq[0].astype(bf16)
    u_bf, v_bf = peer_u[0].astype(bf16), peer_v[0].astype(bf16)
    sub_keys = peer_subkeys[0]

    xp = x_prompt.reshape(t, d)
    q, kvc, kvs, kvw, mq, mk, mv, mo, sm = _project(xp, nm, w_re, bias, PROMPT_TM)
    cmp_p = _compress_prompt(kvc[None], cw)[0]
    att = _nsa_prompt_t(q, sm, cmp_p, kvs, kvw).T
    gates_c = sm[:, g0:g0 + 2 * NH]
    mh, c_p, n_p, m_p = _mlstm_prompt(mq, mk, mv, gates_c, gates_c.T, MLSTM_CHUNK)
    h1, xn_bf, qp = _mix_output(att, mh, mo, xp, ga, gm, w_out_bf, nf, w_q_bf, 256)
    y_p = _channel_mix_and_norm(h1, xn_bf, qp, sub_keys, u_bf, v_bf, nfin, 256, PEER_TM, PEER_ET)
    w_keep = min(WINDOW, t)
    outs_p = (kvc.reshape((1, 1, t) + row), kvs.reshape((1, 1, t) + row),
              kvw[t - w_keep:].reshape((1, 1, w_keep) + row),
              c_p[None, None], n_p[:NH][None, None], m_p[:NH, 0][None, None])

    xs = x_sample.reshape(nb, d)
    q, kvc_s, kvs_s, kvw_s, mq, mk, mv, mo, sm = _project(xs, nm, w_re, bias, nb)
    cmp_s = _compress_paged(page_table, jnp.transpose(cache_cmp_kv[0], (0, 2, 3, 4, 1)), cw)
    pool_s = _pool_matrix(past // CMP_STRIDE, past // SLC_BLOCK)
    o_c, ids = _nsa_sample_cmp(q, cmp_s, pool_s, past)
    win_buf = cache_win_kv[0].reshape(nb, w_buf, KV_COLS)
    att = _nsa_sample_sel(ids[:, :, :N_SELECT].reshape(-1), page_table,
                          jnp.transpose(cache_slc_kv[0], (0, 2, 3, 4, 1)), q, sm, kvs_s, kvw_s, win_buf, o_c, past)
    mh, c_s, n_s, m_s = _mlstm_sample(mq, mk, mv, sm[:, g0:g0 + 2 * NH], state_mlstm_C[0], state_mlstm_n[0],
                                      state_mlstm_m[0])
    h1, xn_bf, qp = _mix_output(att, mh, mo, xs, ga, gm, w_out_bf, nf, w_q_bf, nb)
    padr = lambda a: jnp.pad(a, ((0, SAMPLE_PAD - nb), (0, 0)))
    y_s = _channel_mix_and_norm(padr(h1), padr(xn_bf), padr(qp), sub_keys, u_bf, v_bf, nfin, SAMPLE_PAD,
                                SAMPLE_PAD, PEER_ET)[:nb]
    win_all = jnp.concatenate([win_buf, kvw_s[:, None, :]], axis=1)
    w_keep_s = min(WINDOW, w_buf + 1)
    outs_s = (kvc_s.reshape((1, nb, 1) + row), kvs_s.reshape((1, nb, 1) + row),
              win_all[:, w_buf + 1 - w_keep_s:].reshape((1, nb, w_keep_s) + row),
              c_s[None], n_s[None], m_s[None])

    return (y_p.reshape(1, t, d), y_s.reshape(nb, 1, d),
            outs_p[0], outs_s[0], outs_p[1], outs_s[1], outs_p[2], outs_s[2],
            outs_p[3], outs_s[3], outs_p[4], outs_s[4], outs_p[5], outs_s[5])
```

```python
import functools

import jax
import jax.numpy as jnp
import numpy as np
from jax import lax
from jax.experimental import pallas as pl
from jax.experimental.pallas import tpu as pltpu

f32 = jnp.float32
bf16 = jnp.bfloat16
HIGHEST = lax.Precision.HIGHEST

D_MODEL = 1024
HEAD_DIM = 64
N_ATT_HEADS = 8
N_KV_HEADS = 2
GQA_REP = 4
D_ATT = 512
KV_COLS = 256
CMP_BLOCK = 32
CMP_STRIDE = 16
CMP_HIDDEN = 128
SLC_BLOCK = 64
N_SELECT = 16
WINDOW = 512
Q_BLOCK = 128
FORCE_BONUS = 1000.0
N_MLSTM_HEADS = 4
MLSTM_HEAD_DIM = 128
D_MLSTM = 512
N_GATE_COLS = 3 * N_ATT_HEADS
SPLIT_SIZES = (D_ATT, KV_COLS, KV_COLS, KV_COLS, N_GATE_COLS, D_MLSTM, D_MLSTM, D_MLSTM, D_MLSTM,
               N_MLSTM_HEADS, N_MLSTM_HEADS)
SMALL_COLS = 128
PEER_N_KEYS = 128
PEER_HEADS = 8
PEER_TOPK = 16
PEER_D_KEY = 128
NORM_EPS = 1e-6
NEG_INF = -1e30
VMEM_LIMIT = 56 * 1024 * 1024


def _cparams(*sem):
    return pltpu.CompilerParams(dimension_semantics=sem, vmem_limit_bytes=VMEM_LIMIT)


def _rms(x, w):
    return x * lax.rsqrt(jnp.mean(x * x, axis=-1, keepdims=True) + NORM_EPS) * w


_P_Q, _P_KVC, _P_KVS, _P_KVW, _P_MQ, _P_MK, _P_MV, _P_MO, _P_SM, _P_END = (
    0, 512, 768, 1024, 1280, 1792, 2304, 2816, 3328, 3456)


def _relayout_w_in(w_in):
    pts = np.cumsum(SPLIT_SIZES)[:-1].tolist()
    q, kc, ks, kw, ga, mq, mk, mv, mo, ig, fg = jnp.split(w_in, pts, axis=-1)
    pad = jnp.zeros((w_in.shape[0], SMALL_COLS - N_GATE_COLS - 2 * N_MLSTM_HEADS), w_in.dtype)
    return jnp.concatenate([q, kc, ks, kw, mq, mk, mv, mo, ga, ig, fg, pad], axis=-1).astype(bf16)


def _bias_row(b_i, b_f):
    pad = jnp.zeros((SMALL_COLS - N_GATE_COLS - 2 * N_MLSTM_HEADS,), f32)
    return jnp.concatenate([jnp.zeros((N_GATE_COLS,), f32), b_i, b_f, pad])[None, :]


def _proj_kernel(x_ref, nw_ref, w_ref, b_ref, q_ref, kvc_ref, kvs_ref, kvw_ref, mq_ref, mk_ref, mv_ref,
                 mo_ref, sm_ref):
    xb = _rms(x_ref[...], nw_ref[...]).astype(bf16)

    def mm(lo, hi):
        return jnp.dot(xb, w_ref[:, lo:hi], preferred_element_type=f32)

    q_ref[...] = mm(_P_Q, _P_KVC)
    kvc_ref[...] = mm(_P_KVC, _P_KVS)
    kvs_ref[...] = mm(_P_KVS, _P_KVW)
    kvw_ref[...] = mm(_P_KVW, _P_MQ)
    mq_ref[...] = mm(_P_MQ, _P_MK)
    mk_ref[...] = mm(_P_MK, _P_MV)
    mv_ref[...] = mm(_P_MV, _P_MO)
    mo_ref[...] = jax.nn.sigmoid(mm(_P_MO, _P_SM))
    s = mm(_P_SM, _P_END) + b_ref[...]
    col = lax.broadcasted_iota(jnp.int32, s.shape, 1)
    sm_ref[...] = jnp.where(col < N_GATE_COLS, jax.nn.sigmoid(s), s)


def _project(x, norm_w, w_re, bias_row, tm):
    n = x.shape[0]
    widths = (D_ATT, KV_COLS, KV_COLS, KV_COLS, D_MLSTM, D_MLSTM, D_MLSTM, D_MLSTM, SMALL_COLS)
    return pl.pallas_call(
        _proj_kernel,
        grid=(n // tm,),
        in_specs=[pl.BlockSpec((tm, D_MODEL), lambda i: (i, 0)),
                  pl.BlockSpec((1, D_MODEL), lambda i: (0, 0)),
                  pl.BlockSpec((D_MODEL, _P_END), lambda i: (0, 0)),
                  pl.BlockSpec((1, SMALL_COLS), lambda i: (0, 0))],
        out_specs=[pl.BlockSpec((tm, w), lambda i: (i, 0)) for w in widths],
        out_shape=[jax.ShapeDtypeStruct((n, w), f32) for w in widths],
        compiler_params=_cparams("parallel"),
        name="proj",
    )(x, norm_w, w_re, bias_row)


_HID_COLS = 2 * N_KV_HEADS * CMP_HIDDEN
PAGE_ROWS = 128


def _compress_weights(pe, w1, w2):
    eye = jnp.eye(2, dtype=f32)

    def big1(w):
        return jnp.einsum('cldh,gf->clfdgh', w, eye).reshape(2, CMP_STRIDE * KV_COLS // 2, _HID_COLS // 2).astype(bf16)

    def pe_rows(p):
        row = jnp.broadcast_to(p.transpose(1, 0, 2)[:, :, None, :], (2, CMP_STRIDE, N_KV_HEADS, HEAD_DIM))
        row = row.reshape(2, 1, -1)
        return jnp.concatenate([row, jnp.zeros((2, 7, row.shape[-1]), f32)], axis=1)

    w2_big = jnp.einsum('chd,ce,gf->efhcgd', w2, eye, eye).reshape(_HID_COLS, KV_COLS).astype(bf16)
    pe_all = jnp.stack([pe_rows(pe[:CMP_STRIDE]), pe_rows(pe[CMP_STRIDE:])], axis=0).astype(bf16)
    return big1(w1[:, :CMP_STRIDE]), big1(w1[:, CMP_STRIDE:]), pe_all, w2_big


_CMP_ROWS = 256


def _compress_math(x_lo, x_hi, w1a_ref, w1b_ref, pe_ref, w2_ref):
    n_chunks = x_lo.shape[0] // CMP_STRIDE
    step = min(_CMP_ROWS, n_chunks)
    first, second = [], []
    for c0 in range(0, n_chunks, step):
        f_kv, s_kv = [], []
        for kv, x_ref in enumerate((x_lo, x_hi)):
            x = jnp.concatenate([x_ref[pl.ds(c0 * CMP_STRIDE + l, step, stride=CMP_STRIDE), :]
                                 for l in range(CMP_STRIDE)], axis=1).astype(bf16)
            f_kv.append(jnp.dot(x, w1a_ref[kv], preferred_element_type=f32))
            s_kv.append(jnp.dot(x, w1b_ref[kv], preferred_element_type=f32))
        first.append(jnp.concatenate(f_kv, axis=1))
        second.append(jnp.concatenate(s_kv, axis=1))
    first = jnp.concatenate(first, axis=0)
    second = jnp.concatenate(second, axis=0)
    pe_term = jnp.concatenate(
        [(jnp.dot(pe_ref[0, kv], w1a_ref[kv], preferred_element_type=f32)
          + jnp.dot(pe_ref[1, kv], w1b_ref[kv], preferred_element_type=f32))[0:1] for kv in range(2)], axis=1)
    h = jax.nn.gelu(first + pltpu.roll(second, n_chunks - 1, 0) + pe_term)
    return jnp.dot(h.astype(bf16), w2_ref[...], preferred_element_type=f32)


def _compress_kernel(x_lo, x_hi, w1a_ref, w1b_ref, pe_ref, w2_ref, o_ref):
    o_ref[0] = _compress_math(x_lo.at[0], x_hi.at[0], w1a_ref, w1b_ref, pe_ref, w2_ref)


def _compress_paged_kernel(pt_ref, pool_ref, w1a_ref, w1b_ref, pe_ref, w2_ref, o_ref, raw_s, x_s, sem):
    b = pl.program_id(0)
    n_pages = raw_s.shape[0]
    lanes = N_KV_HEADS * HEAD_DIM

    def page_copy(seq, p):
        return pltpu.make_async_copy(pool_ref.at[pt_ref[seq * n_pages + p]], raw_s.at[p], sem)

    def start_pages(seq):
        def start(p, c):
            page_copy(seq, p).start()
            return c
        lax.fori_loop(0, n_pages, start, 0)

    def wait(p, c):
        page_copy(b, p).wait()
        return c

    def to_rows(p, c):
        r0 = pl.multiple_of(p * PAGE_ROWS, PAGE_ROWS)
        for kv in range(2):
            x_s[kv, pl.ds(r0, PAGE_ROWS), :] = raw_s[p, kv].reshape(lanes, PAGE_ROWS).T
        return c

    @pl.when(b == 0)
    def _():
        start_pages(b)

    lax.fori_loop(0, n_pages, wait, 0)
    lax.fori_loop(0, n_pages, to_rows, 0, unroll=4)

    @pl.when(b + 1 < pl.num_programs(0))
    def _():
        start_pages(b + 1)

    o_ref[0] = _compress_math(x_s.at[0], x_s.at[1], w1a_ref, w1b_ref, pe_ref, w2_ref)


def _compress_paged(page_table, pool_t, cw):
    nb, n_pages = page_table.shape
    n_chunks = n_pages * PAGE_ROWS // CMP_STRIDE
    full = lambda a: pl.BlockSpec(a.shape, lambda b, pt: (0,) * a.ndim, pipeline_mode=pl.Buffered(1))
    return pl.pallas_call(
        _compress_paged_kernel,
        grid_spec=pltpu.PrefetchScalarGridSpec(
            num_scalar_prefetch=1,
            grid=(nb,),
            in_specs=[pl.BlockSpec(memory_space=pl.ANY)] + [full(a) for a in cw],
            out_specs=pl.BlockSpec((1, n_chunks, KV_COLS), lambda b, pt: (b, 0, 0)),
            scratch_shapes=[pltpu.VMEM((n_pages,) + pool_t.shape[1:], f32),
                            pltpu.VMEM((2, n_pages * PAGE_ROWS, KV_COLS // 2), f32),
                            pltpu.SemaphoreType.DMA(())]),
        out_shape=jax.ShapeDtypeStruct((nb, n_chunks, KV_COLS), f32),
        compiler_params=_cparams("arbitrary"),
        name="compress_paged",
    )(page_table.reshape(-1), pool_t, *cw)


def _compress_prompt(kv_rows, cw):
    nb, n_rows, _ = kv_rows.shape
    n_chunks = n_rows // CMP_STRIDE
    full = lambda a: pl.BlockSpec(a.shape, lambda b: (0,) * a.ndim)
    return pl.pallas_call(
        _compress_kernel,
        grid=(nb,),
        in_specs=[pl.BlockSpec((1, n_rows, KV_COLS // 2), lambda b: (b, 0, 0)),
                  pl.BlockSpec((1, n_rows, KV_COLS // 2), lambda b: (b, 0, 1))] + [full(a) for a in cw],
        out_specs=pl.BlockSpec((1, n_chunks, KV_COLS), lambda b: (b, 0, 0)),
        out_shape=jax.ShapeDtypeStruct((nb, n_chunks, KV_COLS), f32),
        compiler_params=_cparams("parallel"),
        name="compress_prompt",
    )(kv_rows, kv_rows, *cw)


def _dot_t0(a, b, **kw):
    return lax.dot_general(a, b, (((0,), (0,)), ((), ())), preferred_element_type=f32, **kw)


def _dot_nt(a, b, **kw):
    return lax.dot_general(a, b, (((1,), (1,)), ((), ())), preferred_element_type=f32, **kw)


def _mlstm_chunk_kernel(q_ref, k_ref, v_ref, gc_ref, gr_ref, h_ref, c_out, n_out, m_out, c_s, n_s, m_s):
    L = q_ref.shape[0]
    NH, DH = N_MLSTM_HEADS, MLSTM_HEAD_DIM

    @pl.when(pl.program_id(0) == 0)
    def _():
        c_s[...] = jnp.zeros_like(c_s)
        n_s[...] = jnp.zeros_like(n_s)
        m_s[...] = jnp.zeros_like(m_s)

    row = lax.broadcasted_iota(jnp.int32, (L, L), 0)
    col = lax.broadcasted_iota(jnp.int32, (L, L), 1)
    causal = col <= row
    gc = gc_ref[...]
    gr = gr_ref[...]
    lf_c = jax.nn.log_sigmoid(gc[:, NH:2 * NH])
    lf_r = jax.nn.log_sigmoid(gr[NH:2 * NH, :])
    f_c = jnp.dot(causal.astype(f32), lf_c, preferred_element_type=f32, precision=HIGHEST)
    f_r = jnp.dot(lf_r, (row <= col).astype(f32), preferred_element_type=f32, precision=HIGHEST)
    for h in range(NH):
        sl = slice(h * DH, (h + 1) * DH)
        fc, fr = f_c[:, h:h + 1], f_r[h:h + 1, :]
        ic, ir = gc[:, h:h + 1], gr[h:h + 1, :]
        m_prev = m_s[h:h + 1, 0:1]
        qh = q_ref[:, sl].astype(bf16)
        kh = k_ref[:, sl] * (DH ** -0.5)
        vh = v_ref[:, sl].astype(bf16)
        log_d = fc - fr + ir
        m_t = jnp.maximum(fc + m_prev, jnp.max(jnp.where(causal, log_d, NEG_INF), axis=1, keepdims=True))
        w = jnp.where(causal, jnp.exp(log_d - m_t), 0.0) * _dot_nt(qh, kh.astype(bf16))
        inter = jnp.exp(fc + m_prev - m_t)
        c_old = c_s[h]
        n_old = n_s[h:h + 1, :]
        num = (jnp.dot(w.astype(bf16), vh, preferred_element_type=f32)
               + inter * jnp.dot(qh, c_old.astype(bf16), preferred_element_type=f32))
        den = (jnp.sum(w, axis=1, keepdims=True)
               + inter * jnp.sum(q_ref[:, sl] * n_old, axis=1, keepdims=True))
        h_ref[:, sl] = num / jnp.maximum(jnp.abs(den), jnp.exp(-m_t))
        f_tot = fc[L - 1:L, :]
        m_new = m_t[L - 1:L, :]
        kw = kh * jnp.exp(f_tot - fc + ic - m_new)
        decay = jnp.exp(f_tot + m_prev - m_new)
        c_new = decay * c_old + _dot_t0(kw.astype(bf16), vh)
        n_new = decay * n_old + jnp.sum(kw, axis=0, keepdims=True)
        c_s[h] = c_new
        n_s[h:h + 1, :] = n_new
        m_s[h:h + 1, :] = jnp.broadcast_to(m_new, (1, DH))
        c_out[h] = c_new
    n_out[...] = n_s[...]
    m_out[...] = m_s[...]


def _mlstm_prompt(mq, mk, mv, gates_c, gates_r, chunk):
    t = mq.shape[0]
    NH, DH = N_MLSTM_HEADS, MLSTM_HEAD_DIM
    tok = lambda w: pl.BlockSpec((chunk, w), lambda c: (c, 0))
    return pl.pallas_call(
        _mlstm_chunk_kernel,
        grid=(t // chunk,),
        in_specs=[tok(D_MLSTM), tok(D_MLSTM), tok(D_MLSTM), tok(2 * NH),
                  pl.BlockSpec((2 * NH, chunk), lambda c: (0, c))],
        out_specs=[tok(D_MLSTM),
                   pl.BlockSpec((NH, DH, DH), lambda c: (0, 0, 0)),
                   pl.BlockSpec((8, DH), lambda c: (0, 0)),
                   pl.BlockSpec((8, DH), lambda c: (0, 0))],
        out_shape=[jax.ShapeDtypeStruct((t, D_MLSTM), f32),
                   jax.ShapeDtypeStruct((NH, DH, DH), f32),
                   jax.ShapeDtypeStruct((8, DH), f32),
                   jax.ShapeDtypeStruct((8, DH), f32)],
        scratch_shapes=[pltpu.VMEM((NH, DH, DH), f32), pltpu.VMEM((8, DH), f32), pltpu.VMEM((8, DH), f32)],
        compiler_params=_cparams("arbitrary"),
        name="mlstm_prompt",
    )(mq, mk, mv, gates_c, gates_r)


def _mlstm_step_kernel(q_ref, k_ref, v_ref, g_ref, c_ref, n_ref, m_ref, h_ref, c_out, n_out, m_out):
    NH, DH = N_MLSTM_HEADS, MLSTM_HEAD_DIM
    b = pl.program_id(0)
    row8 = lax.broadcasted_iota(jnp.int32, (8, DH), 0)
    g = g_ref[pl.ds(b, 1), :]
    m_row = m_ref[pl.ds(b, 1), :]
    m_new_row = jnp.zeros((1, NH), f32)
    lane4 = lax.broadcasted_iota(jnp.int32, (1, NH), 1)
    q_row, k_row, v_row = q_ref[pl.ds(b, 1), :], k_ref[pl.ds(b, 1), :], v_ref[pl.ds(b, 1), :]
    h_parts = []
    for h in range(NH):
        sl = slice(h * DH, (h + 1) * DH)
        q = q_row[:, sl]
        k = k_row[:, sl] * (DH ** -0.5)
        v = v_row[:, sl]
        ig = g[:, h:h + 1]
        lf = jax.nn.log_sigmoid(g[:, NH + h:NH + h + 1])
        m_prev = m_row[:, h:h + 1]
        c_old = c_ref[0, h]
        n_old = n_ref[0, h:h + 1, :]
        m_t = jnp.maximum(lf + m_prev, ig)
        w = jnp.exp(ig - m_t) * jnp.sum(q * k, axis=1, keepdims=True)
        inter = jnp.exp(lf + m_prev - m_t)
        q8 = jnp.where(row8 == 0, q, 0.0)
        qc = jnp.dot(q8, c_old, preferred_element_type=f32, precision=HIGHEST)[0:1]
        num = w * v + inter * qc
        den = w + inter * jnp.sum(q * n_old, axis=1, keepdims=True)
        h_parts.append(num / jnp.maximum(jnp.abs(den), jnp.exp(-m_t)))
        kw = k * jnp.exp(ig - m_t)
        decay = jnp.exp(lf + m_prev - m_t)
        kw8 = jnp.where(row8 == 0, kw, 0.0)
        v8 = jnp.where(row8 == 0, v, 0.0)
        c_out[0, h] = decay * c_old + _dot_t0(kw8, v8, precision=HIGHEST)
        n_out[0, h:h + 1, :] = decay * n_old + kw
        m_new_row = jnp.where(lane4 == h, m_t, m_new_row)
    h_ref[pl.ds(b, 1), :] = jnp.concatenate(h_parts, axis=1)
    m_out[pl.ds(b, 1), :] = m_new_row


def _mlstm_sample(mq, mk, mv, gates, c0, n0, m0):
    nb = mq.shape[0]
    NH, DH = N_MLSTM_HEADS, MLSTM_HEAD_DIM
    full = lambda a: pl.BlockSpec(a.shape, lambda b: (0,) * a.ndim)
    return pl.pallas_call(
        _mlstm_step_kernel,
        grid=(nb,),
        in_specs=[full(mq), full(mk), full(mv), full(gates),
                  pl.BlockSpec((1, NH, DH, DH), lambda b: (b, 0, 0, 0)),
                  pl.BlockSpec((1, NH, DH), lambda b: (b, 0, 0)),
                  full(m0)],
        out_specs=[pl.BlockSpec((nb, D_MLSTM), lambda b: (0, 0)),
                   pl.BlockSpec((1, NH, DH, DH), lambda b: (b, 0, 0, 0)),
                   pl.BlockSpec((1, NH, DH), lambda b: (b, 0, 0)),
                   pl.BlockSpec((nb, NH), lambda b: (0, 0))],
        out_shape=[jax.ShapeDtypeStruct((nb, D_MLSTM), f32),
                   jax.ShapeDtypeStruct((nb, NH, DH, DH), f32),
                   jax.ShapeDtypeStruct((nb, NH, DH), f32),
                   jax.ShapeDtypeStruct((nb, NH), f32)],
        compiler_params=_cparams("arbitrary"),
        name="mlstm_sample",
    )(mq, mk, mv, gates, c0, n0, m0)


SLC_TILE = 512


def _masked_softmax(s, mask):
    sm = jnp.where(mask, s, NEG_INF)
    mx = jnp.max(sm, axis=-1, keepdims=True)
    e = jnp.exp(sm - mx)
    p = e / jnp.sum(e, axis=-1, keepdims=True)
    return jnp.where(mx > 0.5 * NEG_INF, p, 0.0)


def _pool_matrix(n_cmp_rows, n_slc):
    i = np.arange(n_cmp_rows)[:, None]
    j = np.arange(n_slc)[None, :]
    ratio = SLC_BLOCK // CMP_STRIDE
    return jnp.asarray(((i >= ratio * j - 1) & (i <= ratio * j + ratio - 1)).astype(np.float32))


def _select_blocks(p_slc, cur, n_valid_lanes):
    nb = p_slc.shape[1]
    blk = lax.broadcasted_iota(jnp.int32, p_slc.shape, 1)
    valid = blk <= cur
    forced = (blk == 0) | (blk == cur) | (blk == cur - 1)
    score = jnp.where(valid, p_slc + jnp.where(forced, FORCE_BONUS, 0.0), -1.0)
    score = jnp.where(blk < n_valid_lanes, score, -2.0)
    blk_f = blk.astype(f32)
    sel = jnp.zeros(p_slc.shape, f32)
    picks = []
    for _ in range(N_SELECT):
        mx = jnp.max(score, axis=1, keepdims=True)
        idx = jnp.min(jnp.where(score == mx, blk_f, float(nb)), axis=1, keepdims=True)
        hit = blk_f == idx
        sel = jnp.where(hit, 1.0, sel)
        score = jnp.where(hit, -3.0, score)
        picks.append(idx.astype(jnp.int32))
    return sel, picks


_SUB = 128
_M_FLOOR = -1e20


def _with_pos_feature(k, index0=0):
    n = k.shape[0]
    pos = ((jnp.arange(n) + index0) % _SUB).astype(f32)[:, None]
    return jnp.concatenate([k, pos, jnp.zeros((n, _SUB - HEAD_DIM - 1), f32)], axis=1).astype(bf16)


def _colmax8(x):
    out = x[0:8]
    for r in range(8, x.shape[0], 8):
        out = jnp.maximum(out, x[r:r + 8])
    return out


def _colsum8(x):
    out = x[0:8]
    for r in range(8, x.shape[0], 8):
        out = out + x[r:r + 8]
    return out


_PSUM_PAD = 8
_CMP_UNROLL = 2


def _nsa_prompt_t_kernel(qT_ref, gT_ref, kc_ref, vcT_ref, ks_ref, vsT_ref, kw_ref, vwT_ref, o_ref,
                         sc_s, psum_s, sel_s, m_s, l_s, acc_s):
    tq = qT_ref.shape[1]
    L = GQA_REP * tq
    n_cmp = kc_ref.shape[1]
    n_slc = sel_s.shape[0]
    assert tq == _SUB and n_cmp * CMP_STRIDE == n_slc * SLC_BLOCK
    i = pl.program_id(0)
    start = pl.multiple_of(i * tq, tq)
    lane = lax.broadcasted_iota(jnp.int32, (1, L), 1)
    t_lane = lane % tq
    qpos = start + t_lane
    qpos_f = qpos.astype(f32)
    qpos_t = qpos[:, 0:tq]
    sub_l = lax.broadcasted_iota(jnp.int32, (_SUB, L), 0)
    sub_t = lax.broadcasted_iota(jnp.int32, (_SUB, tq), 0)
    feat_row = lax.broadcasted_iota(jnp.int32, (_SUB - HEAD_DIM, L), 0) == 0
    n_tiles = (start + tq + SLC_TILE - 1) // SLC_TILE
    nc_blocks = (start // CMP_STRIDE + (tq - CMP_BLOCK) // CMP_STRIDE) // _SUB + 1
    nc_trips = (nc_blocks + _CMP_UNROLL - 1) // _CMP_UNROLL
    assert (n_cmp // _SUB) % _CMP_UNROLL == 0
    blocks_per_tile = SLC_TILE // SLC_BLOCK

    for g in range(N_KV_HEADS):
        vrows = slice(g * HEAD_DIM, (g + 1) * HEAD_DIM)
        slope = jnp.exp2(-8.0 * (GQA_REP * g + 1 + lane // tq).astype(f32) / N_ATT_HEADS)
        q_rows = jnp.concatenate(
            [qT_ref[(GQA_REP * g + r) * HEAD_DIM:(GQA_REP * g + r + 1) * HEAD_DIM, :] for r in range(GQA_REP)],
            axis=1).astype(f32) * (HEAD_DIM ** -0.5)
        q_pos = jnp.concatenate([q_rows, jnp.where(feat_row, slope, 0.0)], axis=0).astype(bf16)
        q_cmp = jnp.concatenate([q_rows, jnp.where(feat_row, slope * CMP_STRIDE, 0.0)], axis=0).astype(bf16)

        def cmp_scores(ct, m8):
            r0s = [pl.multiple_of((ct * _CMP_UNROLL + u) * _SUB, _SUB) for u in range(_CMP_UNROLL)]
            dots = [jnp.dot(kc_ref[g, pl.ds(r0, _SUB), :], q_cmp, preferred_element_type=f32) for r0 in r0s]
            for r0, s in zip(r0s, dots):
                end0 = r0 * CMP_STRIDE + (CMP_BLOCK - 1)
                off = slope * (end0.astype(f32) - qpos_f)
                vis = (end0 + sub_l * CMP_STRIDE) <= qpos
                s = jnp.where(vis, s + off, NEG_INF)
                sc_s[pl.ds(r0, _SUB), :] = s
                m8 = jnp.maximum(m8, _colmax8(s))
            return m8

        m8 = lax.fori_loop(0, nc_trips, cmp_scores, jnp.full((8, L), NEG_INF, f32))
        m_c = jnp.maximum(jnp.max(m8, axis=0, keepdims=True), _M_FLOOR)

        def cmp_exp(ct, carry):
            l8, o_acc = carry
            for u in range(_CMP_UNROLL):
                r0 = pl.multiple_of((ct * _CMP_UNROLL + u) * _SUB, _SUB)
                e = jnp.exp(sc_s[pl.ds(r0, _SUB), :] - m_c)
                sc_s[pl.ds(r0, _SUB), :] = e
                o_acc = o_acc + jnp.dot(vcT_ref[vrows, pl.ds(r0, _SUB)], e.astype(bf16),
                                        preferred_element_type=f32)
                l8 = l8 + _colsum8(e)
            return l8, o_acc

        l8, o_c = lax.fori_loop(0, nc_trips, cmp_exp,
                                (jnp.zeros((8, L), f32), jnp.zeros((HEAD_DIM, L), f32)))
        l_c = jnp.sum(l8, axis=0, keepdims=True)
        inv_c = jnp.where(l_c > 0.0, 1.0 / jnp.where(l_c > 0.0, l_c, 1.0), 0.0)
        o_c = o_c * inv_c
        psum_s[...] = jnp.zeros_like(psum_s)

        def cmp_group_sum(ct, carry):
            for u in range(_CMP_UNROLL):
                r0 = pl.multiple_of((ct * _CMP_UNROLL + u) * _SUB, _SUB)
                p = sc_s[pl.ds(r0, _SUB), :] * inv_c
                psum_s[pl.ds(r0 + _PSUM_PAD, _SUB), :] = ((p[:, 0:tq] + p[:, tq:2 * tq])
                                                          + (p[:, 2 * tq:3 * tq] + p[:, 3 * tq:4 * tq]))
            return carry

        lax.fori_loop(0, nc_trips, cmp_group_sum, 0)
        ratio = SLC_BLOCK // CMP_STRIDE
        p_slc = psum_s[pl.ds(_PSUM_PAD - 1, n_slc, stride=ratio), :]
        for c in range(ratio):
            p_slc = p_slc + psum_s[pl.ds(_PSUM_PAD + c, n_slc, stride=ratio), :]

        blk = lax.broadcasted_iota(jnp.int32, (n_slc, tq), 0).astype(f32)
        cur = (qpos_t // SLC_BLOCK).astype(f32)
        forced = (blk == 0.0) | (blk == cur) | (blk == cur - 1.0)
        score = jnp.where(blk <= cur, p_slc + jnp.where(forced, FORCE_BONUS, 0.0), -1.0)
        sel = jnp.zeros((n_slc, tq), f32)
        for _ in range(N_SELECT):
            mx = jnp.max(score, axis=0, keepdims=True)
            idx = jnp.min(jnp.where(score == mx, blk, float(n_slc)), axis=0, keepdims=True)
            hit = blk == idx
            sel = jnp.where(hit, 1.0, sel)
            score = jnp.where(hit, -3.0, score)
        sel_s[...] = sel

        per_blk = _SUB // SLC_BLOCK
        m_s[...] = jnp.full(m_s.shape, NEG_INF, f32)
        l_s[...] = jnp.zeros(l_s.shape, f32)
        acc_s[...] = jnp.zeros(acc_s.shape, f32)
        lowest = jnp.min(jnp.where((sel > 0.5) & (blk >= 1.0), blk, float(n_slc))).astype(jnp.int32)

        def slc_tile(j, carry):
            k0 = pl.multiple_of(j * SLC_TILE, SLC_TILE)
            picked = sel_s[pl.ds(pl.multiple_of(j * blocks_per_tile, blocks_per_tile), blocks_per_tile), :]
            s = jnp.dot(ks_ref[g, pl.ds(k0, SLC_TILE), :], q_pos, preferred_element_type=f32)
            parts = []
            for w in range(SLC_TILE // _SUB):
                base = k0 + w * _SUB
                picked_k = jnp.concatenate(
                    [jnp.broadcast_to(picked[per_blk * w + u:per_blk * w + u + 1, :], (SLC_BLOCK, tq))
                     for u in range(per_blk)], axis=0)
                ok = (picked_k > 0.5) & (base + sub_t <= qpos_t)
                bias = jnp.where(ok, 0.0, NEG_INF)
                bias = jnp.concatenate([bias] * GQA_REP, axis=1) + slope * (base.astype(f32) - qpos_f)
                parts.append(s[w * _SUB:(w + 1) * _SUB] + bias)
            s = jnp.concatenate(parts, axis=0)
            m_old = m_s[...]
            m_new = jnp.maximum(jnp.maximum(m_old, jnp.max(_colmax8(s), axis=0, keepdims=True)), _M_FLOOR)
            p = jnp.exp(s - m_new)
            alpha = jnp.exp(m_old - m_new)
            l_s[...] = alpha * l_s[...] + jnp.sum(_colsum8(p), axis=0, keepdims=True)
            acc_s[...] = alpha * acc_s[...] + jnp.dot(vsT_ref[vrows, pl.ds(k0, SLC_TILE)], p.astype(bf16),
                                                      preferred_element_type=f32)
            m_s[...] = m_new
            return carry

        slc_tile(jnp.int32(0), 0)
        lax.fori_loop(jnp.maximum(lowest // blocks_per_tile, 1), n_tiles, slc_tile, 0)
        l_fin = l_s[...]
        o_s = acc_s[...] * jnp.where(l_fin > 0.0, 1.0 / jnp.where(l_fin > 0.0, l_fin, 1.0), 0.0)

        n_win_sub = (WINDOW + tq) // _SUB
        s_parts = []
        for w in range(n_win_sub):
            base = start - WINDOW + w * _SUB
            s = jnp.dot(kw_ref[g, pl.ds(start + w * _SUB, _SUB), :], q_pos, preferred_element_type=f32)
            if w == 0:
                ok = sub_l > t_lane
            elif w == n_win_sub - 1:
                ok = sub_l <= t_lane
            else:
                ok = None
            bias = slope * (base.astype(f32) - qpos_f) + jnp.where(base >= 0, 0.0, NEG_INF)
            s = s + bias
            s_parts.append(s if ok is None else jnp.where(ok, s, NEG_INF))
        m8 = _colmax8(s_parts[0])
        for s in s_parts[1:]:
            m8 = jnp.maximum(m8, _colmax8(s))
        m_w = jnp.maximum(jnp.max(m8, axis=0, keepdims=True), _M_FLOOR)
        l8 = jnp.zeros((8, L), f32)
        o_w = jnp.zeros((HEAD_DIM, L), f32)
        for w, s in enumerate(s_parts):
            e = jnp.exp(s - m_w)
            l8 = l8 + _colsum8(e)
            o_w = o_w + jnp.dot(vwT_ref[vrows, pl.ds(start + w * _SUB, _SUB)], e.astype(bf16),
                                preferred_element_type=f32)
        l_w = jnp.sum(l8, axis=0, keepdims=True)
        o_w = o_w * jnp.where(l_w > 0.0, 1.0 / jnp.where(l_w > 0.0, l_w, 1.0), 0.0)

        for r in range(GQA_REP):
            head = GQA_REP * g + r
            cols = slice(r * tq, (r + 1) * tq)
            o_ref[head * HEAD_DIM:(head + 1) * HEAD_DIM, :] = (
                gT_ref[3 * head:3 * head + 1, :] * o_c[:, cols]
                + gT_ref[3 * head + 1:3 * head + 2, :] * o_s[:, cols]
                + gT_ref[3 * head + 2:3 * head + 3, :] * o_w[:, cols])


def _nsa_prompt_t(q, small, kv_cmp, kvs, kvw):
    t = q.shape[0]
    n_cmp = kv_cmp.shape[0]
    n_slc = t // SLC_BLOCK
    kcol = lambda g: slice(g * HEAD_DIM, (g + 1) * HEAD_DIM)
    vT = lambda a: a[:, N_KV_HEADS * HEAD_DIM:].T.astype(bf16)
    qT = q.T.astype(bf16)
    gT = small[:, :32].T
    kc = jnp.stack([_with_pos_feature(kv_cmp[:, kcol(g)]) for g in range(N_KV_HEADS)])
    ks = jnp.stack([_with_pos_feature(kvs[:, kcol(g)]) for g in range(N_KV_HEADS)])
    kvw_pad = jnp.pad(kvw, ((WINDOW, 0), (0, 0)))
    kw = jnp.stack([_with_pos_feature(kvw_pad[:, kcol(g)]) for g in range(N_KV_HEADS)])
    operands = (qT, gT, kc, vT(kv_cmp), ks, vT(kvs), kw, vT(kvw_pad))
    L = GQA_REP * Q_BLOCK
    const = lambda a: pl.BlockSpec(a.shape, lambda i: (0,) * a.ndim, pipeline_mode=pl.Buffered(1))
    return pl.pallas_call(
        _nsa_prompt_t_kernel,
        grid=(t // Q_BLOCK,),
        in_specs=[pl.BlockSpec((D_ATT, Q_BLOCK), lambda i: (0, i)),
                  pl.BlockSpec((32, Q_BLOCK), lambda i: (0, i))] + [const(a) for a in operands[2:]],
        out_specs=pl.BlockSpec((D_ATT, Q_BLOCK), lambda i: (0, i)),
        out_shape=jax.ShapeDtypeStruct((D_ATT, t), f32),
        scratch_shapes=[pltpu.VMEM((n_cmp, L), f32), pltpu.VMEM((n_cmp + _PSUM_PAD, Q_BLOCK), f32),
                        pltpu.VMEM((n_slc, Q_BLOCK), f32), pltpu.VMEM((1, L), f32), pltpu.VMEM((1, L), f32),
                        pltpu.VMEM((HEAD_DIM, L), f32)],
        compiler_params=_cparams("parallel"),
        name="nsa_prompt",
    )(*operands)


_QROWS = 16
_IDS_LANES = 128


def _group_queries_1(q, g):
    rows = [q[:, (GQA_REP * g + r) * HEAD_DIM:(GQA_REP * g + r + 1) * HEAD_DIM] for r in range(GQA_REP)]
    rows.append(jnp.zeros((_QROWS - GQA_REP, HEAD_DIM), f32))
    return jnp.concatenate(rows, axis=0) * (HEAD_DIM ** -0.5)


def _group_slopes_1(g):
    r = jnp.minimum(lax.broadcasted_iota(jnp.int32, (_QROWS, 1), 0), GQA_REP - 1)
    return jnp.exp2(-8.0 * (GQA_REP * g + 1 + r).astype(f32) / N_ATT_HEADS)


def _nsa_sample_cmp_kernel(q_ref, cmp_ref, pool_ref, oc_ref, ids_ref, *, past):
    b = pl.program_id(0)
    n_cmp = cmp_ref.shape[1]
    n_old = pool_ref.shape[1]
    q = q_ref[pl.ds(b, 1), :]
    cmpb = cmp_ref[0].astype(bf16)
    d_c = past - (lax.broadcasted_iota(jnp.int32, (1, n_cmp), 1) * CMP_STRIDE + (CMP_BLOCK - 1))
    lane = lax.broadcasted_iota(jnp.int32, (1, _IDS_LANES), 1)
    cur = jnp.full((1, 1), past // SLC_BLOCK, jnp.int32)
    oc_parts = []
    for g in range(N_KV_HEADS):
        q16 = _group_queries_1(q, g).astype(bf16)
        s = _dot_nt(q16, cmpb[:, g * HEAD_DIM:(g + 1) * HEAD_DIM]) - _group_slopes_1(g) * d_c.astype(f32)
        p = _masked_softmax(s, jnp.broadcast_to(d_c >= 0, s.shape))
        o_c = jnp.dot(p.astype(bf16), cmpb[:, (N_KV_HEADS + g) * HEAD_DIM:(N_KV_HEADS + g + 1) * HEAD_DIM],
                      preferred_element_type=f32)
        oc_parts.extend(o_c[r:r + 1] for r in range(GQA_REP))
        p_sum = jnp.sum(p[0:GQA_REP], axis=0, keepdims=True)
        p_slc = jnp.dot(jnp.broadcast_to(p_sum, (8, n_cmp)), pool_ref[...], preferred_element_type=f32,
                        precision=HIGHEST)[0:1]
        p_ext = jnp.concatenate([p_slc, jnp.zeros((1, 128), f32)], axis=1)
        _, picks = _select_blocks(p_ext, cur, past // SLC_BLOCK + 1)
        row = jnp.zeros((1, _IDS_LANES), jnp.int32)
        for k, pick in enumerate(picks):
            row = jnp.where(lane == k, pick, row)
        ids_ref[0, g:g + 1, :] = row
    oc_ref[pl.ds(b, 1), :] = jnp.concatenate(oc_parts, axis=1)


def _nsa_sample_cmp(q, cmp_s, pool, past):
    nb = q.shape[0]
    return pl.pallas_call(
        functools.partial(_nsa_sample_cmp_kernel, past=past),
        grid=(nb,),
        in_specs=[pl.BlockSpec(q.shape, lambda b: (0, 0)),
                  pl.BlockSpec((1,) + cmp_s.shape[1:], lambda b: (b, 0, 0)),
                  pl.BlockSpec(pool.shape, lambda b: (0, 0))],
        out_specs=[pl.BlockSpec((nb, D_ATT), lambda b: (0, 0)),
                   pl.BlockSpec((1, N_KV_HEADS, _IDS_LANES), lambda b: (b, 0, 0))],
        out_shape=[jax.ShapeDtypeStruct((nb, D_ATT), f32),
                   jax.ShapeDtypeStruct((nb, N_KV_HEADS, _IDS_LANES), jnp.int32)],
        compiler_params=_cparams("arbitrary"),
        name="nsa_sample_cmp",
    )(q, cmp_s, pool)


def _softmax_with_new(s, mask, s_new, new_ok):
    s = jnp.where(mask, s, NEG_INF)
    s_new = jnp.where(new_ok, s_new, NEG_INF)
    m = jnp.maximum(jnp.max(s, axis=1, keepdims=True), s_new)
    p = jnp.where(mask, jnp.exp(s - m), 0.0)
    p_new = jnp.where(new_ok, jnp.exp(s_new - m), 0.0)
    l = jnp.sum(p, axis=1, keepdims=True) + p_new
    inv = jnp.where(l > 0.0, 1.0 / jnp.where(l > 0.0, l, 1.0), 0.0)
    return p * inv, p_new * inv


def _nsa_sample_sel_kernel(ids_ref, pt_ref, pool_ref, q_ref, sm_ref, kvs_ref, kvw_ref, win_ref, oc_ref, o_ref,
                           buf, sem, *, past):
    b = pl.program_id(0)
    n_pages = past // PAGE_ROWS
    last_blk = past // SLC_BLOCK
    blocks_per_page = PAGE_ROWS // SLC_BLOCK

    def page_copy(g, k, c):
        blk = jnp.minimum(ids_ref[(b * N_KV_HEADS + g) * N_SELECT + k], last_blk - 1)
        page = pt_ref[b * n_pages + blk // blocks_per_page]
        return pltpu.make_async_copy(pool_ref.at[page, c, g], buf.at[g, c, k], sem)

    copies = [(g, k, c) for g in range(N_KV_HEADS) for k in range(N_SELECT) for c in range(2)]
    for g, k, c in copies:
        page_copy(g, k, c).start()
    for g, k, c in copies:
        page_copy(g, k, c).wait()

    q = q_ref[pl.ds(b, 1), :]
    gates = sm_ref[pl.ds(b, 1), :]
    new_s = kvs_ref[pl.ds(b, 1), :]
    new_w = kvw_ref[pl.ds(b, 1), :]
    n_sel = N_SELECT * PAGE_ROWS
    lane = lax.broadcasted_iota(jnp.int32, (1, n_sel), 1)
    w_buf = win_ref.shape[1]
    d_w = w_buf - lax.broadcasted_iota(jnp.int32, (1, w_buf), 1)
    mask_w = jnp.broadcast_to((d_w < WINDOW) & (past - d_w >= 0), (_QROWS, w_buf))
    always = jnp.full((_QROWS, 1), True)
    o_c = oc_ref[pl.ds(b, 1), :]
    out_parts = []
    for g in range(N_KV_HEADS):
        kcol = slice(g * HEAD_DIM, (g + 1) * HEAD_DIM)
        vcol = slice((N_KV_HEADS + g) * HEAD_DIM, (N_KV_HEADS + g + 1) * HEAD_DIM)
        q16f = _group_queries_1(q, g)
        q16 = q16f.astype(bf16)
        slope = _group_slopes_1(g)

        blk_vec = jnp.zeros((1, n_sel), jnp.int32)
        for k in range(N_SELECT):
            blk_vec = jnp.where(lane // PAGE_ROWS == k, ids_ref[(b * N_KV_HEADS + g) * N_SELECT + k], blk_vec)
        kpos = (blk_vec // blocks_per_page) * PAGE_ROWS + lane % PAGE_ROWS
        d_s = past - kpos
        mask_s = jnp.broadcast_to((d_s >= 0) & (blk_vec < last_blk) & (kpos // SLC_BLOCK == blk_vec),
                                  (_QROWS, n_sel))
        has_new = jnp.max(jnp.where(blk_vec == last_blk, 1, 0), axis=1, keepdims=True) > 0
        kt = jnp.concatenate([buf[g, 0, k] for k in range(N_SELECT)], axis=1).astype(bf16)
        vt = jnp.concatenate([buf[g, 1, k] for k in range(N_SELECT)], axis=1).astype(bf16)
        s = jnp.dot(q16, kt, preferred_element_type=f32) - slope * d_s.astype(f32)
        s_new = jnp.sum(q16f * new_s[:, kcol], axis=1, keepdims=True)
        p, p_new = _softmax_with_new(s, mask_s, s_new, jnp.broadcast_to(has_new, (_QROWS, 1)))
        o_s = _dot_nt(p.astype(bf16), vt) + p_new * new_s[:, vcol]

        wb = win_ref[0].astype(bf16)
        s = _dot_nt(q16, wb[:, kcol]) - slope * d_w.astype(f32)
        s_new = jnp.sum(q16f * new_w[:, kcol], axis=1, keepdims=True)
        p, p_new = _softmax_with_new(s, mask_w, s_new, always)
        o_w = jnp.dot(p.astype(bf16), wb[:, vcol], preferred_element_type=f32) + p_new * new_w[:, vcol]

        for r in range(GQA_REP):
            head = GQA_REP * g + r
            hs = slice(head * HEAD_DIM, (head + 1) * HEAD_DIM)
            out_parts.append(gates[:, 3 * head:3 * head + 1] * o_c[:, hs]
                             + gates[:, 3 * head + 1:3 * head + 2] * o_s[r:r + 1]
                             + gates[:, 3 * head + 2:3 * head + 3] * o_w[r:r + 1])
    o_ref[pl.ds(b, 1), :] = jnp.concatenate(out_parts, axis=1)


def _nsa_sample_sel(ids, page_table, pool_slc, q, small, kvs_new, kvw_new, win_buf, o_c, past):
    nb = q.shape[0]
    full = lambda a: pl.BlockSpec(a.shape, lambda b, ids, pt: (0,) * a.ndim)
    return pl.pallas_call(
        functools.partial(_nsa_sample_sel_kernel, past=past),
        grid_spec=pltpu.PrefetchScalarGridSpec(
            num_scalar_prefetch=2,
            grid=(nb,),
            in_specs=[pl.BlockSpec(memory_space=pl.ANY), full(q), full(small), full(kvs_new), full(kvw_new),
                      pl.BlockSpec((1,) + win_buf.shape[1:], lambda b, ids, pt: (b, 0, 0)), full(o_c)],
            out_specs=pl.BlockSpec((nb, D_ATT), lambda b, ids, pt: (0, 0)),
            scratch_shapes=[pltpu.VMEM((N_KV_HEADS, 2, N_SELECT, HEAD_DIM, PAGE_ROWS), f32),
                            pltpu.SemaphoreType.DMA(())]),
        out_shape=jax.ShapeDtypeStruct((nb, D_ATT), f32),
        compiler_params=_cparams("arbitrary"),
        name="nsa_sample_sel",
    )(ids, page_table.reshape(-1), pool_slc, q, small, kvs_new, kvw_new, win_buf, o_c)


def _mixout_kernel(att_ref, mh_ref, mo_ref, x_ref, ga_ref, gm_ref, wo_ref, nf_ref, wq_ref, h_ref, xn_ref,
                   qp_ref):
    parts = []
    for h in range(N_ATT_HEADS):
        sl = slice(h * HEAD_DIM, (h + 1) * HEAD_DIM)
        parts.append(_rms(att_ref[:, sl], ga_ref[:, sl]))
    for h in range(N_MLSTM_HEADS):
        sl = slice(h * MLSTM_HEAD_DIM, (h + 1) * MLSTM_HEAD_DIM)
        parts.append(mo_ref[:, sl] * _rms(mh_ref[:, sl], gm_ref[:, sl]))
    cat = jnp.concatenate(parts, axis=1).astype(bf16)
    h1 = x_ref[...] + jnp.dot(cat, wo_ref[...], preferred_element_type=f32)
    h_ref[...] = h1
    xn = _rms(h1, nf_ref[...]).astype(bf16)
    xn_ref[...] = xn
    qp_ref[...] = jnp.dot(xn, wq_ref[...], preferred_element_type=f32)


def _mix_output(att, mh, mo, x, g_att, g_ml, w_out_bf, norm_ffn, w_q_bf, tm):
    n = x.shape[0]
    tok = lambda w: pl.BlockSpec((tm, w), lambda i: (i, 0))
    full = lambda a: pl.BlockSpec(a.shape, lambda i: (0,) * a.ndim)
    return pl.pallas_call(
        _mixout_kernel,
        grid=(n // tm,),
        in_specs=[tok(D_ATT), tok(D_MLSTM), tok(D_MLSTM), tok(D_MODEL), full(g_att), full(g_ml),
                  full(w_out_bf), full(norm_ffn), full(w_q_bf)],
        out_specs=[tok(D_MODEL), tok(D_MODEL), tok(D_MODEL)],
        out_shape=[jax.ShapeDtypeStruct((n, D_MODEL), f32), jax.ShapeDtypeStruct((n, D_MODEL), bf16),
                   jax.ShapeDtypeStruct((n, D_MODEL), f32)],
        compiler_params=_cparams("parallel"),
        name="mix_output",
    )(att, mh, mo, x, g_att, g_ml, w_out_bf, norm_ffn, w_q_bf)


def _topk_rows(s, k):
    n = s.shape[0]
    rows = lax.broadcasted_iota(jnp.int32, s.shape, 0)
    vals, idxs = [], []
    for _ in range(k):
        mx = jnp.max(s, axis=0, keepdims=True)
        idx = jnp.min(jnp.where(s == mx, rows, n), axis=0, keepdims=True)
        vals.append(mx)
        idxs.append(idx)
        s = jnp.where(rows == idx, NEG_INF, s)
    return jnp.concatenate(vals, axis=0), jnp.concatenate(idxs, axis=0)


def _peer_topk_kernel(qp_ref, sub_ref, ei_ref, ej_ref, g_ref):
    K = PEER_TOPK
    half = PEER_D_KEY // 2
    ei, ej, gg = [], [], []
    for h in range(PEER_HEADS):
        sv, si = [], []
        for c in range(2):
            qhc = qp_ref[:, (2 * h + c) * half:(2 * h + c + 1) * half]
            s = _dot_nt(sub_ref[h, c], qhc, precision=HIGHEST)
            v, i = _topk_rows(s, K)
            sv.append(v)
            si.append(i)
        n_t = sv[0].shape[1]
        counts = [K // (a + 1) for a in range(K)]
        n_cand = -(-sum(counts) // 8) * 8
        pad = n_cand - sum(counts)
        cand = jnp.concatenate([sv[0][a:a + 1] + sv[1][0:counts[a]] for a in range(K)]
                               + [jnp.full((pad, n_t), NEG_INF, f32)], axis=0)
        pos_i = jnp.concatenate([jnp.broadcast_to(si[0][a:a + 1], (counts[a], n_t)) for a in range(K)]
                                + [jnp.zeros((pad, n_t), jnp.int32)], axis=0)
        pos_j = jnp.concatenate([si[1][0:counts[a]] for a in range(K)] + [jnp.zeros((pad, n_t), jnp.int32)],
                                axis=0)
        rows = lax.broadcasted_iota(jnp.int32, cand.shape, 0)
        best, bi, bj = [], [], []
        for _ in range(K):
            mx = jnp.max(cand, axis=0, keepdims=True)
            pos = jnp.min(jnp.where(cand == mx, rows, n_cand), axis=0, keepdims=True)
            hit = rows == pos
            best.append(mx)
            bi.append(jnp.max(jnp.where(hit, pos_i, -1), axis=0, keepdims=True))
            bj.append(jnp.max(jnp.where(hit, pos_j, -1), axis=0, keepdims=True))
            cand = jnp.where(hit, NEG_INF, cand)
        best = jnp.concatenate(best, axis=0)
        e = jnp.exp(best - best[0:1])
        gg.append(e / jnp.sum(e, axis=0, keepdims=True))
        ei.append(jnp.concatenate(bi, axis=0))
        ej.append(jnp.concatenate(bj, axis=0))
    ei_ref[...] = jnp.concatenate(ei, axis=0).astype(f32).T
    ej_ref[...] = jnp.concatenate(ej, axis=0).astype(f32).T
    g_ref[...] = jnp.concatenate(gg, axis=0).T


def _peer_topk(qp, sub_keys, tm):
    n = qp.shape[0]
    hk = PEER_HEADS * PEER_TOPK
    return pl.pallas_call(
        _peer_topk_kernel,
        grid=(n // tm,),
        in_specs=[pl.BlockSpec((tm, D_MODEL), lambda i: (i, 0)),
                  pl.BlockSpec(sub_keys.shape, lambda i: (0, 0, 0, 0))],
        out_specs=[pl.BlockSpec((tm, hk), lambda i: (i, 0))] * 3,
        out_shape=[jax.ShapeDtypeStruct((n, hk), f32)] * 3,
        compiler_params=_cparams("parallel"),
        name="peer_topk",
    )(qp, sub_keys)


_GATE_UNROLL = 32
_GATE_PARTS = 2


def _peer_dense_kernel(ei_ref, ej_ref, gg_ref, x_ref, u_ref, v_ref, h_ref, nf_ref, o_ref, acc_ref, g_s):
    part, e = pl.program_id(1), pl.program_id(2)
    tm = x_ref.shape[0]
    et = u_ref.shape[0]
    nk = PEER_N_KEYS
    part_rows = nk // _GATE_PARTS

    @pl.when((part == 0) & (e == 0))
    def _():
        acc_ref[...] = jnp.zeros_like(acc_ref)

    @pl.when(e == 0)
    def _():
        i0 = (part * part_rows).astype(f32)
        sub_i = lax.broadcasted_iota(jnp.int32, (part_rows, ei_ref.shape[1]), 0).astype(f32) + i0
        sub_j = lax.broadcasted_iota(jnp.int32, (nk, ei_ref.shape[1]), 0).astype(f32)

        zero = jnp.zeros((part_rows, ei_ref.shape[1]), bf16)

        def body(tp, carry):
            t0 = pl.multiple_of(tp * 2, 2)
            ei, ej, gg = ei_ref[pl.ds(t0, 2), :], ej_ref[pl.ds(t0, 2), :], gg_ref[pl.ds(t0, 2), :]
            a = [jnp.where(sub_i == ei[u:u + 1], 1.0, 0.0).astype(bf16) for u in range(2)]
            b = [jnp.where(sub_j == ej[u:u + 1], gg[u:u + 1], 0.0).astype(bf16) for u in range(2)]
            lhs = jnp.concatenate([jnp.concatenate([a[0], zero], axis=1), jnp.concatenate([zero, a[1]], axis=1)],
                                  axis=0)
            g_s[pl.ds(pl.multiple_of(t0 * part_rows, 2 * part_rows), 2 * part_rows), :] = _dot_nt(
                lhs, jnp.concatenate(b, axis=1))
            return carry

        lax.fori_loop(0, tm // 2, body, 0, unroll=_GATE_UNROLL)

    rows = et // nk
    g = jnp.concatenate([g_s[pl.ds(e * rows + r, tm, stride=part_rows), :] for r in range(rows)], axis=1)
    act = jax.nn.gelu(_dot_nt(x_ref[...], u_ref[...]))
    acc_ref[...] += jnp.dot((g * act).astype(bf16), v_ref[...], preferred_element_type=f32)

    @pl.when((part == _GATE_PARTS - 1) & (e == pl.num_programs(2) - 1))
    def _():
        o_ref[...] = _rms(h_ref[...] + acc_ref[...], nf_ref[...])


def _peer_dense(ei, ej, gg, xn_bf, u_bf, v_bf, h1, norm_final, tm, et):
    n = xn_bf.shape[0]
    n_exp = u_bf.shape[0]
    hk = ei.shape[1]
    tok = lambda w: pl.BlockSpec((tm, w), lambda i, p, e: (i, 0))
    steps = n_exp // _GATE_PARTS // et
    expert_tile = pl.BlockSpec((et, D_MODEL), lambda i, p, e: (p * steps + e, 0))
    return pl.pallas_call(
        _peer_dense_kernel,
        grid=(n // tm, _GATE_PARTS, steps),
        in_specs=[tok(hk), tok(hk), tok(hk), tok(D_MODEL), expert_tile, expert_tile, tok(D_MODEL),
                  pl.BlockSpec((1, D_MODEL), lambda i, p, e: (0, 0))],
        out_specs=tok(D_MODEL),
        out_shape=jax.ShapeDtypeStruct((n, D_MODEL), f32),
        scratch_shapes=[pltpu.VMEM((tm, D_MODEL), f32),
                        pltpu.VMEM((tm * PEER_N_KEYS // _GATE_PARTS, PEER_N_KEYS), f32)],
        compiler_params=_cparams("parallel", "arbitrary", "arbitrary"),
        name="peer_dense",
    )(ei, ej, gg, xn_bf, u_bf, v_bf, h1, norm_final)


def _channel_mix_and_norm(h1, xn_bf, qp, sub_keys, u_bf, v_bf, norm_final, tm_topk, tm, et):
    ei, ej, gg = _peer_topk(qp, sub_keys, tm_topk)
    return _peer_dense(ei, ej, gg, xn_bf, u_bf, v_bf, h1, norm_final, tm, et)


PROMPT_TM = 512
PEER_TM = 512
PEER_ET = 1024
MLSTM_CHUNK = 256
SAMPLE_PAD = 128


def kernel(x_prompt, x_sample, cache_cmp_kv, cache_slc_kv, cache_win_kv, state_mlstm_C, state_mlstm_n,
           state_mlstm_m, page_table, norm_mix, w_in, b_igate, b_fgate, pe_cmp, w_cmp1, w_cmp2, norm_att_out,
           norm_mlstm_out, w_out, norm_ffn, peer_wq, peer_subkeys, peer_u, peer_v, norm_final):
    assert x_prompt.shape[0] == 1 and x_sample.shape[1] == 1 and w_in.shape[0] == 1
    _, t, d = x_prompt.shape
    nb = x_sample.shape[0]
    n_pool = cache_cmp_kv.shape[1]
    past = page_table.shape[1] * PAGE_ROWS
    w_buf = cache_win_kv.shape[2]
    row = (2, N_KV_HEADS, HEAD_DIM)
    NH, DH = N_MLSTM_HEADS, MLSTM_HEAD_DIM
    g0 = N_GATE_COLS

    w_re = _relayout_w_in(w_in[0])
    bias = _bias_row(b_igate[0], b_fgate[0])
    cw = _compress_weights(pe_cmp[0], w_cmp1[0], w_cmp2[0])
    nm = norm_mix[0][None]
    ga, gm, nf, nfin = norm_att_out[0][None], norm_mlstm_out[0][None], norm_ffn[0][None], norm_final[None]
    w_out_bf, w_q_bf = w_out[0].astype(bf16), peer_wq[0].astype(bf16)
    u_bf, v_bf = peer_u[0].astype(bf16), peer_v[0].astype(bf16)
    sub_keys = peer_subkeys[0]

    xp = x_prompt.reshape(t, d)
    q, kvc, kvs, kvw, mq, mk, mv, mo, sm = _project(xp, nm, w_re, bias, PROMPT_TM)
    cmp_p = _compress_prompt(kvc[None], cw)[0]
    att = _nsa_prompt_t(q, sm, cmp_p, kvs, kvw).T
    gates_c = sm[:, g0:g0 + 2 * NH]
    mh, c_p, n_p, m_p = _mlstm_prompt(mq, mk, mv, gates_c, gates_c.T, MLSTM_CHUNK)
    h1, xn_bf, qp = _mix_output(att, mh, mo, xp, ga, gm, w_out_bf, nf, w_q_bf, 256)
    y_p = _channel_mix_and_norm(h1, xn_bf, qp, sub_keys, u_bf, v_bf, nfin, 256, PEER_TM, PEER_ET)
    w_keep = min(WINDOW, t)
    outs_p = (kvc.reshape((1, 1, t) + row), kvs.reshape((1, 1, t) + row),
              kvw[t - w_keep:].reshape((1, 1, w_keep) + row),
              c_p[None, None], n_p[:NH][None, None], m_p[:NH, 0][None, None])

    xs = x_sample.reshape(nb, d)
    q, kvc_s, kvs_s, kvw_s, mq, mk, mv, mo, sm = _project(xs, nm, w_re, bias, nb)
    cmp_s = _compress_paged(page_table, jnp.transpose(cache_cmp_kv[0], (0, 2, 3, 4, 1)), cw)
    pool_s = _pool_matrix(past // CMP_STRIDE, past // SLC_BLOCK)
    o_c, ids = _nsa_sample_cmp(q, cmp_s, pool_s, past)
    win_buf = cache_win_kv[0].reshape(nb, w_buf, KV_COLS)
    att = _nsa_sample_sel(ids[:, :, :N_SELECT].reshape(-1), page_table,
                          jnp.transpose(cache_slc_kv[0], (0, 2, 3, 4, 1)), q, sm, kvs_s, kvw_s, win_buf, o_c, past)
    mh, c_s, n_s, m_s = _mlstm_sample(mq, mk, mv, sm[:, g0:g0 + 2 * NH], state_mlstm_C[0], state_mlstm_n[0],
                                      state_mlstm_m[0])
    h1, xn_bf, qp = _mix_output(att, mh, mo, xs, ga, gm, w_out_bf, nf, w_q_bf, nb)
    padr = lambda a: jnp.pad(a, ((0, SAMPLE_PAD - nb), (0, 0)))
    y_s = _channel_mix_and_norm(padr(h1), padr(xn_bf), padr(qp), sub_keys, u_bf, v_bf, nfin, SAMPLE_PAD,
                                SAMPLE_PAD, PEER_ET)[:nb]
    win_all = jnp.concatenate([win_buf, kvw_s[:, None, :]], axis=1)
    w_keep_s = min(WINDOW, w_buf + 1)
    outs_s = (kvc_s.reshape((1, nb, 1) + row), kvs_s.reshape((1, nb, 1) + row),
              win_all[:, w_buf + 1 - w_keep_s:].reshape((1, nb, w_keep_s) + row),
              c_s[None], n_s[None], m_s[None])

    return (y_p.reshape(1, t, d), y_s.reshape(nb, 1, d),
            outs_p[0], outs_s[0], outs_p[1], outs_s[1], outs_p[2], outs_s[2],
            outs_p[3], outs_s[3], outs_p[4], outs_s[4], outs_p[5], outs_s[5])
```

```python
import functools

import jax
import jax.numpy as jnp
import numpy as np
from jax import lax
from jax.experimental import pallas as pl
from jax.experimental.pallas import tpu as pltpu

f32 = jnp.float32
bf16 = jnp.bfloat16
HIGHEST = lax.Precision.HIGHEST

D_MODEL = 1024
HEAD_DIM = 64
N_ATT_HEADS = 8
N_KV_HEADS = 2
GQA_REP = 4
D_ATT = 512
KV_COLS = 256
CMP_BLOCK = 32
CMP_STRIDE = 16
CMP_HIDDEN = 128
SLC_BLOCK = 64
N_SELECT = 16
WINDOW = 512
Q_BLOCK = 128
FORCE_BONUS = 1000.0
N_MLSTM_HEADS = 4
MLSTM_HEAD_DIM = 128
D_MLSTM = 512
N_GATE_COLS = 3 * N_ATT_HEADS
SPLIT_SIZES = (D_ATT, KV_COLS, KV_COLS, KV_COLS, N_GATE_COLS, D_MLSTM, D_MLSTM, D_MLSTM, D_MLSTM,
               N_MLSTM_HEADS, N_MLSTM_HEADS)
SMALL_COLS = 128
PEER_N_KEYS = 128
PEER_HEADS = 8
PEER_TOPK = 16
PEER_D_KEY = 128
NORM_EPS = 1e-6
NEG_INF = -1e30
VMEM_LIMIT = 56 * 1024 * 1024


def _cparams(*sem):
    return pltpu.CompilerParams(dimension_semantics=sem, vmem_limit_bytes=VMEM_LIMIT)


def _rms(x, w):
    return x * lax.rsqrt(jnp.mean(x * x, axis=-1, keepdims=True) + NORM_EPS) * w


_P_Q, _P_KVC, _P_KVS, _P_KVW, _P_MQ, _P_MK, _P_MV, _P_MO, _P_SM, _P_END = (
    0, 512, 768, 1024, 1280, 1792, 2304, 2816, 3328, 3456)


def _relayout_w_in(w_in):
    pts = np.cumsum(SPLIT_SIZES)[:-1].tolist()
    q, kc, ks, kw, ga, mq, mk, mv, mo, ig, fg = jnp.split(w_in, pts, axis=-1)
    pad = jnp.zeros((w_in.shape[0], SMALL_COLS - N_GATE_COLS - 2 * N_MLSTM_HEADS), w_in.dtype)
    return jnp.concatenate([q, kc, ks, kw, mq, mk, mv, mo, ga, ig, fg, pad], axis=-1).astype(bf16)


def _bias_row(b_i, b_f):
    pad = jnp.zeros((SMALL_COLS - N_GATE_COLS - 2 * N_MLSTM_HEADS,), f32)
    return jnp.concatenate([jnp.zeros((N_GATE_COLS,), f32), b_i, b_f, pad])[None, :]


def _proj_kernel(x_ref, nw_ref, w_ref, b_ref, q_ref, kvc_ref, kvs_ref, kvw_ref, mq_ref, mk_ref, mv_ref,
                 mo_ref, sm_ref):
    xb = _rms(x_ref[...], nw_ref[...]).astype(bf16)

    def mm(lo, hi):
        return jnp.dot(xb, w_ref[:, lo:hi], preferred_element_type=f32)

    q_ref[...] = mm(_P_Q, _P_KVC)
    kvc_ref[...] = mm(_P_KVC, _P_KVS)
    kvs_ref[...] = mm(_P_KVS, _P_KVW)
    kvw_ref[...] = mm(_P_KVW, _P_MQ)
    mq_ref[...] = mm(_P_MQ, _P_MK)
    mk_ref[...] = mm(_P_MK, _P_MV)
    mv_ref[...] = mm(_P_MV, _P_MO)
    mo_ref[...] = jax.nn.sigmoid(mm(_P_MO, _P_SM))
    s = mm(_P_SM, _P_END) + b_ref[...]
    col = lax.broadcasted_iota(jnp.int32, s.shape, 1)
    sm_ref[...] = jnp.where(col < N_GATE_COLS, jax.nn.sigmoid(s), s)


def _project(x, norm_w, w_re, bias_row, tm):
    n = x.shape[0]
    widths = (D_ATT, KV_COLS, KV_COLS, KV_COLS, D_MLSTM, D_MLSTM, D_MLSTM, D_MLSTM, SMALL_COLS)
    return pl.pallas_call(
        _proj_kernel,
        grid=(n // tm,),
        in_specs=[pl.BlockSpec((tm, D_MODEL), lambda i: (i, 0)),
                  pl.BlockSpec((1, D_MODEL), lambda i: (0, 0)),
                  pl.BlockSpec((D_MODEL, _P_END), lambda i: (0, 0)),
                  pl.BlockSpec((1, SMALL_COLS), lambda i: (0, 0))],
        out_specs=[pl.BlockSpec((tm, w), lambda i: (i, 0)) for w in widths],
        out_shape=[jax.ShapeDtypeStruct((n, w), f32) for w in widths],
        compiler_params=_cparams("parallel"),
        name="proj",
    )(x, norm_w, w_re, bias_row)


_HID_COLS = 2 * N_KV_HEADS * CMP_HIDDEN
PAGE_ROWS = 128


def _compress_weights(pe, w1, w2):
    eye = jnp.eye(2, dtype=f32)

    def big1(w):
        return jnp.einsum('cldh,gf->clfdgh', w, eye).reshape(2, CMP_STRIDE * KV_COLS // 2, _HID_COLS // 2).astype(bf16)

    def pe_rows(p):
        row = jnp.broadcast_to(p.transpose(1, 0, 2)[:, :, None, :], (2, CMP_STRIDE, N_KV_HEADS, HEAD_DIM))
        row = row.reshape(2, 1, -1)
        return jnp.concatenate([row, jnp.zeros((2, 7, row.shape[-1]), f32)], axis=1)

    w2_big = jnp.einsum('chd,ce,gf->efhcgd', w2, eye, eye).reshape(_HID_COLS, KV_COLS).astype(bf16)
    pe_all = jnp.stack([pe_rows(pe[:CMP_STRIDE]), pe_rows(pe[CMP_STRIDE:])], axis=0).astype(bf16)
    return big1(w1[:, :CMP_STRIDE]), big1(w1[:, CMP_STRIDE:]), pe_all, w2_big


_CMP_ROWS = 256


def _compress_math(x_lo, x_hi, w1a_ref, w1b_ref, pe_ref, w2_ref):
    n_chunks = x_lo.shape[0] // CMP_STRIDE
    step = min(_CMP_ROWS, n_chunks)
    first, second = [], []
    for c0 in range(0, n_chunks, step):
        f_kv, s_kv = [], []
        for kv, x_ref in enumerate((x_lo, x_hi)):
            x = jnp.concatenate([x_ref[pl.ds(c0 * CMP_STRIDE + l, step, stride=CMP_STRIDE), :]
                                 for l in range(CMP_STRIDE)], axis=1).astype(bf16)
            f_kv.append(jnp.dot(x, w1a_ref[kv], preferred_element_type=f32))
            s_kv.append(jnp.dot(x, w1b_ref[kv], preferred_element_type=f32))
        first.append(jnp.concatenate(f_kv, axis=1))
        second.append(jnp.concatenate(s_kv, axis=1))
    first = jnp.concatenate(first, axis=0)
    second = jnp.concatenate(second, axis=0)
    pe_term = jnp.concatenate(
        [(jnp.dot(pe_ref[0, kv], w1a_ref[kv], preferred_element_type=f32)
          + jnp.dot(pe_ref[1, kv], w1b_ref[kv], preferred_element_type=f32))[0:1] for kv in range(2)], axis=1)
    h = jax.nn.gelu(first + pltpu.roll(second, n_chunks - 1, 0) + pe_term)
    return jnp.dot(h.astype(bf16), w2_ref[...], preferred_element_type=f32)


def _compress_kernel(x_lo, x_hi, w1a_ref, w1b_ref, pe_ref, w2_ref, o_ref):
    o_ref[0] = _compress_math(x_lo.at[0], x_hi.at[0], w1a_ref, w1b_ref, pe_ref, w2_ref)


def _compress_paged_kernel(pt_ref, pool_ref, w1a_ref, w1b_ref, pe_ref, w2_ref, o_ref, raw_s, x_s, sem):
    b = pl.program_id(0)
    n_pages = raw_s.shape[0]
    lanes = N_KV_HEADS * HEAD_DIM

    def page_copy(seq, p):
        return pltpu.make_async_copy(pool_ref.at[pt_ref[seq * n_pages + p]], raw_s.at[p], sem)

    def start_pages(seq):
        def start(p, c):
            page_copy(seq, p).start()
            return c
        lax.fori_loop(0, n_pages, start, 0)

    def wait(p, c):
        page_copy(b, p).wait()
        return c

    def to_rows(p, c):
        r0 = pl.multiple_of(p * PAGE_ROWS, PAGE_ROWS)
        for kv in range(2):
            x_s[kv, pl.ds(r0, PAGE_ROWS), :] = raw_s[p, kv].reshape(lanes, PAGE_ROWS).T
        return c

    @pl.when(b == 0)
    def _():
        start_pages(b)

    lax.fori_loop(0, n_pages, wait, 0)
    lax.fori_loop(0, n_pages, to_rows, 0, unroll=4)

    @pl.when(b + 1 < pl.num_programs(0))
    def _():
        start_pages(b + 1)

    o_ref[0] = _compress_math(x_s.at[0], x_s.at[1], w1a_ref, w1b_ref, pe_ref, w2_ref)


def _compress_paged(page_table, pool_t, cw):
    nb, n_pages = page_table.shape
    n_chunks = n_pages * PAGE_ROWS // CMP_STRIDE
    full = lambda a: pl.BlockSpec(a.shape, lambda b, pt: (0,) * a.ndim, pipeline_mode=pl.Buffered(1))
    return pl.pallas_call(
        _compress_paged_kernel,
        grid_spec=pltpu.PrefetchScalarGridSpec(
            num_scalar_prefetch=1,
            grid=(nb,),
            in_specs=[pl.BlockSpec(memory_space=pl.ANY)] + [full(a) for a in cw],
            out_specs=pl.BlockSpec((1, n_chunks, KV_COLS), lambda b, pt: (b, 0, 0)),
            scratch_shapes=[pltpu.VMEM((n_pages,) + pool_t.shape[1:], f32),
                            pltpu.VMEM((2, n_pages * PAGE_ROWS, KV_COLS // 2), f32),
                            pltpu.SemaphoreType.DMA(())]),
        out_shape=jax.ShapeDtypeStruct((nb, n_chunks, KV_COLS), f32),
        compiler_params=_cparams("arbitrary"),
        name="compress_paged",
    )(page_table.reshape(-1), pool_t, *cw)


def _compress_prompt(kv_rows, cw):
    nb, n_rows, _ = kv_rows.shape
    n_chunks = n_rows // CMP_STRIDE
    full = lambda a: pl.BlockSpec(a.shape, lambda b: (0,) * a.ndim)
    return pl.pallas_call(
        _compress_kernel,
        grid=(nb,),
        in_specs=[pl.BlockSpec((1, n_rows, KV_COLS // 2), lambda b: (b, 0, 0)),
                  pl.BlockSpec((1, n_rows, KV_COLS // 2), lambda b: (b, 0, 1))] + [full(a) for a in cw],
        out_specs=pl.BlockSpec((1, n_chunks, KV_COLS), lambda b: (b, 0, 0)),
        out_shape=jax.ShapeDtypeStruct((nb, n_chunks, KV_COLS), f32),
        compiler_params=_cparams("parallel"),
        name="compress_prompt",
    )(kv_rows, kv_rows, *cw)


def _dot_t0(a, b, **kw):
    return lax.dot_general(a, b, (((0,), (0,)), ((), ())), preferred_element_type=f32, **kw)


def _dot_nt(a, b, **kw):
    return lax.dot_general(a, b, (((1,), (1,)), ((), ())), preferred_element_type=f32, **kw)


def _mlstm_chunk_kernel(q_ref, k_ref, v_ref, gc_ref, gr_ref, h_ref, c_out, n_out, m_out, c_s, n_s, m_s):
    L = q_ref.shape[0]
    NH, DH = N_MLSTM_HEADS, MLSTM_HEAD_DIM

    @pl.when(pl.program_id(0) == 0)
    def _():
        c_s[...] = jnp.zeros_like(c_s)
        n_s[...] = jnp.zeros_like(n_s)
        m_s[...] = jnp.zeros_like(m_s)

    row = lax.broadcasted_iota(jnp.int32, (L, L), 0)
    col = lax.broadcasted_iota(jnp.int32, (L, L), 1)
    causal = col <= row
    gc = gc_ref[...]
    gr = gr_ref[...]
    lf_c = jax.nn.log_sigmoid(gc[:, NH:2 * NH])
    lf_r = jax.nn.log_sigmoid(gr[NH:2 * NH, :])
    f_c = jnp.dot(causal.astype(f32), lf_c, preferred_element_type=f32, precision=HIGHEST)
    f_r = jnp.dot(lf_r, (row <= col).astype(f32), preferred_element_type=f32, precision=HIGHEST)
    for h in range(NH):
        sl = slice(h * DH, (h + 1) * DH)
        fc, fr = f_c[:, h:h + 1], f_r[h:h + 1, :]
        ic, ir = gc[:, h:h + 1], gr[h:h + 1, :]
        m_prev = m_s[h:h + 1, 0:1]
        qh = q_ref[:, sl].astype(bf16)
        kh = k_ref[:, sl] * (DH ** -0.5)
        vh = v_ref[:, sl].astype(bf16)
        log_d = fc - fr + ir
        m_t = jnp.maximum(fc + m_prev, jnp.max(jnp.where(causal, log_d, NEG_INF), axis=1, keepdims=True))
        w = jnp.where(causal, jnp.exp(log_d - m_t), 0.0) * _dot_nt(qh, kh.astype(bf16))
        inter = jnp.exp(fc + m_prev - m_t)
        c_old = c_s[h]
        n_old = n_s[h:h + 1, :]
        num = (jnp.dot(w.astype(bf16), vh, preferred_element_type=f32)
               + inter * jnp.dot(qh, c_old.astype(bf16), preferred_element_type=f32))
        den = (jnp.sum(w, axis=1, keepdims=True)
               + inter * jnp.sum(q_ref[:, sl] * n_old, axis=1, keepdims=True))
        h_ref[:, sl] = num / jnp.maximum(jnp.abs(den), jnp.exp(-m_t))
        f_tot = fc[L - 1:L, :]
        m_new = m_t[L - 1:L, :]
        kw = kh * jnp.exp(f_tot - fc + ic - m_new)
        decay = jnp.exp(f_tot + m_prev - m_new)
        c_new = decay * c_old + _dot_t0(kw.astype(bf16), vh)
        n_new = decay * n_old + jnp.sum(kw, axis=0, keepdims=True)
        c_s[h] = c_new
        n_s[h:h + 1, :] = n_new
        m_s[h:h + 1, :] = jnp.broadcast_to(m_new, (1, DH))
        c_out[h] = c_new
    n_out[...] = n_s[...]
    m_out[...] = m_s[...]


def _mlstm_prompt(mq, mk, mv, gates_c, gates_r, chunk):
    t = mq.shape[0]
    NH, DH = N_MLSTM_HEADS, MLSTM_HEAD_DIM
    tok = lambda w: pl.BlockSpec((chunk, w), lambda c: (c, 0))
    return pl.pallas_call(
        _mlstm_chunk_kernel,
        grid=(t // chunk,),
        in_specs=[tok(D_MLSTM), tok(D_MLSTM), tok(D_MLSTM), tok(2 * NH),
                  pl.BlockSpec((2 * NH, chunk), lambda c: (0, c))],
        out_specs=[tok(D_MLSTM),
                   pl.BlockSpec((NH, DH, DH), lambda c: (0, 0, 0)),
                   pl.BlockSpec((8, DH), lambda c: (0, 0)),
                   pl.BlockSpec((8, DH), lambda c: (0, 0))],
        out_shape=[jax.ShapeDtypeStruct((t, D_MLSTM), f32),
                   jax.ShapeDtypeStruct((NH, DH, DH), f32),
                   jax.ShapeDtypeStruct((8, DH), f32),
                   jax.ShapeDtypeStruct((8, DH), f32)],
        scratch_shapes=[pltpu.VMEM((NH, DH, DH), f32), pltpu.VMEM((8, DH), f32), pltpu.VMEM((8, DH), f32)],
        compiler_params=_cparams("arbitrary"),
        name="mlstm_prompt",
    )(mq, mk, mv, gates_c, gates_r)


def _mlstm_step_kernel(q_ref, k_ref, v_ref, g_ref, c_ref, n_ref, m_ref, h_ref, c_out, n_out, m_out):
    NH, DH = N_MLSTM_HEADS, MLSTM_HEAD_DIM
    b = pl.program_id(0)
    row8 = lax.broadcasted_iota(jnp.int32, (8, DH), 0)
    g = g_ref[pl.ds(b, 1), :]
    m_row = m_ref[pl.ds(b, 1), :]
    m_new_row = jnp.zeros((1, NH), f32)
    lane4 = lax.broadcasted_iota(jnp.int32, (1, NH), 1)
    q_row, k_row, v_row = q_ref[pl.ds(b, 1), :], k_ref[pl.ds(b, 1), :], v_ref[pl.ds(b, 1), :]
    h_parts = []
    for h in range(NH):
        sl = slice(h * DH, (h + 1) * DH)
        q = q_row[:, sl]
        k = k_row[:, sl] * (DH ** -0.5)
        v = v_row[:, sl]
        ig = g[:, h:h + 1]
        lf = jax.nn.log_sigmoid(g[:, NH + h:NH + h + 1])
        m_prev = m_row[:, h:h + 1]
        c_old = c_ref[0, h]
        n_old = n_ref[0, h:h + 1, :]
        m_t = jnp.maximum(lf + m_prev, ig)
        w = jnp.exp(ig - m_t) * jnp.sum(q * k, axis=1, keepdims=True)
        inter = jnp.exp(lf + m_prev - m_t)
        q8 = jnp.where(row8 == 0, q, 0.0)
        qc = jnp.dot(q8, c_old, preferred_element_type=f32, precision=HIGHEST)[0:1]
        num = w * v + inter * qc
        den = w + inter * jnp.sum(q * n_old, axis=1, keepdims=True)
        h_parts.append(num / jnp.maximum(jnp.abs(den), jnp.exp(-m_t)))
        kw = k * jnp.exp(ig - m_t)
        decay = jnp.exp(lf + m_prev - m_t)
        kw8 = jnp.where(row8 == 0, kw, 0.0)
        v8 = jnp.where(row8 == 0, v, 0.0)
        c_out[0, h] = decay * c_old + _dot_t0(kw8, v8, precision=HIGHEST)
        n_out[0, h:h + 1, :] = decay * n_old + kw
        m_new_row = jnp.where(lane4 == h, m_t, m_new_row)
    h_ref[pl.ds(b, 1), :] = jnp.concatenate(h_parts, axis=1)
    m_out[pl.ds(b, 1), :] = m_new_row


def _mlstm_sample(mq, mk, mv, gates, c0, n0, m0):
    nb = mq.shape[0]
    NH, DH = N_MLSTM_HEADS, MLSTM_HEAD_DIM
    full = lambda a: pl.BlockSpec(a.shape, lambda b: (0,) * a.ndim)
    return pl.pallas_call(
        _mlstm_step_kernel,
        grid=(nb,),
        in_specs=[full(mq), full(mk), full(mv), full(gates),
                  pl.BlockSpec((1, NH, DH, DH), lambda b: (b, 0, 0, 0)),
                  pl.BlockSpec((1, NH, DH), lambda b: (b, 0, 0)),
                  full(m0)],
        out_specs=[pl.BlockSpec((nb, D_MLSTM), lambda b: (0, 0)),
                   pl.BlockSpec((1, NH, DH, DH), lambda b: (b, 0, 0, 0)),
                   pl.BlockSpec((1, NH, DH), lambda b: (b, 0, 0)),
                   pl.BlockSpec((nb, NH), lambda b: (0, 0))],
        out_shape=[jax.ShapeDtypeStruct((nb, D_MLSTM), f32),
                   jax.ShapeDtypeStruct((nb, NH, DH, DH), f32),
                   jax.ShapeDtypeStruct((nb, NH, DH), f32),
                   jax.ShapeDtypeStruct((nb, NH), f32)],
        compiler_params=_cparams("arbitrary"),
        name="mlstm_sample",
    )(mq, mk, mv, gates, c0, n0, m0)


SLC_TILE = 512


def _masked_softmax(s, mask):
    sm = jnp.where(mask, s, NEG_INF)
    mx = jnp.max(sm, axis=-1, keepdims=True)
    e = jnp.exp(sm - mx)
    p = e / jnp.sum(e, axis=-1, keepdims=True)
    return jnp.where(mx > 0.5 * NEG_INF, p, 0.0)


def _pool_matrix(n_cmp_rows, n_slc):
    i = np.arange(n_cmp_rows)[:, None]
    j = np.arange(n_slc)[None, :]
    ratio = SLC_BLOCK // CMP_STRIDE
    return jnp.asarray(((i >= ratio * j - 1) & (i <= ratio * j + ratio - 1)).astype(np.float32))


def _select_blocks(p_slc, cur, n_valid_lanes):
    nb = p_slc.shape[1]
    blk = lax.broadcasted_iota(jnp.int32, p_slc.shape, 1)
    valid = blk <= cur
    forced = (blk == 0) | (blk == cur) | (blk == cur - 1)
    score = jnp.where(valid, p_slc + jnp.where(forced, FORCE_BONUS, 0.0), -1.0)
    score = jnp.where(blk < n_valid_lanes, score, -2.0)
    blk_f = blk.astype(f32)
    sel = jnp.zeros(p_slc.shape, f32)
    picks = []
    for _ in range(N_SELECT):
        mx = jnp.max(score, axis=1, keepdims=True)
        idx = jnp.min(jnp.where(score == mx, blk_f, float(nb)), axis=1, keepdims=True)
        hit = blk_f == idx
        sel = jnp.where(hit, 1.0, sel)
        score = jnp.where(hit, -3.0, score)
        picks.append(idx.astype(jnp.int32))
    return sel, picks


_SUB = 128
_M_FLOOR = -1e20


def _with_pos_feature(k, index0=0):
    n = k.shape[0]
    pos = ((jnp.arange(n) + index0) % _SUB).astype(f32)[:, None]
    return jnp.concatenate([k, pos, jnp.zeros((n, _SUB - HEAD_DIM - 1), f32)], axis=1).astype(bf16)


def _colmax8(x):
    out = x[0:8]
    for r in range(8, x.shape[0], 8):
        out = jnp.maximum(out, x[r:r + 8])
    return out


def _colsum8(x):
    out = x[0:8]
    for r in range(8, x.shape[0], 8):
        out = out + x[r:r + 8]
    return out


_PSUM_PAD = 8
_CMP_UNROLL = 2


def _nsa_prompt_t_kernel(qT_ref, gT_ref, kc_ref, vcT_ref, ks_ref, vsT_ref, kw_ref, vwT_ref, o_ref,
                         sc_s, psum_s, sel_s, m_s, l_s, acc_s):
    tq = qT_ref.shape[1]
    L = GQA_REP * tq
    n_cmp = kc_ref.shape[1]
    n_slc = sel_s.shape[0]
    assert tq == _SUB and n_cmp * CMP_STRIDE == n_slc * SLC_BLOCK
    i = pl.program_id(0)
    start = pl.multiple_of(i * tq, tq)
    lane = lax.broadcasted_iota(jnp.int32, (1, L), 1)
    t_lane = lane % tq
    qpos = start + t_lane
    qpos_f = qpos.astype(f32)
    qpos_t = qpos[:, 0:tq]
    sub_l = lax.broadcasted_iota(jnp.int32, (_SUB, L), 0)
    sub_t = lax.broadcasted_iota(jnp.int32, (_SUB, tq), 0)
    feat_row = lax.broadcasted_iota(jnp.int32, (_SUB - HEAD_DIM, L), 0) == 0
    n_tiles = (start + tq + SLC_TILE - 1) // SLC_TILE
    nc_blocks = (start // CMP_STRIDE + (tq - CMP_BLOCK) // CMP_STRIDE) // _SUB + 1
    nc_trips = (nc_blocks + _CMP_UNROLL - 1) // _CMP_UNROLL
    assert (n_cmp // _SUB) % _CMP_UNROLL == 0
    blocks_per_tile = SLC_TILE // SLC_BLOCK

    for g in range(N_KV_HEADS):
        vrows = slice(g * HEAD_DIM, (g + 1) * HEAD_DIM)
        slope = jnp.exp2(-8.0 * (GQA_REP * g + 1 + lane // tq).astype(f32) / N_ATT_HEADS)
        q_rows = jnp.concatenate(
            [qT_ref[(GQA_REP * g + r) * HEAD_DIM:(GQA_REP * g + r + 1) * HEAD_DIM, :] for r in range(GQA_REP)],
            axis=1).astype(f32) * (HEAD_DIM ** -0.5)
        q_pos = jnp.concatenate([q_rows, jnp.where(feat_row, slope, 0.0)], axis=0).astype(bf16)
        q_cmp = jnp.concatenate([q_rows, jnp.where(feat_row, slope * CMP_STRIDE, 0.0)], axis=0).astype(bf16)

        def cmp_scores(ct, m8):
            r0s = [pl.multiple_of((ct * _CMP_UNROLL + u) * _SUB, _SUB) for u in range(_CMP_UNROLL)]
            dots = [jnp.dot(kc_ref[g, pl.ds(r0, _SUB), :], q_cmp, preferred_element_type=f32) for r0 in r0s]
            for r0, s in zip(r0s, dots):
                end0 = r0 * CMP_STRIDE + (CMP_BLOCK - 1)
                off = slope * (end0.astype(f32) - qpos_f)
                vis = (end0 + sub_l * CMP_STRIDE) <= qpos
                s = jnp.where(vis, s + off, NEG_INF)
                sc_s[pl.ds(r0, _SUB), :] = s
                m8 = jnp.maximum(m8, _colmax8(s))
            return m8

        m8 = lax.fori_loop(0, nc_trips, cmp_scores, jnp.full((8, L), NEG_INF, f32))
        m_c = jnp.maximum(jnp.max(m8, axis=0, keepdims=True), _M_FLOOR)

        def cmp_exp(ct, carry):
            l8, o_acc = carry
            for u in range(_CMP_UNROLL):
                r0 = pl.multiple_of((ct * _CMP_UNROLL + u) * _SUB, _SUB)
                e = jnp.exp(sc_s[pl.ds(r0, _SUB), :] - m_c)
                sc_s[pl.ds(r0, _SUB), :] = e
                o_acc = o_acc + jnp.dot(vcT_ref[vrows, pl.ds(r0, _SUB)], e.astype(bf16),
                                        preferred_element_type=f32)
                l8 = l8 + _colsum8(e)
            return l8, o_acc

        l8, o_c = lax.fori_loop(0, nc_trips, cmp_exp,
                                (jnp.zeros((8, L), f32), jnp.zeros((HEAD_DIM, L), f32)))
        l_c = jnp.sum(l8, axis=0, keepdims=True)
        inv_c = jnp.where(l_c > 0.0, 1.0 / jnp.where(l_c > 0.0, l_c, 1.0), 0.0)
        o_c = o_c * inv_c
        psum_s[...] = jnp.zeros_like(psum_s)

        def cmp_group_sum(ct, carry):
            for u in range(_CMP_UNROLL):
                r0 = pl.multiple_of((ct * _CMP_UNROLL + u) * _SUB, _SUB)
                p = sc_s[pl.ds(r0, _SUB), :] * inv_c
                psum_s[pl.ds(r0 + _PSUM_PAD, _SUB), :] = ((p[:, 0:tq] + p[:, tq:2 * tq])
                                                          + (p[:, 2 * tq:3 * tq] + p[:, 3 * tq:4 * tq]))
            return carry

        lax.fori_loop(0, nc_trips, cmp_group_sum, 0)
        ratio = SLC_BLOCK // CMP_STRIDE
        p_slc = psum_s[pl.ds(_PSUM_PAD - 1, n_slc, stride=ratio), :]
        for c in range(ratio):
            p_slc = p_slc + psum_s[pl.ds(_PSUM_PAD + c, n_slc, stride=ratio), :]

        blk = lax.broadcasted_iota(jnp.int32, (n_slc, tq), 0).astype(f32)
        cur = (qpos_t // SLC_BLOCK).astype(f32)
        forced = (blk == 0.0) | (blk == cur) | (blk == cur - 1.0)
        score = jnp.where(blk <= cur, p_slc + jnp.where(forced, FORCE_BONUS, 0.0), -1.0)
        sel = jnp.zeros((n_slc, tq), f32)
        for _ in range(N_SELECT):
            mx = jnp.max(score, axis=0, keepdims=True)
            idx = jnp.min(jnp.where(score == mx, blk, float(n_slc)), axis=0, keepdims=True)
            hit = blk == idx
            sel = jnp.where(hit, 1.0, sel)
            score = jnp.where(hit, -3.0, score)
        sel_s[...] = sel

        per_blk = _SUB // SLC_BLOCK
        m_s[...] = jnp.full(m_s.shape, NEG_INF, f32)
        l_s[...] = jnp.zeros(l_s.shape, f32)
        acc_s[...] = jnp.zeros(acc_s.shape, f32)
        lowest = jnp.min(jnp.where((sel > 0.5) & (blk >= 1.0), blk, float(n_slc))).astype(jnp.int32)

        def slc_tile(j, carry):
            k0 = pl.multiple_of(j * SLC_TILE, SLC_TILE)
            picked = sel_s[pl.ds(pl.multiple_of(j * blocks_per_tile, blocks_per_tile), blocks_per_tile), :]
            s = jnp.dot(ks_ref[g, pl.ds(k0, SLC_TILE), :], q_pos, preferred_element_type=f32)
            parts = []
            for w in range(SLC_TILE // _SUB):
                base = k0 + w * _SUB
                picked_k = jnp.concatenate(
                    [jnp.broadcast_to(picked[per_blk * w + u:per_blk * w + u + 1, :], (SLC_BLOCK, tq))
                     for u in range(per_blk)], axis=0)
                ok = (picked_k > 0.5) & (base + sub_t <= qpos_t)
                bias = jnp.where(ok, 0.0, NEG_INF)
                bias = jnp.concatenate([bias] * GQA_REP, axis=1) + slope * (base.astype(f32) - qpos_f)
                parts.append(s[w * _SUB:(w + 1) * _SUB] + bias)
            s = jnp.concatenate(parts, axis=0)
            m_old = m_s[...]
            m_new = jnp.maximum(jnp.maximum(m_old, jnp.max(_colmax8(s), axis=0, keepdims=True)), _M_FLOOR)
            p = jnp.exp(s - m_new)
            alpha = jnp.exp(m_old - m_new)
            l_s[...] = alpha * l_s[...] + jnp.sum(_colsum8(p), axis=0, keepdims=True)
            acc_s[...] = alpha * acc_s[...] + jnp.dot(vsT_ref[vrows, pl.ds(k0, SLC_TILE)], p.astype(bf16),
                                                      preferred_element_type=f32)
            m_s[...] = m_new
            return carry

        slc_tile(jnp.int32(0), 0)
        lax.fori_loop(jnp.maximum(lowest // blocks_per_tile, 1), n_tiles, slc_tile, 0)
        l_fin = l_s[...]
        o_s = acc_s[...] * jnp.where(l_fin > 0.0, 1.0 / jnp.where(l_fin > 0.0, l_fin, 1.0), 0.0)

        n_win_sub = (WINDOW + tq) // _SUB
        s_parts = []
        for w in range(n_win_sub):
            base = start - WINDOW + w * _SUB
            s = jnp.dot(kw_ref[g, pl.ds(start + w * _SUB, _SUB), :], q_pos, preferred_element_type=f32)
            if w == 0:
                ok = sub_l > t_lane
            elif w == n_win_sub - 1:
                ok = sub_l <= t_lane
            else:
                ok = None
            bias = slope * (base.astype(f32) - qpos_f) + jnp.where(base >= 0, 0.0, NEG_INF)
            s = s + bias
            s_parts.append(s if ok is None else jnp.where(ok, s, NEG_INF))
        m8 = _colmax8(s_parts[0])
        for s in s_parts[1:]:
            m8 = jnp.maximum(m8, _colmax8(s))
        m_w = jnp.maximum(jnp.max(m8, axis=0, keepdims=True), _M_FLOOR)
        l8 = jnp.zeros((8, L), f32)
        o_w = jnp.zeros((HEAD_DIM, L), f32)
        for w, s in enumerate(s_parts):
            e = jnp.exp(s - m_w)
            l8 = l8 + _colsum8(e)
            o_w = o_w + jnp.dot(vwT_ref[vrows, pl.ds(start + w * _SUB, _SUB)], e.astype(bf16),
                                preferred_element_type=f32)
        l_w = jnp.sum(l8, axis=0, keepdims=True)
        o_w = o_w * jnp.where(l_w > 0.0, 1.0 / jnp.where(l_w > 0.0, l_w, 1.0), 0.0)

        for r in range(GQA_REP):
            head = GQA_REP * g + r
            cols = slice(r * tq, (r + 1) * tq)
            o_ref[head * HEAD_DIM:(head + 1) * HEAD_DIM, :] = (
                gT_ref[3 * head:3 * head + 1, :] * o_c[:, cols]
                + gT_ref[3 * head + 1:3 * head + 2, :] * o_s[:, cols]
                + gT_ref[3 * head + 2:3 * head + 3, :] * o_w[:, cols])


def _nsa_prompt_t(q, small, kv_cmp, kvs, kvw):
    t = q.shape[0]
    n_cmp = kv_cmp.shape[0]
    n_slc = t // SLC_BLOCK
    kcol = lambda g: slice(g * HEAD_DIM, (g + 1) * HEAD_DIM)
    vT = lambda a: a[:, N_KV_HEADS * HEAD_DIM:].T.astype(bf16)
    qT = q.T.astype(bf16)
    gT = small[:, :32].T
    kc = jnp.stack([_with_pos_feature(kv_cmp[:, kcol(g)]) for g in range(N_KV_HEADS)])
    ks = jnp.stack([_with_pos_feature(kvs[:, kcol(g)]) for g in range(N_KV_HEADS)])
    kvw_pad = jnp.pad(kvw, ((WINDOW, 0), (0, 0)))
    kw = jnp.stack([_with_pos_feature(kvw_pad[:, kcol(g)]) for g in range(N_KV_HEADS)])
    operands = (qT, gT, kc, vT(kv_cmp), ks, vT(kvs), kw, vT(kvw_pad))
    L = GQA_REP * Q_BLOCK
    const = lambda a: pl.BlockSpec(a.shape, lambda i: (0,) * a.ndim, pipeline_mode=pl.Buffered(1))
    return pl.pallas_call(
        _nsa_prompt_t_kernel,
        grid=(t // Q_BLOCK,),
        in_specs=[pl.BlockSpec((D_ATT, Q_BLOCK), lambda i: (0, i)),
                  pl.BlockSpec((32, Q_BLOCK), lambda i: (0, i))] + [const(a) for a in operands[2:]],
        out_specs=pl.BlockSpec((D_ATT, Q_BLOCK), lambda i: (0, i)),
        out_shape=jax.ShapeDtypeStruct((D_ATT, t), f32),
        scratch_shapes=[pltpu.VMEM((n_cmp, L), f32), pltpu.VMEM((n_cmp + _PSUM_PAD, Q_BLOCK), f32),
                        pltpu.VMEM((n_slc, Q_BLOCK), f32), pltpu.VMEM((1, L), f32), pltpu.VMEM((1, L), f32),
                        pltpu.VMEM((HEAD_DIM, L), f32)],
        compiler_params=_cparams("parallel"),
        name="nsa_prompt",
    )(*operands)


_QROWS = 16
_IDS_LANES = 128


def _group_queries_1(q, g):
    rows = [q[:, (GQA_REP * g + r) * HEAD_DIM:(GQA_REP * g + r + 1) * HEAD_DIM] for r in range(GQA_REP)]
    rows.append(jnp.zeros((_QROWS - GQA_REP, HEAD_DIM), f32))
    return jnp.concatenate(rows, axis=0) * (HEAD_DIM ** -0.5)


def _group_slopes_1(g):
    r = jnp.minimum(lax.broadcasted_iota(jnp.int32, (_QROWS, 1), 0), GQA_REP - 1)
    return jnp.exp2(-8.0 * (GQA_REP * g + 1 + r).astype(f32) / N_ATT_HEADS)


def _nsa_sample_cmp_kernel(q_ref, cmp_ref, pool_ref, oc_ref, ids_ref, *, past):
    b = pl.program_id(0)
    n_cmp = cmp_ref.shape[1]
    n_old = pool_ref.shape[1]
    q = q_ref[pl.ds(b, 1), :]
    cmpb = cmp_ref[0].astype(bf16)
    d_c = past - (lax.broadcasted_iota(jnp.int32, (1, n_cmp), 1) * CMP_STRIDE + (CMP_BLOCK - 1))
    lane = lax.broadcasted_iota(jnp.int32, (1, _IDS_LANES), 1)
    cur = jnp.full((1, 1), past // SLC_BLOCK, jnp.int32)
    oc_parts = []
    for g in range(N_KV_HEADS):
        q16 = _group_queries_1(q, g).astype(bf16)
        s = _dot_nt(q16, cmpb[:, g * HEAD_DIM:(g + 1) * HEAD_DIM]) - _group_slopes_1(g) * d_c.astype(f32)
        p = _masked_softmax(s, jnp.broadcast_to(d_c >= 0, s.shape))
        o_c = jnp.dot(p.astype(bf16), cmpb[:, (N_KV_HEADS + g) * HEAD_DIM:(N_KV_HEADS + g + 1) * HEAD_DIM],
                      preferred_element_type=f32)
        oc_parts.extend(o_c[r:r + 1] for r in range(GQA_REP))
        p_sum = jnp.sum(p[0:GQA_REP], axis=0, keepdims=True)
        p_slc = jnp.dot(jnp.broadcast_to(p_sum, (8, n_cmp)), pool_ref[...], preferred_element_type=f32,
                        precision=HIGHEST)[0:1]
        p_ext = jnp.concatenate([p_slc, jnp.zeros((1, 128), f32)], axis=1)
        _, picks = _select_blocks(p_ext, cur, past // SLC_BLOCK + 1)
        row = jnp.zeros((1, _IDS_LANES), jnp.int32)
        for k, pick in enumerate(picks):
            row = jnp.where(lane == k, pick, row)
        ids_ref[0, g:g + 1, :] = row
    oc_ref[pl.ds(b, 1), :] = jnp.concatenate(oc_parts, axis=1)


def _nsa_sample_cmp(q, cmp_s, pool, past):
    nb = q.shape[0]
    return pl.pallas_call(
        functools.partial(_nsa_sample_cmp_kernel, past=past),
        grid=(nb,),
        in_specs=[pl.BlockSpec(q.shape, lambda b: (0, 0)),
                  pl.BlockSpec((1,) + cmp_s.shape[1:], lambda b: (b, 0, 0)),
                  pl.BlockSpec(pool.shape, lambda b: (0, 0))],
        out_specs=[pl.BlockSpec((nb, D_ATT), lambda b: (0, 0)),
                   pl.BlockSpec((1, N_KV_HEADS, _IDS_LANES), lambda b: (b, 0, 0))],
        out_shape=[jax.ShapeDtypeStruct((nb, D_ATT), f32),
                   jax.ShapeDtypeStruct((nb, N_KV_HEADS, _IDS_LANES), jnp.int32)],
        compiler_params=_cparams("arbitrary"),
        name="nsa_sample_cmp",
    )(q, cmp_s, pool)


def _softmax_with_new(s, mask, s_new, new_ok):
    s = jnp.where(mask, s, NEG_INF)
    s_new = jnp.where(new_ok, s_new, NEG_INF)
    m = jnp.maximum(jnp.max(s, axis=1, keepdims=True), s_new)
    p = jnp.where(mask, jnp.exp(s - m), 0.0)
    p_new = jnp.where(new_ok, jnp.exp(s_new - m), 0.0)
    l = jnp.sum(p, axis=1, keepdims=True) + p_new
    inv = jnp.where(l > 0.0, 1.0 / jnp.where(l > 0.0, l, 1.0), 0.0)
    return p * inv, p_new * inv


def _nsa_sample_sel_kernel(ids_ref, pt_ref, pool_ref, q_ref, sm_ref, kvs_ref, kvw_ref, win_ref, oc_ref, o_ref,
                           buf, sem, *, past):
    b = pl.program_id(0)
    n_pages = past // PAGE_ROWS
    last_blk = past // SLC_BLOCK
    blocks_per_page = PAGE_ROWS // SLC_BLOCK

    def page_copy(g, k, c):
        blk = jnp.minimum(ids_ref[(b * N_KV_HEADS + g) * N_SELECT + k], last_blk - 1)
        page = pt_ref[b * n_pages + blk // blocks_per_page]
        return pltpu.make_async_copy(pool_ref.at[page, c, g], buf.at[g, c, k], sem)

    copies = [(g, k, c) for g in range(N_KV_HEADS) for k in range(N_SELECT) for c in range(2)]
    for g, k, c in copies:
        page_copy(g, k, c).start()
    for g, k, c in copies:
        page_copy(g, k, c).wait()

    q = q_ref[pl.ds(b, 1), :]
    gates = sm_ref[pl.ds(b, 1), :]
    new_s = kvs_ref[pl.ds(b, 1), :]
    new_w = kvw_ref[pl.ds(b, 1), :]
    n_sel = N_SELECT * PAGE_ROWS
    lane = lax.broadcasted_iota(jnp.int32, (1, n_sel), 1)
    w_buf = win_ref.shape[1]
    d_w = w_buf - lax.broadcasted_iota(jnp.int32, (1, w_buf), 1)
    mask_w = jnp.broadcast_to((d_w < WINDOW) & (past - d_w >= 0), (_QROWS, w_buf))
    always = jnp.full((_QROWS, 1), True)
    o_c = oc_ref[pl.ds(b, 1), :]
    out_parts = []
    for g in range(N_KV_HEADS):
        kcol = slice(g * HEAD_DIM, (g + 1) * HEAD_DIM)
        vcol = slice((N_KV_HEADS + g) * HEAD_DIM, (N_KV_HEADS + g + 1) * HEAD_DIM)
        q16f = _group_queries_1(q, g)
        q16 = q16f.astype(bf16)
        slope = _group_slopes_1(g)

        blk_vec = jnp.zeros((1, n_sel), jnp.int32)
        for k in range(N_SELECT):
            blk_vec = jnp.where(lane // PAGE_ROWS == k, ids_ref[(b * N_KV_HEADS + g) * N_SELECT + k], blk_vec)
        kpos = (blk_vec // blocks_per_page) * PAGE_ROWS + lane % PAGE_ROWS
        d_s = past - kpos
        mask_s = jnp.broadcast_to((d_s >= 0) & (blk_vec < last_blk) & (kpos // SLC_BLOCK == blk_vec),
                                  (_QROWS, n_sel))
        has_new = jnp.max(jnp.where(blk_vec == last_blk, 1, 0), axis=1, keepdims=True) > 0
        kt = jnp.concatenate([buf[g, 0, k] for k in range(N_SELECT)], axis=1).astype(bf16)
        vt = jnp.concatenate([buf[g, 1, k] for k in range(N_SELECT)], axis=1).astype(bf16)
        s = jnp.dot(q16, kt, preferred_element_type=f32) - slope * d_s.astype(f32)
        s_new = jnp.sum(q16f * new_s[:, kcol], axis=1, keepdims=True)
        p, p_new = _softmax_with_new(s, mask_s, s_new, jnp.broadcast_to(has_new, (_QROWS, 1)))
        o_s = _dot_nt(p.astype(bf16), vt) + p_new * new_s[:, vcol]

        wb = win_ref[0].astype(bf16)
        s = _dot_nt(q16, wb[:, kcol]) - slope * d_w.astype(f32)
        s_new = jnp.sum(q16f * new_w[:, kcol], axis=1, keepdims=True)
        p, p_new = _softmax_with_new(s, mask_w, s_new, always)
        o_w = jnp.dot(p.astype(bf16), wb[:, vcol], preferred_element_type=f32) + p_new * new_w[:, vcol]

        for r in range(GQA_REP):
            head = GQA_REP * g + r
            hs = slice(head * HEAD_DIM, (head + 1) * HEAD_DIM)
            out_parts.append(gates[:, 3 * head:3 * head + 1] * o_c[:, hs]
                             + gates[:, 3 * head + 1:3 * head + 2] * o_s[r:r + 1]
                             + gates[:, 3 * head + 2:3 * head + 3] * o_w[r:r + 1])
    o_ref[pl.ds(b, 1), :] = jnp.concatenate(out_parts, axis=1)


def _nsa_sample_sel(ids, page_table, pool_slc, q, small, kvs_new, kvw_new, win_buf, o_c, past):
    nb = q.shape[0]
    full = lambda a: pl.BlockSpec(a.shape, lambda b, ids, pt: (0,) * a.ndim)
    return pl.pallas_call(
        functools.partial(_nsa_sample_sel_kernel, past=past),
        grid_spec=pltpu.PrefetchScalarGridSpec(
            num_scalar_prefetch=2,
            grid=(nb,),
            in_specs=[pl.BlockSpec(memory_space=pl.ANY), full(q), full(small), full(kvs_new), full(kvw_new),
                      pl.BlockSpec((1,) + win_buf.shape[1:], lambda b, ids, pt: (b, 0, 0)), full(o_c)],
            out_specs=pl.BlockSpec((nb, D_ATT), lambda b, ids, pt: (0, 0)),
            scratch_shapes=[pltpu.VMEM((N_KV_HEADS, 2, N_SELECT, HEAD_DIM, PAGE_ROWS), f32),
                            pltpu.SemaphoreType.DMA(())]),
        out_shape=jax.ShapeDtypeStruct((nb, D_ATT), f32),
        compiler_params=_cparams("arbitrary"),
        name="nsa_sample_sel",
    )(ids, page_table.reshape(-1), pool_slc, q, small, kvs_new, kvw_new, win_buf, o_c)


def _mixout_kernel(att_ref, mh_ref, mo_ref, x_ref, ga_ref, gm_ref, wo_ref, nf_ref, wq_ref, h_ref, xn_ref,
                   qp_ref):
    parts = []
    for h in range(N_ATT_HEADS):
        sl = slice(h * HEAD_DIM, (h + 1) * HEAD_DIM)
        parts.append(_rms(att_ref[:, sl], ga_ref[:, sl]))
    for h in range(N_MLSTM_HEADS):
        sl = slice(h * MLSTM_HEAD_DIM, (h + 1) * MLSTM_HEAD_DIM)
        parts.append(mo_ref[:, sl] * _rms(mh_ref[:, sl], gm_ref[:, sl]))
    cat = jnp.concatenate(parts, axis=1).astype(bf16)
    h1 = x_ref[...] + jnp.dot(cat, wo_ref[...], preferred_element_type=f32)
    h_ref[...] = h1
    xn = _rms(h1, nf_ref[...]).astype(bf16)
    xn_ref[...] = xn
    qp_ref[...] = jnp.dot(xn, wq_ref[...], preferred_element_type=f32)


def _mix_output(att, mh, mo, x, g_att, g_ml, w_out_bf, norm_ffn, w_q_bf, tm):
    n = x.shape[0]
    tok = lambda w: pl.BlockSpec((tm, w), lambda i: (i, 0))
    full = lambda a: pl.BlockSpec(a.shape, lambda i: (0,) * a.ndim)
    return pl.pallas_call(
        _mixout_kernel,
        grid=(n // tm,),
        in_specs=[tok(D_ATT), tok(D_MLSTM), tok(D_MLSTM), tok(D_MODEL), full(g_att), full(g_ml),
                  full(w_out_bf), full(norm_ffn), full(w_q_bf)],
        out_specs=[tok(D_MODEL), tok(D_MODEL), tok(D_MODEL)],
        out_shape=[jax.ShapeDtypeStruct((n, D_MODEL), f32), jax.ShapeDtypeStruct((n, D_MODEL), bf16),
                   jax.ShapeDtypeStruct((n, D_MODEL), f32)],
        compiler_params=_cparams("parallel"),
        name="mix_output",
    )(att, mh, mo, x, g_att, g_ml, w_out_bf, norm_ffn, w_q_bf)


def _topk_rows(s, k):
    n = s.shape[0]
    rows = lax.broadcasted_iota(jnp.int32, s.shape, 0)
    vals, idxs = [], []
    for _ in range(k):
        mx = jnp.max(s, axis=0, keepdims=True)
        idx = jnp.min(jnp.where(s == mx, rows, n), axis=0, keepdims=True)
        vals.append(mx)
        idxs.append(idx)
        s = jnp.where(rows == idx, NEG_INF, s)
    return jnp.concatenate(vals, axis=0), jnp.concatenate(idxs, axis=0)


def _peer_topk_kernel(qp_ref, sub_ref, ei_ref, ej_ref, g_ref):
    K = PEER_TOPK
    half = PEER_D_KEY // 2
    ei, ej, gg = [], [], []
    for h in range(PEER_HEADS):
        sv, si = [], []
        for c in range(2):
            qhc = qp_ref[:, (2 * h + c) * half:(2 * h + c + 1) * half]
            s = _dot_nt(sub_ref[h, c], qhc, precision=HIGHEST)
            v, i = _topk_rows(s, K)
            sv.append(v)
            si.append(i)
        n_t = sv[0].shape[1]
        counts = [K // (a + 1) for a in range(K)]
        n_cand = -(-sum(counts) // 8) * 8
        pad = n_cand - sum(counts)
        cand = jnp.concatenate([sv[0][a:a + 1] + sv[1][0:counts[a]] for a in range(K)]
                               + [jnp.full((pad, n_t), NEG_INF, f32)], axis=0)
        pos_i = jnp.concatenate([jnp.broadcast_to(si[0][a:a + 1], (counts[a], n_t)) for a in range(K)]
                                + [jnp.zeros((pad, n_t), jnp.int32)], axis=0)
        pos_j = jnp.concatenate([si[1][0:counts[a]] for a in range(K)] + [jnp.zeros((pad, n_t), jnp.int32)],
                                axis=0)
        rows = lax.broadcasted_iota(jnp.int32, cand.shape, 0)
        best, bi, bj = [], [], []
        for _ in range(K):
            mx = jnp.max(cand, axis=0, keepdims=True)
            pos = jnp.min(jnp.where(cand == mx, rows, n_cand), axis=0, keepdims=True)
            hit = rows == pos
            best.append(mx)
            bi.append(jnp.max(jnp.where(hit, pos_i, -1), axis=0, keepdims=True))
            bj.append(jnp.max(jnp.where(hit, pos_j, -1), axis=0, keepdims=True))
            cand = jnp.where(hit, NEG_INF, cand)
        best = jnp.concatenate(best, axis=0)
        e = jnp.exp(best - best[0:1])
        gg.append(e / jnp.sum(e, axis=0, keepdims=True))
        ei.append(jnp.concatenate(bi, axis=0))
        ej.append(jnp.concatenate(bj, axis=0))
    ei_ref[...] = jnp.concatenate(ei, axis=0).astype(f32).T
    ej_ref[...] = jnp.concatenate(ej, axis=0).astype(f32).T
    g_ref[...] = jnp.concatenate(gg, axis=0).T


def _peer_topk(qp, sub_keys, tm):
    n = qp.shape[0]
    hk = PEER_HEADS * PEER_TOPK
    return pl.pallas_call(
        _peer_topk_kernel,
        grid=(n // tm,),
        in_specs=[pl.BlockSpec((tm, D_MODEL), lambda i: (i, 0)),
                  pl.BlockSpec(sub_keys.shape, lambda i: (0, 0, 0, 0))],
        out_specs=[pl.BlockSpec((tm, hk), lambda i: (i, 0))] * 3,
        out_shape=[jax.ShapeDtypeStruct((n, hk), f32)] * 3,
        compiler_params=_cparams("parallel"),
        name="peer_topk",
    )(qp, sub_keys)


_GATE_UNROLL = 32
_GATE_PARTS = 2


def _peer_dense_kernel(ei_ref, ej_ref, gg_ref, x_ref, u_ref, v_ref, h_ref, nf_ref, o_ref, acc_ref, g_s):
    part, e = pl.program_id(1), pl.program_id(2)
    tm = x_ref.shape[0]
    et = u_ref.shape[0]
    nk = PEER_N_KEYS
    part_rows = nk // _GATE_PARTS

    @pl.when((part == 0) & (e == 0))
    def _():
        acc_ref[...] = jnp.zeros_like(acc_ref)

    @pl.when(e == 0)
    def _():
        i0 = (part * part_rows).astype(f32)
        sub_i = lax.broadcasted_iota(jnp.int32, (part_rows, ei_ref.shape[1]), 0).astype(f32) + i0
        sub_j = lax.broadcasted_iota(jnp.int32, (nk, ei_ref.shape[1]), 0).astype(f32)

        zero = jnp.zeros((part_rows, ei_ref.shape[1]), bf16)

        def body(tp, carry):
            t0 = pl.multiple_of(tp * 2, 2)
            ei, ej, gg = ei_ref[pl.ds(t0, 2), :], ej_ref[pl.ds(t0, 2), :], gg_ref[pl.ds(t0, 2), :]
            a = [jnp.where(sub_i == ei[u:u + 1], 1.0, 0.0).astype(bf16) for u in range(2)]
            b = [jnp.where(sub_j == ej[u:u + 1], gg[u:u + 1], 0.0).astype(bf16) for u in range(2)]
            lhs = jnp.concatenate([jnp.concatenate([a[0], zero], axis=1), jnp.concatenate([zero, a[1]], axis=1)],
                                  axis=0)
            g_s[pl.ds(pl.multiple_of(t0 * part_rows, 2 * part_rows), 2 * part_rows), :] = _dot_nt(
                lhs, jnp.concatenate(b, axis=1))
            return carry

        lax.fori_loop(0, tm // 2, body, 0, unroll=_GATE_UNROLL)

    rows = et // nk
    g = jnp.concatenate([g_s[pl.ds(e * rows + r, tm, stride=part_rows), :] for r in range(rows)], axis=1)
    act = jax.nn.gelu(_dot_nt(x_ref[...], u_ref[...]))
    acc_ref[...] += jnp.dot((g * act).astype(bf16), v_ref[...], preferred_element_type=f32)

    @pl.when((part == _GATE_PARTS - 1) & (e == pl.num_programs(2) - 1))
    def _():
        o_ref[...] = _rms(h_ref[...] + acc_ref[...], nf_ref[...])


def _peer_dense(ei, ej, gg, xn_bf, u_bf, v_bf, h1, norm_final, tm, et):
    n = xn_bf.shape[0]
    n_exp = u_bf.shape[0]
    hk = ei.shape[1]
    tok = lambda w: pl.BlockSpec((tm, w), lambda i, p, e: (i, 0))
    steps = n_exp // _GATE_PARTS // et
    expert_tile = pl.BlockSpec((et, D_MODEL), lambda i, p, e: (p * steps + e, 0))
    return pl.pallas_call(
        _peer_dense_kernel,
        grid=(n // tm, _GATE_PARTS, steps),
        in_specs=[tok(hk), tok(hk), tok(hk), tok(D_MODEL), expert_tile, expert_tile, tok(D_MODEL),
                  pl.BlockSpec((1, D_MODEL), lambda i, p, e: (0, 0))],
        out_specs=tok(D_MODEL),
        out_shape=jax.ShapeDtypeStruct((n, D_MODEL), f32),
        scratch_shapes=[pltpu.VMEM((tm, D_MODEL), f32),
                        pltpu.VMEM((tm * PEER_N_KEYS // _GATE_PARTS, PEER_N_KEYS), f32)],
        compiler_params=_cparams("parallel", "arbitrary", "arbitrary"),
        name="peer_dense",
    )(ei, ej, gg, xn_bf, u_bf, v_bf, h1, norm_final)


def _channel_mix_and_norm(h1, xn_bf, qp, sub_keys, u_bf, v_bf, norm_final, tm_topk, tm, et):
    ei, ej, gg = _peer_topk(qp, sub_keys, tm_topk)
    return _peer_dense(ei, ej, gg, xn_bf, u_bf, v_bf, h1, norm_final, tm, et)


PROMPT_TM = 512
PEER_TM = 512
PEER_ET = 2048
MLSTM_CHUNK = 256
SAMPLE_PAD = 128


def kernel(x_prompt, x_sample, cache_cmp_kv, cache_slc_kv, cache_win_kv, state_mlstm_C, state_mlstm_n,
           state_mlstm_m, page_table, norm_mix, w_in, b_igate, b_fgate, pe_cmp, w_cmp1, w_cmp2, norm_att_out,
           norm_mlstm_out, w_out, norm_ffn, peer_wq, peer_subkeys, peer_u, peer_v, norm_final):
    assert x_prompt.shape[0] == 1 and x_sample.shape[1] == 1 and w_in.shape[0] == 1
    _, t, d = x_prompt.shape
    nb = x_sample.shape[0]
    n_pool = cache_cmp_kv.shape[1]
    past = page_table.shape[1] * PAGE_ROWS
    w_buf = cache_win_kv.shape[2]
    row = (2, N_KV_HEADS, HEAD_DIM)
    NH, DH = N_MLSTM_HEADS, MLSTM_HEAD_DIM
    g0 = N_GATE_COLS

    w_re = _relayout_w_in(w_in[0])
    bias = _bias_row(b_igate[0], b_fgate[0])
    cw = _compress_weights(pe_cmp[0], w_cmp1[0], w_cmp2[0])
    nm = norm_mix[0][None]
    ga, gm, nf, nfin = norm_att_out[0][None], norm_mlstm_out[0][None], norm_ffn[0][None], norm_final[None]
    w_out_bf, w_q_bf = w_out[0].astype(bf16), peer_wq[0].astype(bf16)
    u_bf, v_bf = peer_u[0].astype(bf16), peer_v[0].astype(bf16)
    sub_keys = peer_subkeys[0]

    xp = x_prompt.reshape(t, d)
    q, kvc, kvs, kvw, mq, mk, mv, mo, sm = _project(xp, nm, w_re, bias, PROMPT_TM)
    cmp_p = _compress_prompt(kvc[None], cw)[0]
    att = _nsa_prompt_t(q, sm, cmp_p, kvs, kvw).T
    gates_c = sm[:, g0:g0 + 2 * NH]
    mh, c_p, n_p, m_p = _mlstm_prompt(mq, mk, mv, gates_c, gates_c.T, MLSTM_CHUNK)
    h1, xn_bf, qp = _mix_output(att, mh, mo, xp, ga, gm, w_out_bf, nf, w_q_bf, 256)
    y_p = _channel_mix_and_norm(h1, xn_bf, qp, sub_keys, u_bf, v_bf, nfin, 256, PEER_TM, PEER_ET)
    w_keep = min(WINDOW, t)
    outs_p = (kvc.reshape((1, 1, t) + row), kvs.reshape((1, 1, t) + row),
              kvw[t - w_keep:].reshape((1, 1, w_keep) + row),
              c_p[None, None], n_p[:NH][None, None], m_p[:NH, 0][None, None])

    xs = x_sample.reshape(nb, d)
    q, kvc_s, kvs_s, kvw_s, mq, mk, mv, mo, sm = _project(xs, nm, w_re, bias, nb)
    cmp_s = _compress_paged(page_table, jnp.transpose(cache_cmp_kv[0], (0, 2, 3, 4, 1)), cw)
    pool_s = _pool_matrix(past // CMP_STRIDE, past // SLC_BLOCK)
    o_c, ids = _nsa_sample_cmp(q, cmp_s, pool_s, past)
    win_buf = cache_win_kv[0].reshape(nb, w_buf, KV_COLS)
    att = _nsa_sample_sel(ids[:, :, :N_SELECT].reshape(-1), page_table,
                          jnp.transpose(cache_slc_kv[0], (0, 2, 3, 4, 1)), q, sm, kvs_s, kvw_s, win_buf, o_c, past)
    mh, c_s, n_s, m_s = _mlstm_sample(mq, mk, mv, sm[:, g0:g0 + 2 * NH], state_mlstm_C[0], state_mlstm_n[0],
                                      state_mlstm_m[0])
    h1, xn_bf, qp = _mix_output(att, mh, mo, xs, ga, gm, w_out_bf, nf, w_q_bf, nb)
    padr = lambda a: jnp.pad(a, ((0, SAMPLE_PAD - nb), (0, 0)))
    y_s = _channel_mix_and_norm(padr(h1), padr(xn_bf), padr(qp), sub_keys, u_bf, v_bf, nfin, SAMPLE_PAD,
                                SAMPLE_PAD, PEER_ET)[:nb]
    win_all = jnp.concatenate([win_buf, kvw_s[:, None, :]], axis=1)
    w_keep_s = min(WINDOW, w_buf + 1)
    outs_s = (kvc_s.reshape((1, nb, 1) + row), kvs_s.reshape((1, nb, 1) + row),
              win_all[:, w_buf + 1 - w_keep_s:].reshape((1, nb, w_keep_s) + row),
              c_s[None], n_s[None], m_s[None])

    return (y_p.reshape(1, t, d), y_s.reshape(nb, 1, d),
            outs_p[0], outs_s[0], outs_p[1], outs_s[1], outs_p[2], outs_s[2],
            outs_p[3], outs_s[3], outs_p[4], outs_s[4], outs_p[5], outs_s[5])
```

```python
import functools

import jax
import jax.numpy as jnp
import numpy as np
from jax import lax
from jax.experimental import pallas as pl
from jax.experimental.pallas import tpu as pltpu

f32 = jnp.float32
bf16 = jnp.bfloat16
HIGHEST = lax.Precision.HIGHEST

D_MODEL = 1024
HEAD_DIM = 64
N_ATT_HEADS = 8
N_KV_HEADS = 2
GQA_REP = 4
D_ATT = 512
KV_COLS = 256
CMP_BLOCK = 32
CMP_STRIDE = 16
CMP_HIDDEN = 128
SLC_BLOCK = 64
N_SELECT = 16
WINDOW = 512
Q_BLOCK = 128
FORCE_BONUS = 1000.0
N_MLSTM_HEADS = 4
MLSTM_HEAD_DIM = 128
D_MLSTM = 512
N_GATE_COLS = 3 * N_ATT_HEADS
SPLIT_SIZES = (D_ATT, KV_COLS, KV_COLS, KV_COLS, N_GATE_COLS, D_MLSTM, D_MLSTM, D_MLSTM, D_MLSTM,
               N_MLSTM_HEADS, N_MLSTM_HEADS)
SMALL_COLS = 128
PEER_N_KEYS = 128
PEER_HEADS = 8
PEER_TOPK = 16
PEER_D_KEY = 128
NORM_EPS = 1e-6
NEG_INF = -1e30
VMEM_LIMIT = 56 * 1024 * 1024


def _cparams(*sem):
    return pltpu.CompilerParams(dimension_semantics=sem, vmem_limit_bytes=VMEM_LIMIT)


def _rms(x, w):
    return x * lax.rsqrt(jnp.mean(x * x, axis=-1, keepdims=True) + NORM_EPS) * w


_P_Q, _P_KVC, _P_KVS, _P_KVW, _P_MQ, _P_MK, _P_MV, _P_MO, _P_SM, _P_END = (
    0, 512, 768, 1024, 1280, 1792, 2304, 2816, 3328, 3456)


def _relayout_w_in(w_in):
    pts = np.cumsum(SPLIT_SIZES)[:-1].tolist()
    q, kc, ks, kw, ga, mq, mk, mv, mo, ig, fg = jnp.split(w_in, pts, axis=-1)
    pad = jnp.zeros((w_in.shape[0], SMALL_COLS - N_GATE_COLS - 2 * N_MLSTM_HEADS), w_in.dtype)
    return jnp.concatenate([q, kc, ks, kw, mq, mk, mv, mo, ga, ig, fg, pad], axis=-1).astype(bf16)


def _bias_row(b_i, b_f):
    pad = jnp.zeros((SMALL_COLS - N_GATE_COLS - 2 * N_MLSTM_HEADS,), f32)
    return jnp.concatenate([jnp.zeros((N_GATE_COLS,), f32), b_i, b_f, pad])[None, :]


def _proj_kernel(x_ref, nw_ref, w_ref, b_ref, q_ref, kvc_ref, kvs_ref, kvw_ref, mq_ref, mk_ref, mv_ref,
                 mo_ref, sm_ref):
    xb = _rms(x_ref[...], nw_ref[...]).astype(bf16)

    def mm(lo, hi):
        return jnp.dot(xb, w_ref[:, lo:hi], preferred_element_type=f32)

    q_ref[...] = mm(_P_Q, _P_KVC)
    kvc_ref[...] = mm(_P_KVC, _P_KVS)
    kvs_ref[...] = mm(_P_KVS, _P_KVW)
    kvw_ref[...] = mm(_P_KVW, _P_MQ)
    mq_ref[...] = mm(_P_MQ, _P_MK)
    mk_ref[...] = mm(_P_MK, _P_MV)
    mv_ref[...] = mm(_P_MV, _P_MO)
    mo_ref[...] = jax.nn.sigmoid(mm(_P_MO, _P_SM))
    s = mm(_P_SM, _P_END) + b_ref[...]
    col = lax.broadcasted_iota(jnp.int32, s.shape, 1)
    sm_ref[...] = jnp.where(col < N_GATE_COLS, jax.nn.sigmoid(s), s)


def _project(x, norm_w, w_re, bias_row, tm):
    n = x.shape[0]
    widths = (D_ATT, KV_COLS, KV_COLS, KV_COLS, D_MLSTM, D_MLSTM, D_MLSTM, D_MLSTM, SMALL_COLS)
    return pl.pallas_call(
        _proj_kernel,
        grid=(n // tm,),
        in_specs=[pl.BlockSpec((tm, D_MODEL), lambda i: (i, 0)),
                  pl.BlockSpec((1, D_MODEL), lambda i: (0, 0)),
                  pl.BlockSpec((D_MODEL, _P_END), lambda i: (0, 0)),
                  pl.BlockSpec((1, SMALL_COLS), lambda i: (0, 0))],
        out_specs=[pl.BlockSpec((tm, w), lambda i: (i, 0)) for w in widths],
        out_shape=[jax.ShapeDtypeStruct((n, w), f32) for w in widths],
        compiler_params=_cparams("parallel"),
        name="proj",
    )(x, norm_w, w_re, bias_row)


_HID_COLS = 2 * N_KV_HEADS * CMP_HIDDEN
PAGE_ROWS = 128


def _compress_weights(pe, w1, w2):
    eye = jnp.eye(2, dtype=f32)

    def big1(w):
        return jnp.einsum('cldh,gf->clfdgh', w, eye).reshape(2, CMP_STRIDE * KV_COLS // 2, _HID_COLS // 2).astype(bf16)

    def pe_rows(p):
        row = jnp.broadcast_to(p.transpose(1, 0, 2)[:, :, None, :], (2, CMP_STRIDE, N_KV_HEADS, HEAD_DIM))
        row = row.reshape(2, 1, -1)
        return jnp.concatenate([row, jnp.zeros((2, 7, row.shape[-1]), f32)], axis=1)

    w2_big = jnp.einsum('chd,ce,gf->efhcgd', w2, eye, eye).reshape(_HID_COLS, KV_COLS).astype(bf16)
    pe_all = jnp.stack([pe_rows(pe[:CMP_STRIDE]), pe_rows(pe[CMP_STRIDE:])], axis=0).astype(bf16)
    return big1(w1[:, :CMP_STRIDE]), big1(w1[:, CMP_STRIDE:]), pe_all, w2_big


_CMP_ROWS = 256


def _compress_math(x_lo, x_hi, w1a_ref, w1b_ref, pe_ref, w2_ref):
    n_chunks = x_lo.shape[0] // CMP_STRIDE
    step = min(_CMP_ROWS, n_chunks)
    first, second = [], []
    for c0 in range(0, n_chunks, step):
        f_kv, s_kv = [], []
        for kv, x_ref in enumerate((x_lo, x_hi)):
            x = jnp.concatenate([x_ref[pl.ds(c0 * CMP_STRIDE + l, step, stride=CMP_STRIDE), :]
                                 for l in range(CMP_STRIDE)], axis=1).astype(bf16)
            f_kv.append(jnp.dot(x, w1a_ref[kv], preferred_element_type=f32))
            s_kv.append(jnp.dot(x, w1b_ref[kv], preferred_element_type=f32))
        first.append(jnp.concatenate(f_kv, axis=1))
        second.append(jnp.concatenate(s_kv, axis=1))
    first = jnp.concatenate(first, axis=0)
    second = jnp.concatenate(second, axis=0)
    pe_term = jnp.concatenate(
        [(jnp.dot(pe_ref[0, kv], w1a_ref[kv], preferred_element_type=f32)
          + jnp.dot(pe_ref[1, kv], w1b_ref[kv], preferred_element_type=f32))[0:1] for kv in range(2)], axis=1)
    h = jax.nn.gelu(first + pltpu.roll(second, n_chunks - 1, 0) + pe_term)
    return jnp.dot(h.astype(bf16), w2_ref[...], preferred_element_type=f32)


def _compress_kernel(x_lo, x_hi, w1a_ref, w1b_ref, pe_ref, w2_ref, o_ref):
    o_ref[0] = _compress_math(x_lo.at[0], x_hi.at[0], w1a_ref, w1b_ref, pe_ref, w2_ref)


def _compress_paged_kernel(pt_ref, pool_ref, w1a_ref, w1b_ref, pe_ref, w2_ref, o_ref, raw_s, x_s, sem):
    b = pl.program_id(0)
    n_pages = raw_s.shape[0]
    lanes = N_KV_HEADS * HEAD_DIM

    def page_copy(seq, p):
        return pltpu.make_async_copy(pool_ref.at[pt_ref[seq * n_pages + p]], raw_s.at[p], sem)

    def start_pages(seq):
        def start(p, c):
            page_copy(seq, p).start()
            return c
        lax.fori_loop(0, n_pages, start, 0)

    def wait(p, c):
        page_copy(b, p).wait()
        return c

    def to_rows(p, c):
        r0 = pl.multiple_of(p * PAGE_ROWS, PAGE_ROWS)
        for kv in range(2):
            x_s[kv, pl.ds(r0, PAGE_ROWS), :] = raw_s[p, kv].reshape(lanes, PAGE_ROWS).T
        return c

    @pl.when(b == 0)
    def _():
        start_pages(b)

    lax.fori_loop(0, n_pages, wait, 0)
    lax.fori_loop(0, n_pages, to_rows, 0, unroll=4)

    @pl.when(b + 1 < pl.num_programs(0))
    def _():
        start_pages(b + 1)

    o_ref[0] = _compress_math(x_s.at[0], x_s.at[1], w1a_ref, w1b_ref, pe_ref, w2_ref)


def _compress_paged(page_table, pool_t, cw):
    nb, n_pages = page_table.shape
    n_chunks = n_pages * PAGE_ROWS // CMP_STRIDE
    full = lambda a: pl.BlockSpec(a.shape, lambda b, pt: (0,) * a.ndim, pipeline_mode=pl.Buffered(1))
    return pl.pallas_call(
        _compress_paged_kernel,
        grid_spec=pltpu.PrefetchScalarGridSpec(
            num_scalar_prefetch=1,
            grid=(nb,),
            in_specs=[pl.BlockSpec(memory_space=pl.ANY)] + [full(a) for a in cw],
            out_specs=pl.BlockSpec((1, n_chunks, KV_COLS), lambda b, pt: (b, 0, 0)),
            scratch_shapes=[pltpu.VMEM((n_pages,) + pool_t.shape[1:], f32),
                            pltpu.VMEM((2, n_pages * PAGE_ROWS, KV_COLS // 2), f32),
                            pltpu.SemaphoreType.DMA(())]),
        out_shape=jax.ShapeDtypeStruct((nb, n_chunks, KV_COLS), f32),
        compiler_params=_cparams("arbitrary"),
        name="compress_paged",
    )(page_table.reshape(-1), pool_t, *cw)


def _compress_prompt(kv_rows, cw):
    nb, n_rows, _ = kv_rows.shape
    n_chunks = n_rows // CMP_STRIDE
    full = lambda a: pl.BlockSpec(a.shape, lambda b: (0,) * a.ndim)
    return pl.pallas_call(
        _compress_kernel,
        grid=(nb,),
        in_specs=[pl.BlockSpec((1, n_rows, KV_COLS // 2), lambda b: (b, 0, 0)),
                  pl.BlockSpec((1, n_rows, KV_COLS // 2), lambda b: (b, 0, 1))] + [full(a) for a in cw],
        out_specs=pl.BlockSpec((1, n_chunks, KV_COLS), lambda b: (b, 0, 0)),
        out_shape=jax.ShapeDtypeStruct((nb, n_chunks, KV_COLS), f32),
        compiler_params=_cparams("parallel"),
        name="compress_prompt",
    )(kv_rows, kv_rows, *cw)


def _dot_t0(a, b, **kw):
    return lax.dot_general(a, b, (((0,), (0,)), ((), ())), preferred_element_type=f32, **kw)


def _dot_nt(a, b, **kw):
    return lax.dot_general(a, b, (((1,), (1,)), ((), ())), preferred_element_type=f32, **kw)


def _mlstm_chunk_kernel(q_ref, k_ref, v_ref, gc_ref, gr_ref, h_ref, c_out, n_out, m_out, c_s, n_s, m_s):
    L = q_ref.shape[0]
    NH, DH = N_MLSTM_HEADS, MLSTM_HEAD_DIM

    @pl.when(pl.program_id(0) == 0)
    def _():
        c_s[...] = jnp.zeros_like(c_s)
        n_s[...] = jnp.zeros_like(n_s)
        m_s[...] = jnp.zeros_like(m_s)

    row = lax.broadcasted_iota(jnp.int32, (L, L), 0)
    col = lax.broadcasted_iota(jnp.int32, (L, L), 1)
    causal = col <= row
    gc = gc_ref[...]
    gr = gr_ref[...]
    lf_c = jax.nn.log_sigmoid(gc[:, NH:2 * NH])
    lf_r = jax.nn.log_sigmoid(gr[NH:2 * NH, :])
    f_c = jnp.dot(causal.astype(f32), lf_c, preferred_element_type=f32, precision=HIGHEST)
    f_r = jnp.dot(lf_r, (row <= col).astype(f32), preferred_element_type=f32, precision=HIGHEST)
    for h in range(NH):
        sl = slice(h * DH, (h + 1) * DH)
        fc, fr = f_c[:, h:h + 1], f_r[h:h + 1, :]
        ic, ir = gc[:, h:h + 1], gr[h:h + 1, :]
        m_prev = m_s[h:h + 1, 0:1]
        qh = q_ref[:, sl].astype(bf16)
        kh = k_ref[:, sl] * (DH ** -0.5)
        vh = v_ref[:, sl].astype(bf16)
        log_d = fc - fr + ir
        m_t = jnp.maximum(fc + m_prev, jnp.max(jnp.where(causal, log_d, NEG_INF), axis=1, keepdims=True))
        w = jnp.where(causal, jnp.exp(log_d - m_t), 0.0) * _dot_nt(qh, kh.astype(bf16))
        inter = jnp.exp(fc + m_prev - m_t)
        c_old = c_s[h]
        n_old = n_s[h:h + 1, :]
        num = (jnp.dot(w.astype(bf16), vh, preferred_element_type=f32)
               + inter * jnp.dot(qh, c_old.astype(bf16), preferred_element_type=f32))
        den = (jnp.sum(w, axis=1, keepdims=True)
               + inter * jnp.sum(q_ref[:, sl] * n_old, axis=1, keepdims=True))
        h_ref[:, sl] = num / jnp.maximum(jnp.abs(den), jnp.exp(-m_t))
        f_tot = fc[L - 1:L, :]
        m_new = m_t[L - 1:L, :]
        kw = kh * jnp.exp(f_tot - fc + ic - m_new)
        decay = jnp.exp(f_tot + m_prev - m_new)
        c_new = decay * c_old + _dot_t0(kw.astype(bf16), vh)
        n_new = decay * n_old + jnp.sum(kw, axis=0, keepdims=True)
        c_s[h] = c_new
        n_s[h:h + 1, :] = n_new
        m_s[h:h + 1, :] = jnp.broadcast_to(m_new, (1, DH))
        c_out[h] = c_new
    n_out[...] = n_s[...]
    m_out[...] = m_s[...]


def _mlstm_prompt(mq, mk, mv, gates_c, gates_r, chunk):
    t = mq.shape[0]
    NH, DH = N_MLSTM_HEADS, MLSTM_HEAD_DIM
    tok = lambda w: pl.BlockSpec((chunk, w), lambda c: (c, 0))
    return pl.pallas_call(
        _mlstm_chunk_kernel,
        grid=(t // chunk,),
        in_specs=[tok(D_MLSTM), tok(D_MLSTM), tok(D_MLSTM), tok(2 * NH),
                  pl.BlockSpec((2 * NH, chunk), lambda c: (0, c))],
        out_specs=[tok(D_MLSTM),
                   pl.BlockSpec((NH, DH, DH), lambda c: (0, 0, 0)),
                   pl.BlockSpec((8, DH), lambda c: (0, 0)),
                   pl.BlockSpec((8, DH), lambda c: (0, 0))],
        out_shape=[jax.ShapeDtypeStruct((t, D_MLSTM), f32),
                   jax.ShapeDtypeStruct((NH, DH, DH), f32),
                   jax.ShapeDtypeStruct((8, DH), f32),
                   jax.ShapeDtypeStruct((8, DH), f32)],
        scratch_shapes=[pltpu.VMEM((NH, DH, DH), f32), pltpu.VMEM((8, DH), f32), pltpu.VMEM((8, DH), f32)],
        compiler_params=_cparams("arbitrary"),
        name="mlstm_prompt",
    )(mq, mk, mv, gates_c, gates_r)


def _mlstm_step_kernel(q_ref, k_ref, v_ref, g_ref, c_ref, n_ref, m_ref, h_ref, c_out, n_out, m_out):
    NH, DH = N_MLSTM_HEADS, MLSTM_HEAD_DIM
    b = pl.program_id(0)
    row8 = lax.broadcasted_iota(jnp.int32, (8, DH), 0)
    g = g_ref[pl.ds(b, 1), :]
    m_row = m_ref[pl.ds(b, 1), :]
    m_new_row = jnp.zeros((1, NH), f32)
    lane4 = lax.broadcasted_iota(jnp.int32, (1, NH), 1)
    q_row, k_row, v_row = q_ref[pl.ds(b, 1), :], k_ref[pl.ds(b, 1), :], v_ref[pl.ds(b, 1), :]
    h_parts = []
    for h in range(NH):
        sl = slice(h * DH, (h + 1) * DH)
        q = q_row[:, sl]
        k = k_row[:, sl] * (DH ** -0.5)
        v = v_row[:, sl]
        ig = g[:, h:h + 1]
        lf = jax.nn.log_sigmoid(g[:, NH + h:NH + h + 1])
        m_prev = m_row[:, h:h + 1]
        c_old = c_ref[0, h]
        n_old = n_ref[0, h:h + 1, :]
        m_t = jnp.maximum(lf + m_prev, ig)
        w = jnp.exp(ig - m_t) * jnp.sum(q * k, axis=1, keepdims=True)
        inter = jnp.exp(lf + m_prev - m_t)
        q8 = jnp.where(row8 == 0, q, 0.0)
        qc = jnp.dot(q8, c_old, preferred_element_type=f32, precision=HIGHEST)[0:1]
        num = w * v + inter * qc
        den = w + inter * jnp.sum(q * n_old, axis=1, keepdims=True)
        h_parts.append(num / jnp.maximum(jnp.abs(den), jnp.exp(-m_t)))
        kw = k * jnp.exp(ig - m_t)
        decay = jnp.exp(lf + m_prev - m_t)
        kw8 = jnp.where(row8 == 0, kw, 0.0)
        v8 = jnp.where(row8 == 0, v, 0.0)
        c_out[0, h] = decay * c_old + _dot_t0(kw8, v8, precision=HIGHEST)
        n_out[0, h:h + 1, :] = decay * n_old + kw
        m_new_row = jnp.where(lane4 == h, m_t, m_new_row)
    h_ref[pl.ds(b, 1), :] = jnp.concatenate(h_parts, axis=1)
    m_out[pl.ds(b, 1), :] = m_new_row


def _mlstm_sample(mq, mk, mv, gates, c0, n0, m0):
    nb = mq.shape[0]
    NH, DH = N_MLSTM_HEADS, MLSTM_HEAD_DIM
    full = lambda a: pl.BlockSpec(a.shape, lambda b: (0,) * a.ndim)
    return pl.pallas_call(
        _mlstm_step_kernel,
        grid=(nb,),
        in_specs=[full(mq), full(mk), full(mv), full(gates),
                  pl.BlockSpec((1, NH, DH, DH), lambda b: (b, 0, 0, 0)),
                  pl.BlockSpec((1, NH, DH), lambda b: (b, 0, 0)),
                  full(m0)],
        out_specs=[pl.BlockSpec((nb, D_MLSTM), lambda b: (0, 0)),
                   pl.BlockSpec((1, NH, DH, DH), lambda b: (b, 0, 0, 0)),
                   pl.BlockSpec((1, NH, DH), lambda b: (b, 0, 0)),
                   pl.BlockSpec((nb, NH), lambda b: (0, 0))],
        out_shape=[jax.ShapeDtypeStruct((nb, D_MLSTM), f32),
                   jax.ShapeDtypeStruct((nb, NH, DH, DH), f32),
                   jax.ShapeDtypeStruct((nb, NH, DH), f32),
                   jax.ShapeDtypeStruct((nb, NH), f32)],
        compiler_params=_cparams("arbitrary"),
        name="mlstm_sample",
    )(mq, mk, mv, gates, c0, n0, m0)


SLC_TILE = 512


def _masked_softmax(s, mask):
    sm = jnp.where(mask, s, NEG_INF)
    mx = jnp.max(sm, axis=-1, keepdims=True)
    e = jnp.exp(sm - mx)
    p = e / jnp.sum(e, axis=-1, keepdims=True)
    return jnp.where(mx > 0.5 * NEG_INF, p, 0.0)


def _pool_matrix(n_cmp_rows, n_slc):
    i = np.arange(n_cmp_rows)[:, None]
    j = np.arange(n_slc)[None, :]
    ratio = SLC_BLOCK // CMP_STRIDE
    return jnp.asarray(((i >= ratio * j - 1) & (i <= ratio * j + ratio - 1)).astype(np.float32))


def _select_blocks(p_slc, cur, n_valid_lanes):
    nb = p_slc.shape[1]
    blk = lax.broadcasted_iota(jnp.int32, p_slc.shape, 1)
    valid = blk <= cur
    forced = (blk == 0) | (blk == cur) | (blk == cur - 1)
    score = jnp.where(valid, p_slc + jnp.where(forced, FORCE_BONUS, 0.0), -1.0)
    score = jnp.where(blk < n_valid_lanes, score, -2.0)
    blk_f = blk.astype(f32)
    sel = jnp.zeros(p_slc.shape, f32)
    picks = []
    for _ in range(N_SELECT):
        mx = jnp.max(score, axis=1, keepdims=True)
        idx = jnp.min(jnp.where(score == mx, blk_f, float(nb)), axis=1, keepdims=True)
        hit = blk_f == idx
        sel = jnp.where(hit, 1.0, sel)
        score = jnp.where(hit, -3.0, score)
        picks.append(idx.astype(jnp.int32))
    return sel, picks


_SUB = 128
_M_FLOOR = -1e20


def _with_pos_feature(k, index0=0):
    n = k.shape[0]
    pos = ((jnp.arange(n) + index0) % _SUB).astype(f32)[:, None]
    return jnp.concatenate([k, pos, jnp.zeros((n, _SUB - HEAD_DIM - 1), f32)], axis=1).astype(bf16)


def _colmax8(x):
    out = x[0:8]
    for r in range(8, x.shape[0], 8):
        out = jnp.maximum(out, x[r:r + 8])
    return out


def _colsum8(x):
    out = x[0:8]
    for r in range(8, x.shape[0], 8):
        out = out + x[r:r + 8]
    return out


_PSUM_PAD = 8
_CMP_UNROLL = 2


def _nsa_prompt_t_kernel(qT_ref, gT_ref, kc_ref, vcT_ref, ks_ref, vsT_ref, kw_ref, vwT_ref, o_ref,
                         sc_s, psum_s, sel_s, m_s, l_s, acc_s):
    tq = qT_ref.shape[1]
    L = GQA_REP * tq
    n_cmp = kc_ref.shape[1]
    n_slc = sel_s.shape[0]
    assert tq == _SUB and n_cmp * CMP_STRIDE == n_slc * SLC_BLOCK
    i = pl.program_id(0)
    start = pl.multiple_of(i * tq, tq)
    lane = lax.broadcasted_iota(jnp.int32, (1, L), 1)
    t_lane = lane % tq
    qpos = start + t_lane
    qpos_f = qpos.astype(f32)
    qpos_t = qpos[:, 0:tq]
    sub_l = lax.broadcasted_iota(jnp.int32, (_SUB, L), 0)
    sub_t = lax.broadcasted_iota(jnp.int32, (_SUB, tq), 0)
    feat_row = lax.broadcasted_iota(jnp.int32, (_SUB - HEAD_DIM, L), 0) == 0
    n_tiles = (start + tq + SLC_TILE - 1) // SLC_TILE
    nc_blocks = (start // CMP_STRIDE + (tq - CMP_BLOCK) // CMP_STRIDE) // _SUB + 1
    nc_trips = (nc_blocks + _CMP_UNROLL - 1) // _CMP_UNROLL
    assert (n_cmp // _SUB) % _CMP_UNROLL == 0
    blocks_per_tile = SLC_TILE // SLC_BLOCK

    for g in range(N_KV_HEADS):
        vrows = slice(g * HEAD_DIM, (g + 1) * HEAD_DIM)
        slope = jnp.exp2(-8.0 * (GQA_REP * g + 1 + lane // tq).astype(f32) / N_ATT_HEADS)
        q_rows = jnp.concatenate(
            [qT_ref[(GQA_REP * g + r) * HEAD_DIM:(GQA_REP * g + r + 1) * HEAD_DIM, :] for r in range(GQA_REP)],
            axis=1).astype(f32) * (HEAD_DIM ** -0.5)
        q_pos = jnp.concatenate([q_rows, jnp.where(feat_row, slope, 0.0)], axis=0).astype(bf16)
        q_cmp = jnp.concatenate([q_rows, jnp.where(feat_row, slope * CMP_STRIDE, 0.0)], axis=0).astype(bf16)

        def cmp_scores(ct, m8):
            r0s = [pl.multiple_of((ct * _CMP_UNROLL + u) * _SUB, _SUB) for u in range(_CMP_UNROLL)]
            dots = [jnp.dot(kc_ref[g, pl.ds(r0, _SUB), :], q_cmp, preferred_element_type=f32) for r0 in r0s]
            for r0, s in zip(r0s, dots):
                end0 = r0 * CMP_STRIDE + (CMP_BLOCK - 1)
                off = slope * (end0.astype(f32) - qpos_f)
                vis = (end0 + sub_l * CMP_STRIDE) <= qpos
                s = jnp.where(vis, s + off, NEG_INF)
                sc_s[pl.ds(r0, _SUB), :] = s
                m8 = jnp.maximum(m8, _colmax8(s))
            return m8

        m8 = lax.fori_loop(0, nc_trips, cmp_scores, jnp.full((8, L), NEG_INF, f32))
        m_c = jnp.maximum(jnp.max(m8, axis=0, keepdims=True), _M_FLOOR)

        def cmp_exp(ct, carry):
            l8, o_acc = carry
            for u in range(_CMP_UNROLL):
                r0 = pl.multiple_of((ct * _CMP_UNROLL + u) * _SUB, _SUB)
                e = jnp.exp(sc_s[pl.ds(r0, _SUB), :] - m_c)
                sc_s[pl.ds(r0, _SUB), :] = e
                o_acc = o_acc + jnp.dot(vcT_ref[vrows, pl.ds(r0, _SUB)], e.astype(bf16),
                                        preferred_element_type=f32)
                l8 = l8 + _colsum8(e)
            return l8, o_acc

        l8, o_c = lax.fori_loop(0, nc_trips, cmp_exp,
                                (jnp.zeros((8, L), f32), jnp.zeros((HEAD_DIM, L), f32)))
        l_c = jnp.sum(l8, axis=0, keepdims=True)
        inv_c = jnp.where(l_c > 0.0, 1.0 / jnp.where(l_c > 0.0, l_c, 1.0), 0.0)
        o_c = o_c * inv_c
        psum_s[...] = jnp.zeros_like(psum_s)

        def cmp_group_sum(ct, carry):
            for u in range(_CMP_UNROLL):
                r0 = pl.multiple_of((ct * _CMP_UNROLL + u) * _SUB, _SUB)
                p = sc_s[pl.ds(r0, _SUB), :] * inv_c
                psum_s[pl.ds(r0 + _PSUM_PAD, _SUB), :] = ((p[:, 0:tq] + p[:, tq:2 * tq])
                                                          + (p[:, 2 * tq:3 * tq] + p[:, 3 * tq:4 * tq]))
            return carry

        lax.fori_loop(0, nc_trips, cmp_group_sum, 0)
        ratio = SLC_BLOCK // CMP_STRIDE
        p_slc = psum_s[pl.ds(_PSUM_PAD - 1, n_slc, stride=ratio), :]
        for c in range(ratio):
            p_slc = p_slc + psum_s[pl.ds(_PSUM_PAD + c, n_slc, stride=ratio), :]

        blk = lax.broadcasted_iota(jnp.int32, (n_slc, tq), 0).astype(f32)
        cur = (qpos_t // SLC_BLOCK).astype(f32)
        forced = (blk == 0.0) | (blk == cur) | (blk == cur - 1.0)
        score = jnp.where(blk <= cur, p_slc + jnp.where(forced, FORCE_BONUS, 0.0), -1.0)
        sel = jnp.zeros((n_slc, tq), f32)
        for _ in range(N_SELECT):
            mx = jnp.max(score, axis=0, keepdims=True)
            idx = jnp.min(jnp.where(score == mx, blk, float(n_slc)), axis=0, keepdims=True)
            hit = blk == idx
            sel = jnp.where(hit, 1.0, sel)
            score = jnp.where(hit, -3.0, score)
        sel_s[...] = sel

        per_blk = _SUB // SLC_BLOCK
        m_s[...] = jnp.full(m_s.shape, NEG_INF, f32)
        l_s[...] = jnp.zeros(l_s.shape, f32)
        acc_s[...] = jnp.zeros(acc_s.shape, f32)
        lowest = jnp.min(jnp.where((sel > 0.5) & (blk >= 1.0), blk, float(n_slc))).astype(jnp.int32)

        def slc_tiles(tiles, live):
            k0s = [pl.multiple_of(j * SLC_TILE, SLC_TILE) for j in tiles]
            dots = [jnp.dot(ks_ref[g, pl.ds(k0, SLC_TILE), :], q_pos, preferred_element_type=f32) for k0 in k0s]
            parts = []
            for j, k0, s, alive in zip(tiles, k0s, dots, live):
                picked = sel_s[pl.ds(pl.multiple_of(j * blocks_per_tile, blocks_per_tile), blocks_per_tile), :]
                for w in range(SLC_TILE // _SUB):
                    base = k0 + w * _SUB
                    picked_k = jnp.concatenate(
                        [jnp.broadcast_to(picked[per_blk * w + u:per_blk * w + u + 1, :], (SLC_BLOCK, tq))
                         for u in range(per_blk)], axis=0)
                    ok = (picked_k > 0.5) & (base + sub_t <= qpos_t)
                    bias = jnp.where(ok, 0.0, NEG_INF)
                    bias = jnp.concatenate([bias] * GQA_REP, axis=1) + slope * (base.astype(f32) - qpos_f)
                    if alive is not True:
                        bias = bias + jnp.where(alive, 0.0, NEG_INF)
                    parts.append(s[w * _SUB:(w + 1) * _SUB] + bias)
            s = jnp.concatenate(parts, axis=0)
            m_old = m_s[...]
            m_new = jnp.maximum(jnp.maximum(m_old, jnp.max(_colmax8(s), axis=0, keepdims=True)), _M_FLOOR)
            p = jnp.exp(s - m_new)
            alpha = jnp.exp(m_old - m_new)
            l_s[...] = alpha * l_s[...] + jnp.sum(_colsum8(p), axis=0, keepdims=True)
            acc = alpha * acc_s[...]
            for i, k0 in enumerate(k0s):
                acc = acc + jnp.dot(vsT_ref[vrows, pl.ds(k0, SLC_TILE)],
                                    p[i * SLC_TILE:(i + 1) * SLC_TILE].astype(bf16), preferred_element_type=f32)
            acc_s[...] = acc
            m_s[...] = m_new

        def slc_pair(pi, carry):
            ja = jnp.where(pi == 0, 0, first_tile + 2 * pi - 1)
            jb = first_tile + 2 * pi
            slc_tiles([ja, jnp.minimum(jb, n_tiles - 1)], [True, jb < n_tiles])
            return carry

        first_tile = jnp.maximum(lowest // blocks_per_tile, 1)
        n_swept = 1 + jnp.maximum(n_tiles - first_tile, 0)
        lax.fori_loop(0, (n_swept + 1) // 2, slc_pair, 0)
        l_fin = l_s[...]
        o_s = acc_s[...] * jnp.where(l_fin > 0.0, 1.0 / jnp.where(l_fin > 0.0, l_fin, 1.0), 0.0)

        n_win_sub = (WINDOW + tq) // _SUB
        s_parts = []
        for w in range(n_win_sub):
            base = start - WINDOW + w * _SUB
            s = jnp.dot(kw_ref[g, pl.ds(start + w * _SUB, _SUB), :], q_pos, preferred_element_type=f32)
            if w == 0:
                ok = sub_l > t_lane
            elif w == n_win_sub - 1:
                ok = sub_l <= t_lane
            else:
                ok = None
            bias = slope * (base.astype(f32) - qpos_f) + jnp.where(base >= 0, 0.0, NEG_INF)
            s = s + bias
            s_parts.append(s if ok is None else jnp.where(ok, s, NEG_INF))
        m8 = _colmax8(s_parts[0])
        for s in s_parts[1:]:
            m8 = jnp.maximum(m8, _colmax8(s))
        m_w = jnp.maximum(jnp.max(m8, axis=0, keepdims=True), _M_FLOOR)
        l8 = jnp.zeros((8, L), f32)
        o_w = jnp.zeros((HEAD_DIM, L), f32)
        for w, s in enumerate(s_parts):
            e = jnp.exp(s - m_w)
            l8 = l8 + _colsum8(e)
            o_w = o_w + jnp.dot(vwT_ref[vrows, pl.ds(start + w * _SUB, _SUB)], e.astype(bf16),
                                preferred_element_type=f32)
        l_w = jnp.sum(l8, axis=0, keepdims=True)
        o_w = o_w * jnp.where(l_w > 0.0, 1.0 / jnp.where(l_w > 0.0, l_w, 1.0), 0.0)

        for r in range(GQA_REP):
            head = GQA_REP * g + r
            cols = slice(r * tq, (r + 1) * tq)
            o_ref[head * HEAD_DIM:(head + 1) * HEAD_DIM, :] = (
                gT_ref[3 * head:3 * head + 1, :] * o_c[:, cols]
                + gT_ref[3 * head + 1:3 * head + 2, :] * o_s[:, cols]
                + gT_ref[3 * head + 2:3 * head + 3, :] * o_w[:, cols])


def _nsa_prompt_t(q, small, kv_cmp, kvs, kvw):
    t = q.shape[0]
    n_cmp = kv_cmp.shape[0]
    n_slc = t // SLC_BLOCK
    kcol = lambda g: slice(g * HEAD_DIM, (g + 1) * HEAD_DIM)
    vT = lambda a: a[:, N_KV_HEADS * HEAD_DIM:].T.astype(bf16)
    qT = q.T.astype(bf16)
    gT = small[:, :32].T
    kc = jnp.stack([_with_pos_feature(kv_cmp[:, kcol(g)]) for g in range(N_KV_HEADS)])
    ks = jnp.stack([_with_pos_feature(kvs[:, kcol(g)]) for g in range(N_KV_HEADS)])
    kvw_pad = jnp.pad(kvw, ((WINDOW, 0), (0, 0)))
    kw = jnp.stack([_with_pos_feature(kvw_pad[:, kcol(g)]) for g in range(N_KV_HEADS)])
    operands = (qT, gT, kc, vT(kv_cmp), ks, vT(kvs), kw, vT(kvw_pad))
    L = GQA_REP * Q_BLOCK
    const = lambda a: pl.BlockSpec(a.shape, lambda i: (0,) * a.ndim, pipeline_mode=pl.Buffered(1))
    return pl.pallas_call(
        _nsa_prompt_t_kernel,
        grid=(t // Q_BLOCK,),
        in_specs=[pl.BlockSpec((D_ATT, Q_BLOCK), lambda i: (0, i)),
                  pl.BlockSpec((32, Q_BLOCK), lambda i: (0, i))] + [const(a) for a in operands[2:]],
        out_specs=pl.BlockSpec((D_ATT, Q_BLOCK), lambda i: (0, i)),
        out_shape=jax.ShapeDtypeStruct((D_ATT, t), f32),
        scratch_shapes=[pltpu.VMEM((n_cmp, L), f32), pltpu.VMEM((n_cmp + _PSUM_PAD, Q_BLOCK), f32),
                        pltpu.VMEM((n_slc, Q_BLOCK), f32), pltpu.VMEM((1, L), f32), pltpu.VMEM((1, L), f32),
                        pltpu.VMEM((HEAD_DIM, L), f32)],
        compiler_params=_cparams("parallel"),
        name="nsa_prompt",
    )(*operands)


_QROWS = 16
_IDS_LANES = 128


def _group_queries_1(q, g):
    rows = [q[:, (GQA_REP * g + r) * HEAD_DIM:(GQA_REP * g + r + 1) * HEAD_DIM] for r in range(GQA_REP)]
    rows.append(jnp.zeros((_QROWS - GQA_REP, HEAD_DIM), f32))
    return jnp.concatenate(rows, axis=0) * (HEAD_DIM ** -0.5)


def _group_slopes_1(g):
    r = jnp.minimum(lax.broadcasted_iota(jnp.int32, (_QROWS, 1), 0), GQA_REP - 1)
    return jnp.exp2(-8.0 * (GQA_REP * g + 1 + r).astype(f32) / N_ATT_HEADS)


def _nsa_sample_cmp_kernel(q_ref, cmp_ref, pool_ref, oc_ref, ids_ref, *, past):
    b = pl.program_id(0)
    n_cmp = cmp_ref.shape[1]
    n_old = pool_ref.shape[1]
    q = q_ref[pl.ds(b, 1), :]
    cmpb = cmp_ref[0].astype(bf16)
    d_c = past - (lax.broadcasted_iota(jnp.int32, (1, n_cmp), 1) * CMP_STRIDE + (CMP_BLOCK - 1))
    lane = lax.broadcasted_iota(jnp.int32, (1, _IDS_LANES), 1)
    cur = jnp.full((1, 1), past // SLC_BLOCK, jnp.int32)
    oc_parts = []
    for g in range(N_KV_HEADS):
        q16 = _group_queries_1(q, g).astype(bf16)
        s = _dot_nt(q16, cmpb[:, g * HEAD_DIM:(g + 1) * HEAD_DIM]) - _group_slopes_1(g) * d_c.astype(f32)
        p = _masked_softmax(s, jnp.broadcast_to(d_c >= 0, s.shape))
        o_c = jnp.dot(p.astype(bf16), cmpb[:, (N_KV_HEADS + g) * HEAD_DIM:(N_KV_HEADS + g + 1) * HEAD_DIM],
                      preferred_element_type=f32)
        oc_parts.extend(o_c[r:r + 1] for r in range(GQA_REP))
        p_sum = jnp.sum(p[0:GQA_REP], axis=0, keepdims=True)
        p_slc = jnp.dot(jnp.broadcast_to(p_sum, (8, n_cmp)), pool_ref[...], preferred_element_type=f32,
                        precision=HIGHEST)[0:1]
        p_ext = jnp.concatenate([p_slc, jnp.zeros((1, 128), f32)], axis=1)
        _, picks = _select_blocks(p_ext, cur, past // SLC_BLOCK + 1)
        row = jnp.zeros((1, _IDS_LANES), jnp.int32)
        for k, pick in enumerate(picks):
            row = jnp.where(lane == k, pick, row)
        ids_ref[0, g:g + 1, :] = row
    oc_ref[pl.ds(b, 1), :] = jnp.concatenate(oc_parts, axis=1)


def _nsa_sample_cmp(q, cmp_s, pool, past):
    nb = q.shape[0]
    return pl.pallas_call(
        functools.partial(_nsa_sample_cmp_kernel, past=past),
        grid=(nb,),
        in_specs=[pl.BlockSpec(q.shape, lambda b: (0, 0)),
                  pl.BlockSpec((1,) + cmp_s.shape[1:], lambda b: (b, 0, 0)),
                  pl.BlockSpec(pool.shape, lambda b: (0, 0))],
        out_specs=[pl.BlockSpec((nb, D_ATT), lambda b: (0, 0)),
                   pl.BlockSpec((1, N_KV_HEADS, _IDS_LANES), lambda b: (b, 0, 0))],
        out_shape=[jax.ShapeDtypeStruct((nb, D_ATT), f32),
                   jax.ShapeDtypeStruct((nb, N_KV_HEADS, _IDS_LANES), jnp.int32)],
        compiler_params=_cparams("arbitrary"),
        name="nsa_sample_cmp",
    )(q, cmp_s, pool)


def _softmax_with_new(s, mask, s_new, new_ok):
    s = jnp.where(mask, s, NEG_INF)
    s_new = jnp.where(new_ok, s_new, NEG_INF)
    m = jnp.maximum(jnp.max(s, axis=1, keepdims=True), s_new)
    p = jnp.where(mask, jnp.exp(s - m), 0.0)
    p_new = jnp.where(new_ok, jnp.exp(s_new - m), 0.0)
    l = jnp.sum(p, axis=1, keepdims=True) + p_new
    inv = jnp.where(l > 0.0, 1.0 / jnp.where(l > 0.0, l, 1.0), 0.0)
    return p * inv, p_new * inv


def _nsa_sample_sel_kernel(ids_ref, pt_ref, pool_ref, q_ref, sm_ref, kvs_ref, kvw_ref, win_ref, oc_ref, o_ref,
                           buf, sem, *, past):
    b = pl.program_id(0)
    n_pages = past // PAGE_ROWS
    last_blk = past // SLC_BLOCK
    blocks_per_page = PAGE_ROWS // SLC_BLOCK

    def page_copy(g, k, c):
        blk = jnp.minimum(ids_ref[(b * N_KV_HEADS + g) * N_SELECT + k], last_blk - 1)
        page = pt_ref[b * n_pages + blk // blocks_per_page]
        return pltpu.make_async_copy(pool_ref.at[page, c, g], buf.at[g, c, k], sem)

    copies = [(g, k, c) for g in range(N_KV_HEADS) for k in range(N_SELECT) for c in range(2)]
    for g, k, c in copies:
        page_copy(g, k, c).start()
    for g, k, c in copies:
        page_copy(g, k, c).wait()

    q = q_ref[pl.ds(b, 1), :]
    gates = sm_ref[pl.ds(b, 1), :]
    new_s = kvs_ref[pl.ds(b, 1), :]
    new_w = kvw_ref[pl.ds(b, 1), :]
    n_sel = N_SELECT * PAGE_ROWS
    lane = lax.broadcasted_iota(jnp.int32, (1, n_sel), 1)
    w_buf = win_ref.shape[1]
    d_w = w_buf - lax.broadcasted_iota(jnp.int32, (1, w_buf), 1)
    mask_w = jnp.broadcast_to((d_w < WINDOW) & (past - d_w >= 0), (_QROWS, w_buf))
    always = jnp.full((_QROWS, 1), True)
    o_c = oc_ref[pl.ds(b, 1), :]
    out_parts = []
    for g in range(N_KV_HEADS):
        kcol = slice(g * HEAD_DIM, (g + 1) * HEAD_DIM)
        vcol = slice((N_KV_HEADS + g) * HEAD_DIM, (N_KV_HEADS + g + 1) * HEAD_DIM)
        q16f = _group_queries_1(q, g)
        q16 = q16f.astype(bf16)
        slope = _group_slopes_1(g)

        blk_vec = jnp.zeros((1, n_sel), jnp.int32)
        for k in range(N_SELECT):
            blk_vec = jnp.where(lane // PAGE_ROWS == k, ids_ref[(b * N_KV_HEADS + g) * N_SELECT + k], blk_vec)
        kpos = (blk_vec // blocks_per_page) * PAGE_ROWS + lane % PAGE_ROWS
        d_s = past - kpos
        mask_s = jnp.broadcast_to((d_s >= 0) & (blk_vec < last_blk) & (kpos // SLC_BLOCK == blk_vec),
                                  (_QROWS, n_sel))
        has_new = jnp.max(jnp.where(blk_vec == last_blk, 1, 0), axis=1, keepdims=True) > 0
        kt = jnp.concatenate([buf[g, 0, k] for k in range(N_SELECT)], axis=1).astype(bf16)
        vt = jnp.concatenate([buf[g, 1, k] for k in range(N_SELECT)], axis=1).astype(bf16)
        s = jnp.dot(q16, kt, preferred_element_type=f32) - slope * d_s.astype(f32)
        s_new = jnp.sum(q16f * new_s[:, kcol], axis=1, keepdims=True)
        p, p_new = _softmax_with_new(s, mask_s, s_new, jnp.broadcast_to(has_new, (_QROWS, 1)))
        o_s = _dot_nt(p.astype(bf16), vt) + p_new * new_s[:, vcol]

        wb = win_ref[0].astype(bf16)
        s = _dot_nt(q16, wb[:, kcol]) - slope * d_w.astype(f32)
        s_new = jnp.sum(q16f * new_w[:, kcol], axis=1, keepdims=True)
        p, p_new = _softmax_with_new(s, mask_w, s_new, always)
        o_w = jnp.dot(p.astype(bf16), wb[:, vcol], preferred_element_type=f32) + p_new * new_w[:, vcol]

        for r in range(GQA_REP):
            head = GQA_REP * g + r
            hs = slice(head * HEAD_DIM, (head + 1) * HEAD_DIM)
            out_parts.append(gates[:, 3 * head:3 * head + 1] * o_c[:, hs]
                             + gates[:, 3 * head + 1:3 * head + 2] * o_s[r:r + 1]
                             + gates[:, 3 * head + 2:3 * head + 3] * o_w[r:r + 1])
    o_ref[pl.ds(b, 1), :] = jnp.concatenate(out_parts, axis=1)


def _nsa_sample_sel(ids, page_table, pool_slc, q, small, kvs_new, kvw_new, win_buf, o_c, past):
    nb = q.shape[0]
    full = lambda a: pl.BlockSpec(a.shape, lambda b, ids, pt: (0,) * a.ndim)
    return pl.pallas_call(
        functools.partial(_nsa_sample_sel_kernel, past=past),
        grid_spec=pltpu.PrefetchScalarGridSpec(
            num_scalar_prefetch=2,
            grid=(nb,),
            in_specs=[pl.BlockSpec(memory_space=pl.ANY), full(q), full(small), full(kvs_new), full(kvw_new),
                      pl.BlockSpec((1,) + win_buf.shape[1:], lambda b, ids, pt: (b, 0, 0)), full(o_c)],
            out_specs=pl.BlockSpec((nb, D_ATT), lambda b, ids, pt: (0, 0)),
            scratch_shapes=[pltpu.VMEM((N_KV_HEADS, 2, N_SELECT, HEAD_DIM, PAGE_ROWS), f32),
                            pltpu.SemaphoreType.DMA(())]),
        out_shape=jax.ShapeDtypeStruct((nb, D_ATT), f32),
        compiler_params=_cparams("arbitrary"),
        name="nsa_sample_sel",
    )(ids, page_table.reshape(-1), pool_slc, q, small, kvs_new, kvw_new, win_buf, o_c)


def _mixout_kernel(att_ref, mh_ref, mo_ref, x_ref, ga_ref, gm_ref, wo_ref, nf_ref, wq_ref, h_ref, xn_ref,
                   qp_ref):
    parts = []
    for h in range(N_ATT_HEADS):
        sl = slice(h * HEAD_DIM, (h + 1) * HEAD_DIM)
        parts.append(_rms(att_ref[:, sl], ga_ref[:, sl]))
    for h in range(N_MLSTM_HEADS):
        sl = slice(h * MLSTM_HEAD_DIM, (h + 1) * MLSTM_HEAD_DIM)
        parts.append(mo_ref[:, sl] * _rms(mh_ref[:, sl], gm_ref[:, sl]))
    cat = jnp.concatenate(parts, axis=1).astype(bf16)
    h1 = x_ref[...] + jnp.dot(cat, wo_ref[...], preferred_element_type=f32)
    h_ref[...] = h1
    xn = _rms(h1, nf_ref[...]).astype(bf16)
    xn_ref[...] = xn
    qp_ref[...] = jnp.dot(xn, wq_ref[...], preferred_element_type=f32)


def _mix_output(att, mh, mo, x, g_att, g_ml, w_out_bf, norm_ffn, w_q_bf, tm):
    n = x.shape[0]
    tok = lambda w: pl.BlockSpec((tm, w), lambda i: (i, 0))
    full = lambda a: pl.BlockSpec(a.shape, lambda i: (0,) * a.ndim)
    return pl.pallas_call(
        _mixout_kernel,
        grid=(n // tm,),
        in_specs=[tok(D_ATT), tok(D_MLSTM), tok(D_MLSTM), tok(D_MODEL), full(g_att), full(g_ml),
                  full(w_out_bf), full(norm_ffn), full(w_q_bf)],
        out_specs=[tok(D_MODEL), tok(D_MODEL), tok(D_MODEL)],
        out_shape=[jax.ShapeDtypeStruct((n, D_MODEL), f32), jax.ShapeDtypeStruct((n, D_MODEL), bf16),
                   jax.ShapeDtypeStruct((n, D_MODEL), f32)],
        compiler_params=_cparams("parallel"),
        name="mix_output",
    )(att, mh, mo, x, g_att, g_ml, w_out_bf, norm_ffn, w_q_bf)


def _topk_rows(s, k):
    n = s.shape[0]
    rows = lax.broadcasted_iota(jnp.int32, s.shape, 0)
    vals, idxs = [], []
    for _ in range(k):
        mx = jnp.max(s, axis=0, keepdims=True)
        idx = jnp.min(jnp.where(s == mx, rows, n), axis=0, keepdims=True)
        vals.append(mx)
        idxs.append(idx)
        s = jnp.where(rows == idx, NEG_INF, s)
    return jnp.concatenate(vals, axis=0), jnp.concatenate(idxs, axis=0)


def _peer_topk_kernel(qp_ref, sub_ref, ei_ref, ej_ref, g_ref):
    K = PEER_TOPK
    half = PEER_D_KEY // 2
    ei, ej, gg = [], [], []
    for h in range(PEER_HEADS):
        sv, si = [], []
        for c in range(2):
            qhc = qp_ref[:, (2 * h + c) * half:(2 * h + c + 1) * half]
            s = _dot_nt(sub_ref[h, c], qhc, precision=HIGHEST)
            v, i = _topk_rows(s, K)
            sv.append(v)
            si.append(i)
        n_t = sv[0].shape[1]
        counts = [K // (a + 1) for a in range(K)]
        n_cand = -(-sum(counts) // 8) * 8
        pad = n_cand - sum(counts)
        cand = jnp.concatenate([sv[0][a:a + 1] + sv[1][0:counts[a]] for a in range(K)]
                               + [jnp.full((pad, n_t), NEG_INF, f32)], axis=0)
        pos_i = jnp.concatenate([jnp.broadcast_to(si[0][a:a + 1], (counts[a], n_t)) for a in range(K)]
                                + [jnp.zeros((pad, n_t), jnp.int32)], axis=0)
        pos_j = jnp.concatenate([si[1][0:counts[a]] for a in range(K)] + [jnp.zeros((pad, n_t), jnp.int32)],
                                axis=0)
        rows = lax.broadcasted_iota(jnp.int32, cand.shape, 0)
        best, bi, bj = [], [], []
        for _ in range(K):
            mx = jnp.max(cand, axis=0, keepdims=True)
            pos = jnp.min(jnp.where(cand == mx, rows, n_cand), axis=0, keepdims=True)
            hit = rows == pos
            best.append(mx)
            bi.append(jnp.max(jnp.where(hit, pos_i, -1), axis=0, keepdims=True))
            bj.append(jnp.max(jnp.where(hit, pos_j, -1), axis=0, keepdims=True))
            cand = jnp.where(hit, NEG_INF, cand)
        best = jnp.concatenate(best, axis=0)
        e = jnp.exp(best - best[0:1])
        gg.append(e / jnp.sum(e, axis=0, keepdims=True))
        ei.append(jnp.concatenate(bi, axis=0))
        ej.append(jnp.concatenate(bj, axis=0))
    ei_ref[...] = jnp.concatenate(ei, axis=0).astype(f32).T
    ej_ref[...] = jnp.concatenate(ej, axis=0).astype(f32).T
    g_ref[...] = jnp.concatenate(gg, axis=0).T


def _peer_topk(qp, sub_keys, tm):
    n = qp.shape[0]
    hk = PEER_HEADS * PEER_TOPK
    return pl.pallas_call(
        _peer_topk_kernel,
        grid=(n // tm,),
        in_specs=[pl.BlockSpec((tm, D_MODEL), lambda i: (i, 0)),
                  pl.BlockSpec(sub_keys.shape, lambda i: (0, 0, 0, 0))],
        out_specs=[pl.BlockSpec((tm, hk), lambda i: (i, 0))] * 3,
        out_shape=[jax.ShapeDtypeStruct((n, hk), f32)] * 3,
        compiler_params=_cparams("parallel"),
        name="peer_topk",
    )(qp, sub_keys)


_GATE_UNROLL = 32
_GATE_PARTS = 2


def _peer_dense_kernel(ei_ref, ej_ref, gg_ref, x_ref, u_ref, v_ref, h_ref, nf_ref, o_ref, acc_ref, g_s):
    part, e = pl.program_id(1), pl.program_id(2)
    tm = x_ref.shape[0]
    et = u_ref.shape[0]
    nk = PEER_N_KEYS
    part_rows = nk // _GATE_PARTS

    @pl.when((part == 0) & (e == 0))
    def _():
        acc_ref[...] = jnp.zeros_like(acc_ref)

    @pl.when(e == 0)
    def _():
        i0 = (part * part_rows).astype(f32)
        sub_i = lax.broadcasted_iota(jnp.int32, (part_rows, ei_ref.shape[1]), 0).astype(f32) + i0
        sub_j = lax.broadcasted_iota(jnp.int32, (nk, ei_ref.shape[1]), 0).astype(f32)

        zero = jnp.zeros((part_rows, ei_ref.shape[1]), bf16)

        def body(tp, carry):
            t0 = pl.multiple_of(tp * 2, 2)
            ei, ej, gg = ei_ref[pl.ds(t0, 2), :], ej_ref[pl.ds(t0, 2), :], gg_ref[pl.ds(t0, 2), :]
            a = [jnp.where(sub_i == ei[u:u + 1], 1.0, 0.0).astype(bf16) for u in range(2)]
            b = [jnp.where(sub_j == ej[u:u + 1], gg[u:u + 1], 0.0).astype(bf16) for u in range(2)]
            lhs = jnp.concatenate([jnp.concatenate([a[0], zero], axis=1), jnp.concatenate([zero, a[1]], axis=1)],
                                  axis=0)
            g_s[pl.ds(pl.multiple_of(t0 * part_rows, 2 * part_rows), 2 * part_rows), :] = _dot_nt(
                lhs, jnp.concatenate(b, axis=1))
            return carry

        lax.fori_loop(0, tm // 2, body, 0, unroll=_GATE_UNROLL)

    rows = et // nk
    g = jnp.concatenate([g_s[pl.ds(e * rows + r, tm, stride=part_rows), :] for r in range(rows)], axis=1)
    act = jax.nn.gelu(_dot_nt(x_ref[...], u_ref[...]))
    acc_ref[...] += jnp.dot((g * act).astype(bf16), v_ref[...], preferred_element_type=f32)

    @pl.when((part == _GATE_PARTS - 1) & (e == pl.num_programs(2) - 1))
    def _():
        o_ref[...] = _rms(h_ref[...] + acc_ref[...], nf_ref[...])


def _peer_dense(ei, ej, gg, xn_bf, u_bf, v_bf, h1, norm_final, tm, et):
    n = xn_bf.shape[0]
    n_exp = u_bf.shape[0]
    hk = ei.shape[1]
    tok = lambda w: pl.BlockSpec((tm, w), lambda i, p, e: (i, 0))
    steps = n_exp // _GATE_PARTS // et
    expert_tile = pl.BlockSpec((et, D_MODEL), lambda i, p, e: (p * steps + e, 0))
    return pl.pallas_call(
        _peer_dense_kernel,
        grid=(n // tm, _GATE_PARTS, steps),
        in_specs=[tok(hk), tok(hk), tok(hk), tok(D_MODEL), expert_tile, expert_tile, tok(D_MODEL),
                  pl.BlockSpec((1, D_MODEL), lambda i, p, e: (0, 0))],
        out_specs=tok(D_MODEL),
        out_shape=jax.ShapeDtypeStruct((n, D_MODEL), f32),
        scratch_shapes=[pltpu.VMEM((tm, D_MODEL), f32),
                        pltpu.VMEM((tm * PEER_N_KEYS // _GATE_PARTS, PEER_N_KEYS), f32)],
        compiler_params=_cparams("parallel", "arbitrary", "arbitrary"),
        name="peer_dense",
    )(ei, ej, gg, xn_bf, u_bf, v_bf, h1, norm_final)


def _channel_mix_and_norm(h1, xn_bf, qp, sub_keys, u_bf, v_bf, norm_final, tm_topk, tm, et):
    ei, ej, gg = _peer_topk(qp, sub_keys, tm_topk)
    return _peer_dense(ei, ej, gg, xn_bf, u_bf, v_bf, h1, norm_final, tm, et)


PROMPT_TM = 512
PEER_TM = 512
PEER_ET = 2048
MLSTM_CHUNK = 256
SAMPLE_PAD = 128


def kernel(x_prompt, x_sample, cache_cmp_kv, cache_slc_kv, cache_win_kv, state_mlstm_C, state_mlstm_n,
           state_mlstm_m, page_table, norm_mix, w_in, b_igate, b_fgate, pe_cmp, w_cmp1, w_cmp2, norm_att_out,
           norm_mlstm_out, w_out, norm_ffn, peer_wq, peer_subkeys, peer_u, peer_v, norm_final):
    assert x_prompt.shape[0] == 1 and x_sample.shape[1] == 1 and w_in.shape[0] == 1
    _, t, d = x_prompt.shape
    nb = x_sample.shape[0]
    n_pool = cache_cmp_kv.shape[1]
    past = page_table.shape[1] * PAGE_ROWS
    w_buf = cache_win_kv.shape[2]
    row = (2, N_KV_HEADS, HEAD_DIM)
    NH, DH = N_MLSTM_HEADS, MLSTM_HEAD_DIM
    g0 = N_GATE_COLS

    w_re = _relayout_w_in(w_in[0])
    bias = _bias_row(b_igate[0], b_fgate[0])
    cw = _compress_weights(pe_cmp[0], w_cmp1[0], w_cmp2[0])
    nm = norm_mix[0][None]
    ga, gm, nf, nfin = norm_att_out[0][None], norm_mlstm_out[0][None], norm_ffn[0][None], norm_final[None]
    w_out_bf, w_q_bf = w_out[0].astype(bf16), peer_wq[0].astype(bf16)
    u_bf, v_bf = peer_u[0].astype(bf16), peer_v[0].astype(bf16)
    sub_keys = peer_subkeys[0]

    xp = x_prompt.reshape(t, d)
    q, kvc, kvs, kvw, mq, mk, mv, mo, sm = _project(xp, nm, w_re, bias, PROMPT_TM)
    cmp_p = _compress_prompt(kvc[None], cw)[0]
    att = _nsa_prompt_t(q, sm, cmp_p, kvs, kvw).T
    gates_c = sm[:, g0:g0 + 2 * NH]
    mh, c_p, n_p, m_p = _mlstm_prompt(mq, mk, mv, gates_c, gates_c.T, MLSTM_CHUNK)
    h1, xn_bf, qp = _mix_output(att, mh, mo, xp, ga, gm, w_out_bf, nf, w_q_bf, 256)
    y_p = _channel_mix_and_norm(h1, xn_bf, qp, sub_keys, u_bf, v_bf, nfin, 256, PEER_TM, PEER_ET)
    w_keep = min(WINDOW, t)
    outs_p = (kvc.reshape((1, 1, t) + row), kvs.reshape((1, 1, t) + row),
              kvw[t - w_keep:].reshape((1, 1, w_keep) + row),
              c_p[None, None], n_p[:NH][None, None], m_p[:NH, 0][None, None])

    xs = x_sample.reshape(nb, d)
    q, kvc_s, kvs_s, kvw_s, mq, mk, mv, mo, sm = _project(xs, nm, w_re, bias, nb)
    cmp_s = _compress_paged(page_table, jnp.transpose(cache_cmp_kv[0], (0, 2, 3, 4, 1)), cw)
    pool_s = _pool_matrix(past // CMP_STRIDE, past // SLC_BLOCK)
    o_c, ids = _nsa_sample_cmp(q, cmp_s, pool_s, past)
    win_buf = cache_win_kv[0].reshape(nb, w_buf, KV_COLS)
    att = _nsa_sample_sel(ids[:, :, :N_SELECT].reshape(-1), page_table,
                          jnp.transpose(cache_slc_kv[0], (0, 2, 3, 4, 1)), q, sm, kvs_s, kvw_s, win_buf, o_c, past)
    mh, c_s, n_s, m_s = _mlstm_sample(mq, mk, mv, sm[:, g0:g0 + 2 * NH], state_mlstm_C[0], state_mlstm_n[0],
                                      state_mlstm_m[0])
    h1, xn_bf, qp = _mix_output(att, mh, mo, xs, ga, gm, w_out_bf, nf, w_q_bf, nb)
    padr = lambda a: jnp.pad(a, ((0, SAMPLE_PAD - nb), (0, 0)))
    y_s = _channel_mix_and_norm(padr(h1), padr(xn_bf), padr(qp), sub_keys, u_bf, v_bf, nfin, SAMPLE_PAD,
                                SAMPLE_PAD, PEER_ET)[:nb]
    win_all = jnp.concatenate([win_buf, kvw_s[:, None, :]], axis=1)
    w_keep_s = min(WINDOW, w_buf + 1)
    outs_s = (kvc_s.reshape((1, nb, 1) + row), kvs_s.reshape((1, nb, 1) + row),
              win_all[:, w_buf + 1 - w_keep_s:].reshape((1, nb, w_keep_s) + row),
              c_s[None], n_s[None], m_s[None])

    return (y_p.reshape(1, t, d), y_s.reshape(nb, 1, d),
            outs_p[0], outs_s[0], outs_p[1], outs_s[1], outs_p[2], outs_s[2],
            outs_p[3], outs_s[3], outs_p[4], outs_s[4], outs_p[5], outs_s[5])
```
